```python
import math, functools
import jax, jax.numpy as jnp
from jax import lax
import numpy as np

D_MODEL = 1024
BATCH = 16
SEQ = 256
DEPTH = 2
DEC_BATCH = 8
DEC_SEQ = 2048
PAST_LEN = 512

GRID_W = 64
D_MIX = D_MODEL
D_A = D_MIX // 4
A_HEADS = 4
A_HEAD_DIM = D_A // A_HEADS
CHUNK = 128
D_B = D_MIX // 4
B_GROUPS = 4
B_GROUP_DIM = D_B // B_GROUPS
D_C = D_MIX - D_A - D_B
C_HEAD_DIM = 64
C_HEADS = D_C // C_HEAD_DIM
P_IN = 2 * D_A + D_B + 3 * D_C
NA_ROWS = 8
NA_KW = 16
NA_QB = 16
NA_BAND = NA_QB + NA_KW
ATTN_QBLK = 128
D_FF = 2816
CONV_W = 3
ALPHA = (2 * DEPTH) ** 0.25
BETA = (8 * DEPTH) ** -0.25
LN_EPS = 1e-6

kernel_name = "hybrid_gmlp_fnet_natten_diffusion_step"


def _norm(x):
    xf = x.astype(jnp.float32)
    mu = xf.mean(-1, keepdims=True)
    var = jnp.square(xf - mu).mean(-1, keepdims=True)
    return ((xf - mu) * lax.rsqrt(var + LN_EPS)).astype(x.dtype)


def _affine_ln(x, g, b):
    return _norm(x) * g + b


def _chunk_gmlp(z, ws, bs):
    B, L, _ = z.shape
    z = jax.nn.gelu(z)
    u, v = jnp.split(z, 2, axis=-1)
    v = _norm(v).reshape(B, L // CHUNK, CHUNK, A_HEADS, A_HEAD_DIM)
    v = jnp.einsum('hpq,bnqhd->bnphd', ws, v) + bs.T[None, None, :, :, None]
    return u * v.reshape(B, L, D_A)


def _fourier(z, fw):
    B, L, _ = z.shape
    zg = z.reshape(B, L, B_GROUPS, B_GROUP_DIM).astype(jnp.float32)
    zf = jnp.fft.fft2(zg, axes=(1, 3), norm='ortho').real.astype(z.dtype)
    return jnp.einsum('blgc,gcd->blgd', zf, fw).reshape(B, L, D_B)


def _conv_ffn(h, w_gu, conv_w, conv_b, w_down):
    g, u = jnp.split(h @ w_gu, 2, axis=-1)
    L = g.shape[1]
    pad = CONV_W // 2
    gp = jnp.pad(g, ((0, 0), (pad, pad), (0, 0)))
    gc = conv_b
    for t in range(CONV_W):
        gc = gc + gp[:, t:t + L] * conv_w[t]
    return (jax.nn.silu(gc) * u) @ w_down


def _context_attention(q, k, v):
    B, L, H, d = q.shape
    scale = d ** -0.5
    qb = q.reshape(B, L // ATTN_QBLK, ATTN_QBLK, H, d).transpose(1, 0, 2, 3, 4)

    def blk(qi):
        s = jnp.einsum('bqhd,bkhd->bhqk', qi, k).astype(jnp.float32) * scale
        p = jax.nn.softmax(s, axis=-1).astype(v.dtype)
        return jnp.einsum('bhqk,bkhd->bqhd', p, v)

    o = lax.map(blk, qb)
    return o.transpose(1, 0, 2, 3, 4).reshape(B, L, H * d)


def _neighbourhood_attention(q, k, v, kc, vc, rpb):
    B, L, H, d = q.shape
    R = L // GRID_W
    kh = min(NA_ROWS, R)
    scale = d ** -0.5
    nb = GRID_W // NA_QB
    cols = np.arange(GRID_W)
    col_start = np.clip(cols - NA_KW // 2, 0, GRID_W - NA_KW).reshape(nb, NA_QB)
    band_start = np.clip(np.arange(nb) * NA_QB - NA_KW // 2, 0, GRID_W - NA_BAND)
    band_cols = band_start[:, None] + np.arange(NA_BAND)
    qcols = cols.reshape(nb, NA_QB)
    kcols = band_cols[:, None, :]
    in_win = (kcols >= col_start[:, :, None]) & (kcols < col_start[:, :, None] + NA_KW)
    col_idx = np.clip(kcols - qcols[:, :, None] + NA_KW - 1, 0, 2 * NA_KW - 2)
    mask = in_win[:, :, None, :]
    qg = q.reshape(B, R, GRID_W, H, d)
    kg = k.reshape(B, R, GRID_W, H, d)
    vg = v.reshape(B, R, GRID_W, H, d)
    n_loc = kh * NA_BAND

    def row(r):
        rs = jnp.clip(r - kh // 2, 0, R - kh)
        kr = lax.dynamic_slice_in_dim(kg, rs, kh, axis=1)
        vr = lax.dynamic_slice_in_dim(vg, rs, kh, axis=1)
        kb = kr[:, :, band_cols]
        vb = vr[:, :, band_cols]
        qr = lax.dynamic_index_in_dim(qg, r, axis=1, keepdims=False).reshape(B, nb, NA_QB, H, d)
        row_idx = rs + jnp.arange(kh) - r + NA_ROWS - 1
        bias = rpb[:, row_idx][:, :, col_idx].transpose(0, 2, 3, 1, 4)
        s_loc = jnp.einsum('bjqhd,bkjwhd->bhjqkw', qr, kb).astype(jnp.float32) * scale
        s_loc = jnp.where(mask, s_loc + bias.astype(jnp.float32)[None], -jnp.inf)
        s_loc = s_loc.reshape(B, H, nb, NA_QB, n_loc)
        s_ctx = jnp.einsum('bjqhd,bchd->bhjqc', qr, kc).astype(jnp.float32) * scale
        p = jax.nn.softmax(jnp.concatenate([s_loc, s_ctx], axis=-1), axis=-1).astype(v.dtype)
        p_loc = p[..., :n_loc].reshape(B, H, nb, NA_QB, kh, NA_BAND)
        p_ctx = p[..., n_loc:]
        o = (jnp.einsum('bhjqkw,bkjwhd->bjqhd', p_loc, vb)
             + jnp.einsum('bhjqc,bchd->bjqhd', p_ctx, vc))
        return o.reshape(B, GRID_W, H * d)

    o = lax.map(row, jnp.arange(R))
    return o.transpose(1, 0, 2, 3).reshape(B, L, H * d)


def _layer(x, mod, w_in, a_ws, a_bs, f_w, w_out, ln1_g, ln1_b,
           w_gu, conv_w, conv_b, w_down, ln2_g, ln2_b, attend):
    shift1, scale1, gate1, shift2, scale2, gate2 = jnp.split(mod, 6, axis=-1)
    B, L, _ = x.shape
    h = _norm(x) * (1 + scale1) + shift1
    z = h @ w_in
    o1 = 2 * D_A
    o2 = o1 + D_B
    o3 = o2 + D_C
    o4 = o3 + D_C
    za, zb, q, k, v = jnp.split(z, [o1, o2, o3, o4], axis=-1)
    q = q.reshape(B, L, C_HEADS, C_HEAD_DIM)
    k = k.reshape(B, L, C_HEADS, C_HEAD_DIM)
    v = v.reshape(B, L, C_HEADS, C_HEAD_DIM)
    y = jnp.concatenate([_chunk_gmlp(za, a_ws, a_bs), _fourier(zb, f_w), attend(q, k, v)], axis=-1) @ w_out
    x = _affine_ln(ALPHA * x + gate1 * y, ln1_g, ln1_b)
    h = _norm(x) * (1 + scale2) + shift2
    x = _affine_ln(ALPHA * x + gate2 * _conv_ffn(h, w_gu, conv_w, conv_b, w_down), ln2_g, ln2_b)
    return x, k, v


def setup_inputs(seed: int = 0) -> dict:
    key = jax.random.key(seed)
    ks = jax.random.split(key, 24)
    n = lambda i, shape, s=1.0: jax.random.normal(ks[i], shape, jnp.float32) * s
    return {
        'x_prompt': n(0, (BATCH, SEQ, D_MODEL)),
        'x_sample': n(1, (DEC_BATCH, DEC_SEQ, D_MODEL)),
        'cache_k': n(2, (DEC_BATCH, DEPTH, PAST_LEN, C_HEADS, C_HEAD_DIM)),
        'cache_v': n(3, (DEC_BATCH, DEPTH, PAST_LEN, C_HEADS, C_HEAD_DIM)),
        'c': n(4, (DEC_BATCH, D_MODEL)),
        'c_ctx': n(5, (D_MODEL,)),
        'w_ada': n(6, (DEPTH, D_MODEL, 6 * D_MODEL), 0.5 * D_MODEL ** -0.5),
        'b_ada': n(7, (DEPTH, 6 * D_MODEL), 0.01),
        'w_in': n(8, (DEPTH, D_MODEL, P_IN), D_MODEL ** -0.5),
        'a_ws': n(9, (DEPTH, A_HEADS, CHUNK, CHUNK), CHUNK ** -0.5),
        'a_bs': 1.0 + n(10, (DEPTH, A_HEADS, CHUNK), 0.1),
        'f_w': n(11, (DEPTH, B_GROUPS, B_GROUP_DIM, B_GROUP_DIM), B_GROUP_DIM ** -0.5),
        'rpb': n(12, (DEPTH, C_HEADS, 2 * NA_ROWS - 1, 2 * NA_KW - 1), 0.5),
        'w_out': n(13, (DEPTH, D_MIX, D_MODEL), BETA * D_MIX ** -0.5),
        'ln1_g': 1.0 + n(14, (DEPTH, D_MODEL), 0.05),
        'ln1_b': n(15, (DEPTH, D_MODEL), 0.01),
        'w_gu': n(16, (DEPTH, D_MODEL, 2 * D_FF), D_MODEL ** -0.5),
        'conv_w': n(17, (DEPTH, CONV_W, D_FF), CONV_W ** -0.5),
        'conv_b': n(18, (DEPTH, D_FF), 0.01),
        'w_down': n(19, (DEPTH, D_FF, D_MODEL), BETA * D_FF ** -0.5),
        'ln2_g': 1.0 + n(20, (DEPTH, D_MODEL), 0.05),
        'ln2_b': n(21, (DEPTH, D_MODEL), 0.01),
    }


def reference(x_prompt, x_sample, cache_k, cache_v, c, c_ctx, w_ada, b_ada, w_in, a_ws, a_bs,
              f_w, rpb, w_out, ln1_g, ln1_b, w_gu, conv_w, conv_b, w_down, ln2_g, ln2_b):
    xp = x_prompt
    xs = x_sample
    new_ks = []
    new_vs = []
    for l in range(DEPTH):
        shared = (w_in[l], a_ws[l], a_bs[l], f_w[l], w_out[l], ln1_g[l], ln1_b[l],
                  w_gu[l], conv_w[l], conv_b[l], w_down[l], ln2_g[l], ln2_b[l])
        mod_ctx = (jax.nn.silu(c_ctx) @ w_ada[l] + b_ada[l])[None, None, :]
        xp, k_l, v_l = _layer(xp, mod_ctx, *shared, attend=_context_attention)
        new_ks.append(k_l)
        new_vs.append(v_l)
        mod_lat = (jax.nn.silu(c) @ w_ada[l] + b_ada[l])[:, None, :]
        attend_lat = functools.partial(_neighbourhood_attention, kc=cache_k[:, l], vc=cache_v[:, l], rpb=rpb[l])
        xs, _, _ = _layer(xs, mod_lat, *shared, attend=attend_lat)
    new_k = jnp.stack(new_ks, axis=1)
    new_v = jnp.stack(new_vs, axis=1)
    return (xp, xs, new_k, new_v)
```

```python
import functools

import numpy as np
import jax
import jax.numpy as jnp
from jax import lax
from jax.experimental import pallas as pl
from jax.experimental.pallas import tpu as pltpu

D_MODEL = 1024
DEPTH = 2
GRID_W = 64
D_A = 256
A_HEADS = 4
A_HEAD_DIM = 64
CHUNK = 128
D_B = 256
B_GROUPS = 4
B_GROUP_DIM = 64
D_C = 512
C_HEAD_DIM = 64
C_HEADS = 8
P_IN = 2 * D_A + D_B + 3 * D_C
NA_ROWS = 8
NA_KW = 16
D_FF = 2816
ALPHA = (2 * DEPTH) ** 0.25
LN_EPS = 1e-6

HEAD_PAIR = 2 * C_HEAD_DIM
N_HEAD_PAIRS = C_HEADS // 2
ROW_TILE = 256
FF_CHUNK = 256
HALO = 16
VMEM_LIMIT = 56 * 1024 * 1024

_F32 = jnp.float32
_BF16 = jnp.bfloat16


def _dot(a, b):
    return jnp.dot(a, b, preferred_element_type=_F32)


def _dot_nt(a, b):
    return lax.dot_general(a, b, (((1,), (1,)), ((), ())), preferred_element_type=_F32)


def _norm(x):
    mu = jnp.mean(x, axis=-1, keepdims=True)
    xc = x - mu
    var = jnp.mean(xc * xc, axis=-1, keepdims=True)
    return xc * lax.rsqrt(var + LN_EPS)


def _params(sem):
    return pltpu.CompilerParams(dimension_semantics=sem, vmem_limit_bytes=VMEM_LIMIT)


def _const_spec(shape):
    nd = len(shape)
    return pl.BlockSpec(shape, lambda *_: (0,) * nd, pipeline_mode=pl.Buffered(1))


def _mod_kernel(c_ref, w_ref, b_ref, o_ref):
    c = c_ref[...]
    s = c * jax.nn.sigmoid(c)
    o_ref[0] = _dot(s.astype(_BF16), w_ref[0].astype(_BF16)) + b_ref[0]


def _modulation(cs, w_ada, b_ada):
    n = cs.shape[0]
    tn = 1536
    return pl.pallas_call(
        _mod_kernel,
        out_shape=jax.ShapeDtypeStruct((DEPTH, n, 6 * D_MODEL), _F32),
        grid=(DEPTH, 6 * D_MODEL // tn),
        in_specs=[
            pl.BlockSpec((n, D_MODEL), lambda l, j: (0, 0)),
            pl.BlockSpec((1, D_MODEL, tn), lambda l, j: (l, 0, j)),
            pl.BlockSpec((1, 1, tn), lambda l, j: (l, 0, j)),
        ],
        out_specs=pl.BlockSpec((1, n, tn), lambda l, j: (l, 0, j)),
        compiler_params=_params(("parallel", "parallel")),
        name="adaln_mod",
    )(cs, w_ada, b_ada.reshape(DEPTH, 1, 6 * D_MODEL))


def _in_kernel(x_ref, mod_ref, w_ref, cc_ref, cs_ref,
               u_ref, vn_ref, zc_ref, zs_ref, q_ref, k_ref, v_ref):
    x = x_ref[0]
    m = mod_ref[0]
    h = _norm(x) * (1.0 + m[1:2]) + m[0:1]
    z = _dot(h.astype(_BF16), w_ref[...])
    za = jax.nn.gelu(z[:, :2 * D_A])
    u_ref[0] = za[:, :D_A]
    vn_ref[0] = _norm(za[:, D_A:]).astype(_BF16)
    o1 = 2 * D_A
    zb = z[:, o1:o1 + D_B].astype(_BF16)
    zc_ref[...] = _dot(zb, cc_ref[...]).astype(_BF16)
    zs_ref[...] = _dot(zb, cs_ref[...]).astype(_BF16)
    o2 = o1 + D_B
    q_ref[0] = (z[:, o2:o2 + D_C] * (C_HEAD_DIM ** -0.5)).astype(q_ref.dtype)
    k_ref[0] = z[:, o2 + D_C:o2 + 2 * D_C].astype(k_ref.dtype)
    v_ref[0] = z[:, o2 + 2 * D_C:o2 + 3 * D_C].astype(v_ref.dtype)


def _in_proj(x, mod, w_in, ch_cos, ch_sin, kv_dtype):
    B, L, _ = x.shape
    tm = ROW_TILE
    nmod = mod.shape[0]
    mod_idx = (lambda b, i: (b, 0, 0)) if nmod > 1 else (lambda b, i: (0, 0, 0))
    row = lambda w, dt: jax.ShapeDtypeStruct((B, L, w), dt)
    row_spec = lambda w: pl.BlockSpec((1, tm, w), lambda b, i: (b, i, 0))
    four = jax.ShapeDtypeStruct((L, B * D_B), _BF16)
    four_spec = pl.BlockSpec((tm, D_B), lambda b, i: (i, b))
    return pl.pallas_call(
        _in_kernel,
        out_shape=(row(D_A, _F32), row(D_A, _BF16), four, four,
                   row(D_C, _BF16), row(D_C, kv_dtype), row(D_C, kv_dtype)),
        grid=(B, L // tm),
        in_specs=[
            row_spec(D_MODEL),
            pl.BlockSpec((1, 6, D_MODEL), mod_idx),
            _const_spec((D_MODEL, P_IN)),
            _const_spec((D_B, D_B)),
            _const_spec((D_B, D_B)),
        ],
        out_specs=(row_spec(D_A), row_spec(D_A), four_spec, four_spec,
                   row_spec(D_C), row_spec(D_C), row_spec(D_C)),
        compiler_params=_params(("parallel", "parallel")),
        name="in_proj",
    )(x, mod, w_in, ch_cos, ch_sin)


def _dft_kernel(c_ref, s_ref, zc_ref, zs_ref, o_ref):
    o_ref[...] = (_dot(c_ref[...], zc_ref[...]) - _dot(s_ref[...], zs_ref[...])).astype(o_ref.dtype)


def _pos_dft(pos_cos, pos_sin, zc, zs):
    L, N = zc.shape
    ti = min(L, 512)
    tj = 512 if L > 512 else 1024
    return pl.pallas_call(
        _dft_kernel,
        out_shape=jax.ShapeDtypeStruct((L, N), _BF16),
        grid=(L // ti, N // tj),
        in_specs=[
            pl.BlockSpec((ti, L), lambda i, j: (i, 0)),
            pl.BlockSpec((ti, L), lambda i, j: (i, 0)),
            pl.BlockSpec((L, tj), lambda i, j: (0, j)),
            pl.BlockSpec((L, tj), lambda i, j: (0, j)),
        ],
        out_specs=pl.BlockSpec((ti, tj), lambda i, j: (i, j)),
        compiler_params=_params(("parallel", "parallel")),
        name="pos_dft",
    )(pos_cos, pos_sin, zc, zs)


def _head_lane_mask(rows, head):
    lane = lax.broadcasted_iota(jnp.int32, (rows, HEAD_PAIR), 1)
    return (lane >= head * C_HEAD_DIM) & (lane < (head + 1) * C_HEAD_DIM)


def _ctx_attn_kernel(q_ref, k_ref, v_ref, o_ref):
    q = q_ref[0]
    k = k_ref[0].astype(_BF16)
    v = v_ref[0].astype(_BF16)
    rows = q.shape[0]
    out = jnp.zeros((rows, HEAD_PAIR), _F32)
    for head in range(2):
        sel = _head_lane_mask(rows, head)
        s = _dot_nt(jnp.where(sel, q, jnp.zeros_like(q)), k)
        p = jnp.exp(s - jnp.max(s, axis=-1, keepdims=True))
        o = _dot(p.astype(_BF16), v) / jnp.sum(p, axis=-1, keepdims=True)
        out = jnp.where(sel, o, out)
    o_ref[0] = out.astype(o_ref.dtype)


def _ctx_attention(q, k, v):
    B, L, _ = q.shape
    spec = pl.BlockSpec((1, L, HEAD_PAIR), lambda b, p: (b, 0, p))
    return pl.pallas_call(
        _ctx_attn_kernel,
        out_shape=jax.ShapeDtypeStruct((B, L, D_C), _BF16),
        grid=(B, N_HEAD_PAIRS),
        in_specs=[spec, spec, spec],
        out_specs=spec,
        compiler_params=_params(("parallel", "parallel")),
        name="ctx_attention",
    )(q, k, v)


def _na_kernel(q_ref, k_ref, v_ref, kc_ref, vc_ref, bias_ref, o_ref, *, n_rows):
    kh = min(NA_ROWS, n_rows)
    kc = kc_ref[0]
    vc = vc_ref[0]

    def row_body(r, carry):
        rs = jnp.clip(r - kh // 2, 0, n_rows - kh)
        base = rs - r + NA_ROWS - 1
        q = q_ref[0, pl.ds(pl.multiple_of(r * GRID_W, GRID_W), GRID_W), :]
        k_start = pl.multiple_of(rs * GRID_W, GRID_W)
        kl = k_ref[0, pl.ds(k_start, kh * GRID_W), :]
        vl = v_ref[0, pl.ds(k_start, kh * GRID_W), :]
        out = jnp.zeros((GRID_W, HEAD_PAIR), _F32)
        for head in range(2):
            sel = _head_lane_mask(GRID_W, head)
            qh = jnp.where(sel, q, jnp.zeros_like(q))
            s_loc = _dot_nt(qh, kl) + bias_ref[head, base]
            s_ctx = _dot_nt(qh, kc)
            m = jnp.maximum(jnp.max(s_loc, axis=-1, keepdims=True),
                            jnp.max(s_ctx, axis=-1, keepdims=True))
            p_loc = jnp.exp(s_loc - m)
            p_ctx = jnp.exp(s_ctx - m)
            denom = jnp.sum(p_loc, axis=-1, keepdims=True) + jnp.sum(p_ctx, axis=-1, keepdims=True)
            o = (_dot(p_loc.astype(_BF16), vl) + _dot(p_ctx.astype(_BF16), vc)) / denom
            out = jnp.where(sel, o, out)
        o_ref[0, pl.ds(pl.multiple_of(r * GRID_W, GRID_W), GRID_W), :] = out.astype(o_ref.dtype)
        return carry

    lax.fori_loop(0, n_rows, row_body, 0)


def _na_bias_table(rpb_l, n_rows):
    kh = min(NA_ROWS, n_rows)
    qc = np.arange(GRID_W)[:, None]
    kc = np.arange(GRID_W)[None, :]
    start = np.clip(qc - NA_KW // 2, 0, GRID_W - NA_KW)
    in_win = (kc >= start) & (kc < start + NA_KW)
    col_idx = np.clip(kc - qc + NA_KW - 1, 0, 2 * NA_KW - 2)
    g = jnp.where(in_win[None, None], rpb_l[:, :, col_idx], -jnp.inf)
    tabs = []
    for base in range(NA_ROWS):
        t = g[:, base:base + kh]
        tabs.append(t.transpose(0, 2, 1, 3).reshape(C_HEADS, GRID_W, kh * GRID_W))
    return jnp.stack(tabs, axis=1).astype(_F32)


def _na_attention(q, k, v, kc, vc, bias):
    B, L, _ = q.shape
    Lc = kc.shape[1]
    n_rows = L // GRID_W
    kh = min(NA_ROWS, n_rows)
    lat = pl.BlockSpec((1, L, HEAD_PAIR), lambda p, b: (b, 0, p))
    ctx = pl.BlockSpec((1, Lc, HEAD_PAIR), lambda p, b: (b, 0, p))
    return pl.pallas_call(
        functools.partial(_na_kernel, n_rows=n_rows),
        out_shape=jax.ShapeDtypeStruct((B, L, D_C), _BF16),
        grid=(N_HEAD_PAIRS, B),
        in_specs=[lat, lat, lat, ctx, ctx,
                  pl.BlockSpec((2, NA_ROWS, GRID_W, kh * GRID_W), lambda p, b: (p, 0, 0, 0))],
        out_specs=lat,
        compiler_params=_params(("parallel", "parallel")),
        name="na_attention",
    )(q, k, v, kc, vc, bias)


def _out_kernel(x_ref, u_ref, vn_ref, f_ref, a_ref, mod_ref, ws_ref, bs_ref, fw_ref, wo_ref,
                g_ref, b_ref, x1_ref, h2_ref):
    tm = x_ref.shape[1]
    lane = lax.broadcasted_iota(jnp.int32, (CHUNK, D_A), 1)
    ya = []
    for n in range(tm // CHUNK):
        rows = slice(n * CHUNK, (n + 1) * CHUNK)
        r = _dot(ws_ref[...], vn_ref[0, rows, :])
        mixed = r[:CHUNK]
        for hd in range(1, A_HEADS):
            mixed = jnp.where(lane >= hd * A_HEAD_DIM, r[hd * CHUNK:(hd + 1) * CHUNK], mixed)
        ya.append(u_ref[0, rows, :] * (mixed + bs_ref[...]))
    ya = jnp.concatenate(ya, axis=0).astype(_BF16)
    yb = _dot(f_ref[...], fw_ref[...]).astype(_BF16)
    y = (_dot(ya, wo_ref[:D_A, :]) + _dot(yb, wo_ref[D_A:D_A + D_B, :])
         + _dot(a_ref[0], wo_ref[D_A + D_B:, :]))
    m = mod_ref[0]
    x1 = _norm(ALPHA * x_ref[0] + m[2:3] * y) * g_ref[...] + b_ref[...]
    x1_ref[0] = x1
    h2_ref[0] = (_norm(x1) * (1.0 + m[4:5]) + m[3:4]).astype(_BF16)


def _out_proj(x, u, vn, four, attn, mod, ws, bs, fw, w_out, ln_g, ln_b):
    B, L, _ = x.shape
    tm = ROW_TILE
    nmod = mod.shape[0]
    mod_idx = (lambda b, i: (b, 0, 0)) if nmod > 1 else (lambda b, i: (0, 0, 0))
    row_spec = lambda w: pl.BlockSpec((1, tm, w), lambda b, i: (b, i, 0))
    return pl.pallas_call(
        _out_kernel,
        out_shape=(jax.ShapeDtypeStruct((B, L, D_MODEL), _F32),
                   jax.ShapeDtypeStruct((B, L, D_MODEL), _BF16)),
        grid=(B, L // tm),
        in_specs=[
            row_spec(D_MODEL), row_spec(D_A), row_spec(D_A),
            pl.BlockSpec((tm, D_B), lambda b, i: (i, b)),
            row_spec(D_C),
            pl.BlockSpec((1, 6, D_MODEL), mod_idx),
            _const_spec((A_HEADS * CHUNK, CHUNK)),
            _const_spec((CHUNK, D_A)),
            _const_spec((D_B, D_B)),
            _const_spec((D_MODEL, D_MODEL)),
            _const_spec((1, D_MODEL)),
            _const_spec((1, D_MODEL)),
        ],
        out_specs=(row_spec(D_MODEL), row_spec(D_MODEL)),
        compiler_params=_params(("parallel", "parallel")),
        name="out_proj",
    )(x, u, vn, four, attn, mod, ws, bs, fw, w_out, ln_g, ln_b)


def _ffn_kernel(h_ref, hp_ref, hn_ref, x_ref, mod_ref, wg_ref, wu_ref, cw_ref, cb_ref, wd_ref,
                g_ref, b_ref, o_ref, hcat_ref):
    tm = h_ref.shape[1]
    i = pl.program_id(1)
    hcat_ref[:HALO, :] = hp_ref[0]
    hcat_ref[HALO:HALO + tm, :] = h_ref[0]
    hcat_ref[HALO + tm:, :] = hn_ref[0]
    ext = tm + 2 * HALO
    row = lax.broadcasted_iota(jnp.int32, (ext, 1), 0)
    first_row = jnp.where(i > 0, 0, HALO)
    end_row = jnp.where(i < pl.num_programs(1) - 1, ext, HALO + tm)
    inside = (row >= first_row) & (row < end_row)
    acc = jnp.zeros((tm, D_MODEL), _F32)
    for j in range(D_FF // FF_CHUNK):
        cols = slice(j * FF_CHUNK, (j + 1) * FF_CHUNK)
        g = jnp.where(inside, _dot(hcat_ref[...], wg_ref[:, cols]), 0.0)
        g_prev = pltpu.roll(g, 1, 0)[HALO:HALO + tm]
        g_next = pltpu.roll(g, ext - 1, 0)[HALO:HALO + tm]
        gc = (cb_ref[:, cols] + g_prev * cw_ref[0:1, cols] + g[HALO:HALO + tm] * cw_ref[1:2, cols]
              + g_next * cw_ref[2:3, cols])
        up = _dot(h_ref[0], wu_ref[:, cols])
        act = (gc * jax.nn.sigmoid(gc)) * up
        acc = acc + _dot(act.astype(_BF16), wd_ref[cols, :])
    m = mod_ref[0]
    o_ref[0] = _norm(ALPHA * x_ref[0] + m[5:6] * acc) * g_ref[...] + b_ref[...]


def _conv_ffn(h2, x1, mod, w_g, w_u, conv_w, conv_b, w_down, ln_g, ln_b):
    B, L, _ = x1.shape
    tm = min(L, 512)
    nmod = mod.shape[0]
    mod_idx = (lambda b, i: (b, 0, 0)) if nmod > 1 else (lambda b, i: (0, 0, 0))
    row_spec = pl.BlockSpec((1, tm, D_MODEL), lambda b, i: (b, i, 0))
    per = tm // HALO
    last = L // HALO - 1
    prev_spec = pl.BlockSpec((1, HALO, D_MODEL), lambda b, i: (b, jnp.maximum(i * per - 1, 0), 0))
    next_spec = pl.BlockSpec((1, HALO, D_MODEL), lambda b, i: (b, jnp.minimum((i + 1) * per, last), 0))
    return pl.pallas_call(
        _ffn_kernel,
        out_shape=jax.ShapeDtypeStruct((B, L, D_MODEL), _F32),
        grid=(B, L // tm),
        in_specs=[
            row_spec, prev_spec, next_spec, row_spec,
            pl.BlockSpec((1, 6, D_MODEL), mod_idx),
            _const_spec((D_MODEL, D_FF)),
            _const_spec((D_MODEL, D_FF)),
            _const_spec((3, D_FF)),
            _const_spec((1, D_FF)),
            _const_spec((D_FF, D_MODEL)),
            _const_spec((1, D_MODEL)),
            _const_spec((1, D_MODEL)),
        ],
        out_specs=row_spec,
        scratch_shapes=[pltpu.VMEM((tm + 2 * HALO, D_MODEL), _BF16)],
        compiler_params=_params(("parallel", "parallel")),
        name="conv_ffn",
    )(h2, h2, h2, x1, mod, w_g, w_u, conv_w, conv_b, w_down, ln_g, ln_b)


def _dft_tables(n):
    idx = np.arange(n)
    ang = 2.0 * np.pi * ((idx[:, None] * idx[None, :]) % n) / n
    return np.cos(ang) / np.sqrt(n), np.sin(ang) / np.sqrt(n)


def _block_diag(blocks):
    n = len(blocks)
    d = blocks[0].shape[0]
    out = np.zeros((n * d, n * d), blocks[0].dtype)
    for g, blk in enumerate(blocks):
        out[g * d:(g + 1) * d, g * d:(g + 1) * d] = blk
    return out


def kernel(x_prompt, x_sample, cache_k, cache_v, c, c_ctx, w_ada, b_ada, w_in, a_ws, a_bs, f_w, rpb,
           w_out, ln1_g, ln1_b, w_gu, conv_w, conv_b, w_down, ln2_g, ln2_b):
    n_ctx_b, ctx_len, _ = x_prompt.shape
    n_lat_b, lat_len, _ = x_sample.shape
    past_len = cache_k.shape[2]

    cs = jnp.zeros((16, D_MODEL), _F32).at[0].set(c_ctx).at[1:1 + n_lat_b].set(c)
    mod = _modulation(cs, w_ada, b_ada).reshape(DEPTH, 16, 6, D_MODEL)

    cc, sc = _dft_tables(B_GROUP_DIM)
    ch_cos = jnp.asarray(_block_diag([cc] * B_GROUPS), _F32).astype(_BF16)
    ch_sin = jnp.asarray(_block_diag([sc] * B_GROUPS), _F32).astype(_BF16)
    pos_tabs = {n: tuple(jnp.asarray(t, _F32).astype(_BF16) for t in _dft_tables(n))
                for n in (ctx_len, lat_len)}

    xp, xs = x_prompt, x_sample
    new_k, new_v = [], []
    for l in range(DEPTH):
        w_in_l = w_in[l].astype(_BF16)
        w_out_l = w_out[l].astype(_BF16)
        w_g = w_gu[l, :, :D_FF].astype(_BF16)
        w_u = w_gu[l, :, D_FF:].astype(_BF16)
        w_d = w_down[l].astype(_BF16)
        ws = a_ws[l].reshape(A_HEADS * CHUNK, CHUNK).astype(_BF16)
        bs = jnp.repeat(a_bs[l].T, A_HEAD_DIM, axis=1)
        fw = jnp.zeros((D_B, D_B), _F32)
        for g in range(B_GROUPS):
            sl = slice(g * B_GROUP_DIM, (g + 1) * B_GROUP_DIM)
            fw = fw.at[sl, sl].set(f_w[l, g])
        fw = fw.astype(_BF16)
        g1, b1 = ln1_g[l][None], ln1_b[l][None]
        g2, b2 = ln2_g[l][None], ln2_b[l][None]
        cb = conv_b[l][None]
        bias = _na_bias_table(rpb[l], lat_len // GRID_W)
        kc = cache_k[:, l].reshape(n_lat_b, past_len, D_C).astype(_BF16)
        vc = cache_v[:, l].reshape(n_lat_b, past_len, D_C).astype(_BF16)

        def layer(x, mod_rows, attend, kv_dtype):
            L = x.shape[1]
            u, vn, zc, zs, q, k, v = _in_proj(x, mod_rows, w_in_l, ch_cos, ch_sin, kv_dtype)
            four = _pos_dft(*pos_tabs[L], zc, zs)
            attn = attend(q, k, v)
            x1, h2 = _out_proj(x, u, vn, four, attn, mod_rows, ws, bs, fw, w_out_l, g1, b1)
            x2 = _conv_ffn(h2, x1, mod_rows, w_g, w_u, conv_w[l], cb, w_d, g2, b2)
            return x2, k, v

        xp, k_l, v_l = layer(xp, mod[l, 0:1], _ctx_attention, _F32)
        new_k.append(k_l)
        new_v.append(v_l)
        xs, _, _ = layer(xs, mod[l, 1:1 + n_lat_b],
                         lambda q, k, v: _na_attention(q, k, v, kc, vc, bias), _BF16)

    kv_shape = (n_ctx_b, DEPTH, ctx_len, C_HEADS, C_HEAD_DIM)
    return (xp, xs, jnp.stack(new_k, axis=1).reshape(kv_shape), jnp.stack(new_v, axis=1).reshape(kv_shape))
```

```python
import functools

import numpy as np
import jax
import jax.numpy as jnp
from jax import lax
from jax.experimental import pallas as pl
from jax.experimental.pallas import tpu as pltpu

D_MODEL = 1024
DEPTH = 2
GRID_W = 64
D_A = 256
A_HEADS = 4
A_HEAD_DIM = 64
CHUNK = 128
D_B = 256
B_GROUPS = 4
B_GROUP_DIM = 64
D_C = 512
C_HEAD_DIM = 64
C_HEADS = 8
P_IN = 2 * D_A + D_B + 3 * D_C
NA_ROWS = 8
NA_KW = 16
D_FF = 2816
ALPHA = (2 * DEPTH) ** 0.25
LN_EPS = 1e-6

LANES = 128
HEAD_PAIR = 2 * C_HEAD_DIM
N_HEAD_PAIRS = C_HEADS // 2
MOD_ROWS = 16
PROJ_ROW_TILE = 512
FFN_ROW_TILE = 1024
FF_CHUNK = 256
HALO = 16
NA_ROW_BLOCK = 8
VMEM_LIMIT = 56 * 1024 * 1024

_F32 = jnp.float32
_BF16 = jnp.bfloat16


def _dot(a, b):
    return jnp.dot(a, b, preferred_element_type=_F32)


def _dot_nt(a, b):
    return lax.dot_general(a, b, (((1,), (1,)), ((), ())), preferred_element_type=_F32)


def _norm(x):
    mu = jnp.mean(x, axis=-1, keepdims=True)
    xc = x - mu
    var = jnp.mean(xc * xc, axis=-1, keepdims=True)
    return xc * lax.rsqrt(var + LN_EPS)


def _params(sem):
    return pltpu.CompilerParams(dimension_semantics=sem, vmem_limit_bytes=VMEM_LIMIT)


def _const_spec(shape):
    nd = len(shape)
    return pl.BlockSpec(shape, lambda *_: (0,) * nd, pipeline_mode=pl.Buffered(1))


def _layer_spec(shape, layer):
    nd = len(shape)
    return pl.BlockSpec((None,) + shape, lambda *_: (layer,) + (0,) * nd, pipeline_mode=pl.Buffered(1))


def _mod_spec(layer, first_row, rows_per_seq):
    if rows_per_seq is None:
        return pl.BlockSpec((None, None, 6, D_MODEL), lambda *_: (layer, first_row, 0, 0))
    return pl.BlockSpec((None, None, 6, D_MODEL), rows_per_seq(layer, first_row))


def _mod_kernel(c_ref, w_ref, b_ref, o_ref):
    c = c_ref[...]
    s = c * jax.nn.sigmoid(c)
    o_ref[0] = _dot(s.astype(_BF16), w_ref[0].astype(_BF16)) + b_ref[0]


def _modulation(cs, w_ada, b_ada):
    n = cs.shape[0]
    tn = 1536
    return pl.pallas_call(
        _mod_kernel,
        out_shape=jax.ShapeDtypeStruct((DEPTH, n, 6 * D_MODEL), _F32),
        grid=(DEPTH, 6 * D_MODEL // tn),
        in_specs=[
            pl.BlockSpec((n, D_MODEL), lambda l, j: (0, 0)),
            pl.BlockSpec((1, D_MODEL, tn), lambda l, j: (l, 0, j)),
            pl.BlockSpec((1, 1, tn), lambda l, j: (l, 0, j)),
        ],
        out_specs=pl.BlockSpec((1, n, tn), lambda l, j: (l, 0, j)),
        compiler_params=_params(("parallel", "parallel")),
        name="adaln_mod",
    )(cs, w_ada, b_ada.reshape(DEPTH, 1, 6 * D_MODEL))


def _in_kernel(x_ref, mod_ref, w_ref, cc_ref, cs_ref,
               u_ref, vn_ref, zc_ref, zs_ref, q_ref, k_ref, v_ref):
    x = x_ref[0]
    m = mod_ref[...]
    h = _norm(x) * (1.0 + m[1:2]) + m[0:1]
    z = _dot(h.astype(_BF16), w_ref[...])
    za = jax.nn.gelu(z[:, :2 * D_A])
    u_ref[0] = za[:, :D_A]
    vn_ref[0] = _norm(za[:, D_A:]).astype(_BF16)
    o1 = 2 * D_A
    zb = z[:, o1:o1 + D_B].astype(_BF16)
    zc_ref[...] = _dot(zb, cc_ref[...]).astype(_BF16)
    zs_ref[...] = _dot(zb, cs_ref[...]).astype(_BF16)
    o2 = o1 + D_B
    q_ref[0] = (z[:, o2:o2 + D_C] * (C_HEAD_DIM ** -0.5)).astype(q_ref.dtype)
    k_ref[0] = z[:, o2 + D_C:o2 + 2 * D_C].astype(k_ref.dtype)
    v_ref[0] = z[:, o2 + 2 * D_C:o2 + 3 * D_C].astype(v_ref.dtype)


def _in_proj(x, mod, w_in, ch_cos, ch_sin, *, layer, mod_row, per_seq_mod, kv_dtype):
    B, L, _ = x.shape
    tm = min(L, PROJ_ROW_TILE)
    row = lambda w, dt: jax.ShapeDtypeStruct((B, L, w), dt)
    row_spec = lambda w: pl.BlockSpec((1, tm, w), lambda b, i: (b, i, 0))
    four = jax.ShapeDtypeStruct((L, B * D_B), _BF16)
    four_spec = pl.BlockSpec((tm, D_B), lambda b, i: (i, b))
    mod_idx = (lambda l, r: (lambda b, i: (l, r + b, 0, 0))) if per_seq_mod else None
    return pl.pallas_call(
        _in_kernel,
        out_shape=(row(D_A, _F32), row(D_A, _BF16), four, four,
                   row(D_C, _BF16), row(D_C, kv_dtype), row(D_C, kv_dtype)),
        grid=(B, L // tm),
        in_specs=[
            row_spec(D_MODEL),
            _mod_spec(layer, mod_row, mod_idx),
            _layer_spec((D_MODEL, P_IN), layer),
            _const_spec((D_B, D_B)),
            _const_spec((D_B, D_B)),
        ],
        out_specs=(row_spec(D_A), row_spec(D_A), four_spec, four_spec,
                   row_spec(D_C), row_spec(D_C), row_spec(D_C)),
        compiler_params=_params(("parallel", "parallel")),
        name="in_proj",
    )(x, mod, w_in, ch_cos, ch_sin)


def _dft_kernel(c_ref, s_ref, zc_ref, zs_ref, o_ref):
    o_ref[...] = (_dot(c_ref[...], zc_ref[...]) - _dot(s_ref[...], zs_ref[...])).astype(o_ref.dtype)


def _pos_dft(pos_cos, pos_sin, zc, zs):
    L, N = zc.shape
    ti = min(L, 512)
    tj = 512 if L > 512 else 1024
    return pl.pallas_call(
        _dft_kernel,
        out_shape=jax.ShapeDtypeStruct((L, N), _BF16),
        grid=(L // ti, N // tj),
        in_specs=[
            pl.BlockSpec((ti, L), lambda i, j: (i, 0)),
            pl.BlockSpec((ti, L), lambda i, j: (i, 0)),
            pl.BlockSpec((L, tj), lambda i, j: (0, j)),
            pl.BlockSpec((L, tj), lambda i, j: (0, j)),
        ],
        out_specs=pl.BlockSpec((ti, tj), lambda i, j: (i, j)),
        compiler_params=_params(("parallel", "parallel")),
        name="pos_dft",
    )(pos_cos, pos_sin, zc, zs)


def _head_lane_mask(rows, head):
    lane = lax.broadcasted_iota(jnp.int32, (rows, HEAD_PAIR), 1)
    return (lane >= head * C_HEAD_DIM) & (lane < (head + 1) * C_HEAD_DIM)


def _ctx_attn_kernel(q_ref, k_ref, v_ref, o_ref):
    rows = q_ref.shape[1]
    for pair in range(N_HEAD_PAIRS):
        lanes = slice(pair * HEAD_PAIR, (pair + 1) * HEAD_PAIR)
        q = q_ref[0, :, lanes]
        k = k_ref[0, :, lanes].astype(_BF16)
        v = v_ref[0, :, lanes].astype(_BF16)
        out = jnp.zeros((rows, HEAD_PAIR), _F32)
        for head in range(2):
            sel = _head_lane_mask(rows, head)
            s = _dot_nt(jnp.where(sel, q, jnp.zeros_like(q)), k)
            p = jnp.exp(s - jnp.max(s, axis=-1, keepdims=True))
            o = _dot(p.astype(_BF16), v) / jnp.sum(p, axis=-1, keepdims=True)
            out = jnp.where(sel, o, out)
        o_ref[0, :, lanes] = out.astype(o_ref.dtype)


def _ctx_attention(q, k, v):
    B, L, _ = q.shape
    spec = pl.BlockSpec((1, L, D_C), lambda b: (b, 0, 0))
    return pl.pallas_call(
        _ctx_attn_kernel,
        out_shape=jax.ShapeDtypeStruct((B, L, D_C), _BF16),
        grid=(B,),
        in_specs=[spec, spec, spec],
        out_specs=spec,
        compiler_params=_params(("parallel",)),
        name="ctx_attention",
    )(q, k, v)


def _na_kernel(q_ref, k_ref, v_ref, kc_ref, vc_ref, bias_ref, o_ref, *, n_rows):
    kc = kc_ref[...]
    vc = vc_ref[...]
    blk = NA_ROW_BLOCK * GRID_W
    win_keys = NA_ROWS * GRID_W

    def block_body(i, carry):
        q_start = pl.multiple_of(i * blk, blk)
        q = q_ref[0, pl.ds(q_start, blk), :]
        windows = []
        for j in range(NA_ROW_BLOCK):
            r = i * NA_ROW_BLOCK + j
            rs = jnp.clip(r - NA_ROWS // 2, 0, n_rows - NA_ROWS)
            windows.append((pl.ds(pl.multiple_of(rs * GRID_W, GRID_W), win_keys),
                            rs - r + NA_ROWS - 1))
        out = jnp.zeros((blk, HEAD_PAIR), _F32)
        for head in range(2):
            sel = _head_lane_mask(blk, head)
            qh = jnp.where(sel, q, jnp.zeros_like(q))
            s_ctx = _dot_nt(qh, kc)
            s_loc = []
            for j, (win, base) in enumerate(windows):
                bias = jnp.concatenate(
                    [bias_ref[head, base + 2 * t] for t in range(NA_ROWS // 2)], axis=1)
                s_loc.append(_dot_nt(qh[j * GRID_W:(j + 1) * GRID_W], k_ref[0, win, :]) + bias)
            s_loc = jnp.concatenate(s_loc, axis=0)
            m = jnp.maximum(jnp.max(s_loc, axis=-1, keepdims=True),
                            jnp.max(s_ctx, axis=-1, keepdims=True))
            p_loc = jnp.exp(s_loc - m)
            p_ctx = jnp.exp(s_ctx - m)
            denom = jnp.sum(p_loc, axis=-1, keepdims=True) + jnp.sum(p_ctx, axis=-1, keepdims=True)
            p_loc = p_loc.astype(_BF16)
            o_loc = jnp.concatenate(
                [_dot(p_loc[j * GRID_W:(j + 1) * GRID_W], v_ref[0, win, :])
                 for j, (win, _) in enumerate(windows)], axis=0)
            o = (o_loc + _dot(p_ctx.astype(_BF16), vc)) / denom
            out = jnp.where(sel, o, out)
        o_ref[0, pl.ds(q_start, blk), :] = out.astype(o_ref.dtype)
        return carry

    lax.fori_loop(0, n_rows // NA_ROW_BLOCK, block_body, 0)


def _na_bias_table(rpb):
    n_ri = 2 * NA_ROWS - 1
    span = 2 * GRID_W - 1
    lo = GRID_W - NA_KW
    v = jnp.pad(rpb, ((0, 0), (0, 0), (0, 0), (lo, span - lo - (2 * NA_KW - 1) + 1)))
    tiled = jnp.broadcast_to(v[..., None, :], v.shape[:-1] + (GRID_W, 2 * GRID_W))
    tiled = tiled.reshape(v.shape[:-1] + (GRID_W * 2 * GRID_W,))[..., :GRID_W * span]
    toep = tiled.reshape(v.shape[:-1] + (GRID_W, span))[..., GRID_W - 1:]
    qc = np.arange(GRID_W)[:, None]
    kc = np.arange(GRID_W)[None, :]
    start = np.clip(qc - NA_KW // 2, 0, GRID_W - NA_KW)
    in_win = (kc >= start) & (kc < start + NA_KW)
    g = jnp.where(in_win, toep, -jnp.inf)
    return jnp.concatenate([g[:, :, :n_ri - 1], g[:, :, 1:]], axis=-1)


def _na_attention(q, k, v, kc, vc, bias, *, layer):
    B, L, _ = q.shape
    Lc = kc.shape[2]
    n_rows = L // GRID_W
    assert n_rows >= NA_ROWS and n_rows % NA_ROW_BLOCK == 0 and 2 * GRID_W == LANES
    lat = pl.BlockSpec((1, L, HEAD_PAIR), lambda p, b: (b, 0, p))
    ctx = pl.BlockSpec((None, None, Lc, HEAD_PAIR), lambda p, b: (b, layer, 0, p))
    return pl.pallas_call(
        functools.partial(_na_kernel, n_rows=n_rows),
        out_shape=jax.ShapeDtypeStruct((B, L, D_C), _BF16),
        grid=(N_HEAD_PAIRS, B),
        in_specs=[lat, lat, lat, ctx, ctx,
                  pl.BlockSpec((None, 2, 2 * NA_ROWS - 2, GRID_W, LANES),
                               lambda p, b: (layer, p, 0, 0, 0))],
        out_specs=lat,
        compiler_params=_params(("parallel", "parallel")),
        name="na_attention",
    )(q, k, v, kc, vc, bias)


def _out_kernel(x_ref, u_ref, vn_ref, f_ref, a_ref, mod_ref, ws_ref, bs_ref, fw_ref, wo_ref,
                g_ref, b_ref, x1_ref, h2_ref):
    tm = x_ref.shape[1]
    lane = lax.broadcasted_iota(jnp.int32, (CHUNK, D_A), 1)
    ya = []
    for n in range(tm // CHUNK):
        rows = slice(n * CHUNK, (n + 1) * CHUNK)
        r = _dot(ws_ref[...], vn_ref[0, rows, :])
        mixed = r[:CHUNK]
        for hd in range(1, A_HEADS):
            mixed = jnp.where(lane >= hd * A_HEAD_DIM, r[hd * CHUNK:(hd + 1) * CHUNK], mixed)
        ya.append(u_ref[0, rows, :] * (mixed + bs_ref[...]))
    ya = jnp.concatenate(ya, axis=0).astype(_BF16)
    yb = _dot(f_ref[...], fw_ref[...]).astype(_BF16)
    y = (_dot(ya, wo_ref[:D_A, :]) + _dot(yb, wo_ref[D_A:D_A + D_B, :])
         + _dot(a_ref[0], wo_ref[D_A + D_B:, :]))
    m = mod_ref[...]
    x1 = _norm(ALPHA * x_ref[0] + m[2:3] * y) * g_ref[...] + b_ref[...]
    x1_ref[0] = x1
    h2_ref[0] = (_norm(x1) * (1.0 + m[4:5]) + m[3:4]).astype(_BF16)


def _out_proj(x, u, vn, four, attn, mod, ws, bs, fw, w_out, ln_g, ln_b, *, layer, mod_row, per_seq_mod):
    B, L, _ = x.shape
    tm = min(L, PROJ_ROW_TILE)
    row_spec = lambda w: pl.BlockSpec((1, tm, w), lambda b, i: (b, i, 0))
    mod_idx = (lambda l, r: (lambda b, i: (l, r + b, 0, 0))) if per_seq_mod else None
    return pl.pallas_call(
        _out_kernel,
        out_shape=(jax.ShapeDtypeStruct((B, L, D_MODEL), _F32),
                   jax.ShapeDtypeStruct((B, L, D_MODEL), _BF16)),
        grid=(B, L // tm),
        in_specs=[
            row_spec(D_MODEL), row_spec(D_A), row_spec(D_A),
            pl.BlockSpec((tm, D_B), lambda b, i: (i, b)),
            row_spec(D_C),
            _mod_spec(layer, mod_row, mod_idx),
            _layer_spec((A_HEADS * CHUNK, CHUNK), layer),
            _layer_spec((CHUNK, D_A), layer),
            _layer_spec((D_B, D_B), layer),
            _layer_spec((D_MODEL, D_MODEL), layer),
            _layer_spec((1, D_MODEL), layer),
            _layer_spec((1, D_MODEL), layer),
        ],
        out_specs=(row_spec(D_MODEL), row_spec(D_MODEL)),
        compiler_params=_params(("parallel", "parallel")),
        name="out_proj",
    )(x, u, vn, four, attn, mod, ws, bs, fw, w_out, ln_g, ln_b)


def _ffn_kernel(h_ref, hp_ref, hn_ref, x_ref, mod_ref, wgu_ref, cw_ref, cb_ref, wd_ref,
                g_ref, b_ref, o_ref, hcat_ref, act_ref, *, seq_len):
    tm = h_ref.shape[0]
    ext = tm + 2 * HALO
    inside_one_seq = seq_len % tm == 0
    if inside_one_seq:
        t = pl.program_id(0) % (seq_len // tm)
        hcat_ref[:HALO, :] = jnp.where(t > 0, hp_ref[...], jnp.zeros_like(hp_ref[...]))
        hcat_ref[HALO + tm:, :] = jnp.where(t < seq_len // tm - 1, hn_ref[...],
                                            jnp.zeros_like(hn_ref[...]))
    else:
        assert seq_len & (seq_len - 1) == 0
        hcat_ref[:HALO, :] = hp_ref[...]
        hcat_ref[HALO + tm:, :] = hn_ref[...]
        pos = lax.broadcasted_iota(jnp.int32, (tm, 1), 0) & (seq_len - 1)
        has_prev = pos != 0
        has_next = pos != seq_len - 1
    hcat_ref[HALO:HALO + tm, :] = h_ref[...]
    for j in range(D_FF // FF_CHUNK):
        cols = slice(j * FF_CHUNK, (j + 1) * FF_CHUNK)
        up_cols = slice(D_FF + j * FF_CHUNK, D_FF + (j + 1) * FF_CHUNK)
        g = _dot(hcat_ref[...], wgu_ref[:, cols])
        g_prev = pltpu.roll(g, 1, 0)[HALO:HALO + tm]
        g_next = pltpu.roll(g, ext - 1, 0)[HALO:HALO + tm]
        if not inside_one_seq:
            g_prev = jnp.where(has_prev, g_prev, 0.0)
            g_next = jnp.where(has_next, g_next, 0.0)
        gc = (cb_ref[:, cols] + g_prev * cw_ref[0:1, cols] + g[HALO:HALO + tm] * cw_ref[1:2, cols]
              + g_next * cw_ref[2:3, cols])
        up = _dot(h_ref[...], wgu_ref[:, up_cols])
        act_ref[:, cols] = ((gc * jax.nn.sigmoid(gc)) * up).astype(_BF16)
    y = _dot(act_ref[...], wd_ref[...])
    m = mod_ref[...]
    o_ref[...] = _norm(ALPHA * x_ref[...] + m[5:6] * y) * g_ref[...] + b_ref[...]


def _conv_ffn(h2, x1, mod, w_gu, conv_w, conv_b, w_down, ln_g, ln_b, *, layer, mod_row, per_seq_mod):
    B, L, _ = x1.shape
    n_tok = B * L
    tm = FFN_ROW_TILE
    assert n_tok % tm == 0 and (L % tm == 0 or (tm % L == 0 and not per_seq_mod))
    h2 = h2.reshape(n_tok, D_MODEL)
    x1 = x1.reshape(n_tok, D_MODEL)
    row_spec = pl.BlockSpec((tm, D_MODEL), lambda t: (t, 0))
    per = tm // HALO
    last = n_tok // HALO - 1
    prev_spec = pl.BlockSpec((HALO, D_MODEL), lambda t: (jnp.maximum(t * per - 1, 0), 0))
    next_spec = pl.BlockSpec((HALO, D_MODEL), lambda t: (jnp.minimum((t + 1) * per, last), 0))
    mod_idx = (lambda l, r: (lambda t: (l, r + (t * tm) // L, 0, 0))) if per_seq_mod else None
    out = pl.pallas_call(
        functools.partial(_ffn_kernel, seq_len=L),
        out_shape=jax.ShapeDtypeStruct((n_tok, D_MODEL), _F32),
        grid=(n_tok // tm,),
        in_specs=[
            row_spec, prev_spec, next_spec, row_spec,
            _mod_spec(layer, mod_row, mod_idx),
            _layer_spec((D_MODEL, 2 * D_FF), layer),
            _layer_spec((3, D_FF), layer),
            _layer_spec((1, D_FF), layer),
            _layer_spec((D_FF, D_MODEL), layer),
            _layer_spec((1, D_MODEL), layer),
            _layer_spec((1, D_MODEL), layer),
        ],
        out_specs=row_spec,
        scratch_shapes=[pltpu.VMEM((tm + 2 * HALO, D_MODEL), _BF16), pltpu.VMEM((tm, D_FF), _BF16)],
        compiler_params=_params(("parallel",)),
        name="conv_ffn",
    )(h2, h2, h2, x1, mod, w_gu, conv_w, conv_b, w_down, ln_g, ln_b)
    return out.reshape(B, L, D_MODEL)


def _dft_tables(n):
    idx = np.arange(n)
    ang = 2.0 * np.pi * ((idx[:, None] * idx[None, :]) % n) / n
    return np.cos(ang) / np.sqrt(n), np.sin(ang) / np.sqrt(n)


def _block_diag(blocks):
    n = len(blocks)
    d = blocks[0].shape[0]
    out = np.zeros((n * d, n * d), blocks[0].dtype)
    for g, blk in enumerate(blocks):
        out[g * d:(g + 1) * d, g * d:(g + 1) * d] = blk
    return out


def kernel(x_prompt, x_sample, cache_k, cache_v, c, c_ctx, w_ada, b_ada, w_in, a_ws, a_bs, f_w, rpb,
           w_out, ln1_g, ln1_b, w_gu, conv_w, conv_b, w_down, ln2_g, ln2_b):
    n_ctx_b, ctx_len, _ = x_prompt.shape
    n_lat_b, lat_len, _ = x_sample.shape
    past_len = cache_k.shape[2]

    cs = jnp.zeros((MOD_ROWS, D_MODEL), _F32).at[0].set(c_ctx).at[1:1 + n_lat_b].set(c)
    mod = _modulation(cs, w_ada, b_ada).reshape(DEPTH, MOD_ROWS, 6, D_MODEL)

    cc, sc = _dft_tables(B_GROUP_DIM)
    ch_cos = jnp.asarray(_block_diag([cc] * B_GROUPS), _F32).astype(_BF16)
    ch_sin = jnp.asarray(_block_diag([sc] * B_GROUPS), _F32).astype(_BF16)
    pos_tabs = {n: tuple(jnp.asarray(t, _F32).astype(_BF16) for t in _dft_tables(n))
                for n in (ctx_len, lat_len)}

    w_in_b = w_in.astype(_BF16)
    w_out_b = w_out.astype(_BF16)
    w_gu_b = w_gu.astype(_BF16)
    w_down_b = w_down.astype(_BF16)
    ws = a_ws.reshape(DEPTH, A_HEADS * CHUNK, CHUNK).astype(_BF16)
    bs = jnp.repeat(a_bs.transpose(0, 2, 1), A_HEAD_DIM, axis=2)
    group_eye = jnp.eye(B_GROUPS, dtype=_F32)[None, :, None, :, None]
    fw = (f_w[:, :, :, None, :] * group_eye).reshape(DEPTH, D_B, D_B).astype(_BF16)
    vec = lambda p: p.reshape(DEPTH, 1, p.shape[-1])
    g1, b1, g2, b2, cb = vec(ln1_g), vec(ln1_b), vec(ln2_g), vec(ln2_b), vec(conv_b)
    bias = _na_bias_table(rpb)
    kc = cache_k.reshape(n_lat_b, DEPTH, past_len, D_C).astype(_BF16)
    vc = cache_v.reshape(n_lat_b, DEPTH, past_len, D_C).astype(_BF16)

    xp, xs = x_prompt, x_sample
    new_k, new_v = [], []
    for l in range(DEPTH):
        def layer(x, attend, kv_dtype, **where):
            u, vn, zc, zs, q, k, v = _in_proj(x, mod, w_in_b, ch_cos, ch_sin, kv_dtype=kv_dtype, **where)
            four = _pos_dft(*pos_tabs[x.shape[1]], zc, zs)
            attn = attend(q, k, v)
            x1, h2 = _out_proj(x, u, vn, four, attn, mod, ws, bs, fw, w_out_b, g1, b1, **where)
            x2 = _conv_ffn(h2, x1, mod, w_gu_b, conv_w, cb, w_down_b, g2, b2, **where)
            return x2, k, v

        xp, k_l, v_l = layer(xp, _ctx_attention, _F32, layer=l, mod_row=0, per_seq_mod=False)
        new_k.append(k_l)
        new_v.append(v_l)
        xs, _, _ = layer(xs, functools.partial(_na_attention, kc=kc, vc=vc, bias=bias, layer=l), _BF16,
                         layer=l, mod_row=1, per_seq_mod=True)

    kv_shape = (n_ctx_b, DEPTH, ctx_len, C_HEADS, C_HEAD_DIM)
    return (xp, xs, jnp.stack(new_k, axis=1).reshape(kv_shape), jnp.stack(new_v, axis=1).reshape(kv_shape))
```

```python
import functools

import numpy as np
import jax
import jax.numpy as jnp
from jax import lax
from jax.experimental import pallas as pl
from jax.experimental.pallas import tpu as pltpu

D_MODEL = 1024
DEPTH = 2
GRID_W = 64
D_A = 256
A_HEADS = 4
A_HEAD_DIM = 64
CHUNK = 128
D_B = 256
B_GROUPS = 4
B_GROUP_DIM = 64
D_C = 512
C_HEAD_DIM = 64
C_HEADS = 8
P_IN = 2 * D_A + D_B + 3 * D_C
NA_ROWS = 8
NA_KW = 16
D_FF = 2816
ALPHA = (2 * DEPTH) ** 0.25
LN_EPS = 1e-6

LANES = 128
HEAD_PAIR = 2 * C_HEAD_DIM
N_HEAD_PAIRS = C_HEADS // 2
MOD_ROWS = 16
IN_ROW_TILE = 512
OUT_ROW_TILE = 1024
OUT_SUB_ROWS = 256
FFN_ROW_TILE = 1024
FF_CHUNK = 256
HALO = 16
VMEM_LIMIT = 56 * 1024 * 1024

_F32 = jnp.float32
_BF16 = jnp.bfloat16


def _dot(a, b):
    return jnp.dot(a, b, preferred_element_type=_F32)


def _dot_nt(a, b):
    return lax.dot_general(a, b, (((1,), (1,)), ((), ())), preferred_element_type=_F32)


def _norm(x):
    mu = jnp.mean(x, axis=-1, keepdims=True)
    xc = x - mu
    var = jnp.mean(xc * xc, axis=-1, keepdims=True)
    return xc * lax.rsqrt(var + LN_EPS)


def _params(sem):
    return pltpu.CompilerParams(dimension_semantics=sem, vmem_limit_bytes=VMEM_LIMIT)


def _const_spec(shape):
    nd = len(shape)
    return pl.BlockSpec(shape, lambda *_: (0,) * nd, pipeline_mode=pl.Buffered(1))


def _layer_spec(shape, layer):
    nd = len(shape)
    return pl.BlockSpec((None,) + shape, lambda *_: (layer,) + (0,) * nd, pipeline_mode=pl.Buffered(1))


def _mod_spec(layer, first_row, rows_per_seq):
    if rows_per_seq is None:
        return pl.BlockSpec((None, None, 6, D_MODEL), lambda *_: (layer, first_row, 0, 0))
    return pl.BlockSpec((None, None, 6, D_MODEL), rows_per_seq(layer, first_row))


def _mod_kernel(c_ref, w_ref, b_ref, o_ref):
    c = c_ref[...]
    s = c * jax.nn.sigmoid(c)
    o_ref[0] = _dot(s.astype(_BF16), w_ref[0].astype(_BF16)) + b_ref[0]


def _modulation(cs, w_ada, b_ada):
    n = cs.shape[0]
    tn = 1536
    return pl.pallas_call(
        _mod_kernel,
        out_shape=jax.ShapeDtypeStruct((DEPTH, n, 6 * D_MODEL), _F32),
        grid=(DEPTH, 6 * D_MODEL // tn),
        in_specs=[
            pl.BlockSpec((n, D_MODEL), lambda l, j: (0, 0)),
            pl.BlockSpec((1, D_MODEL, tn), lambda l, j: (l, 0, j)),
            pl.BlockSpec((1, 1, tn), lambda l, j: (l, 0, j)),
        ],
        out_specs=pl.BlockSpec((1, n, tn), lambda l, j: (l, 0, j)),
        compiler_params=_params(("parallel", "parallel")),
        name="adaln_mod",
    )(cs, w_ada, b_ada.reshape(DEPTH, 1, 6 * D_MODEL))


def _in_kernel(x_ref, mod_ref, w_ref, cc_ref, cs_ref,
               u_ref, vn_ref, zc_ref, zs_ref, q_ref, k_ref, v_ref):
    x = x_ref[0]
    m = mod_ref[...]
    h = _norm(x) * (1.0 + m[1:2]) + m[0:1]
    z = _dot(h.astype(_BF16), w_ref[...])
    za = jax.nn.gelu(z[:, :2 * D_A])
    u_ref[0] = za[:, :D_A]
    vn_ref[0] = _norm(za[:, D_A:]).astype(_BF16)
    o1 = 2 * D_A
    zb = z[:, o1:o1 + D_B].astype(_BF16)
    zc_ref[...] = _dot(zb, cc_ref[...]).astype(_BF16)
    zs_ref[...] = _dot(zb, cs_ref[...]).astype(_BF16)
    o2 = o1 + D_B
    q_ref[0] = (z[:, o2:o2 + D_C] * (C_HEAD_DIM ** -0.5)).astype(q_ref.dtype)
    k_ref[0] = z[:, o2 + D_C:o2 + 2 * D_C].astype(k_ref.dtype)
    v_ref[0] = z[:, o2 + 2 * D_C:o2 + 3 * D_C].astype(v_ref.dtype)


def _in_proj(x, mod, w_in, ch_cos, ch_sin, *, layer, mod_row, per_seq_mod, kv_dtype):
    B, L, _ = x.shape
    tm = min(L, IN_ROW_TILE)
    row = lambda w, dt: jax.ShapeDtypeStruct((B, L, w), dt)
    row_spec = lambda w: pl.BlockSpec((1, tm, w), lambda b, i: (b, i, 0))
    four = jax.ShapeDtypeStruct((L, B * D_B), _BF16)
    four_spec = pl.BlockSpec((tm, D_B), lambda b, i: (i, b))
    mod_idx = (lambda l, r: (lambda b, i: (l, r + b, 0, 0))) if per_seq_mod else None
    return pl.pallas_call(
        _in_kernel,
        out_shape=(row(D_A, _F32), row(D_A, _BF16), four, four,
                   row(D_C, _BF16), row(D_C, kv_dtype), row(D_C, kv_dtype)),
        grid=(B, L // tm),
        in_specs=[
            row_spec(D_MODEL),
            _mod_spec(layer, mod_row, mod_idx),
            _layer_spec((D_MODEL, P_IN), layer),
            _const_spec((D_B, D_B)),
            _const_spec((D_B, D_B)),
        ],
        out_specs=(row_spec(D_A), row_spec(D_A), four_spec, four_spec,
                   row_spec(D_C), row_spec(D_C), row_spec(D_C)),
        compiler_params=_params(("parallel", "parallel")),
        name="in_proj",
    )(x, mod, w_in, ch_cos, ch_sin)


def _dft_kernel(c_ref, s_ref, zc_ref, zs_ref, o_ref):
    o_ref[...] = (_dot(c_ref[...], zc_ref[...]) - _dot(s_ref[...], zs_ref[...])).astype(o_ref.dtype)


def _pos_dft(pos_cos, pos_sin, zc, zs):
    L, N = zc.shape
    ti = min(L, 512)
    tj = 512 if L > 512 else 1024
    return pl.pallas_call(
        _dft_kernel,
        out_shape=jax.ShapeDtypeStruct((L, N), _BF16),
        grid=(L // ti, N // tj),
        in_specs=[
            pl.BlockSpec((ti, L), lambda i, j: (i, 0)),
            pl.BlockSpec((ti, L), lambda i, j: (i, 0)),
            pl.BlockSpec((L, tj), lambda i, j: (0, j)),
            pl.BlockSpec((L, tj), lambda i, j: (0, j)),
        ],
        out_specs=pl.BlockSpec((ti, tj), lambda i, j: (i, j)),
        compiler_params=_params(("parallel", "parallel")),
        name="pos_dft",
    )(pos_cos, pos_sin, zc, zs)


def _head_lane_mask(rows, head):
    lane = lax.broadcasted_iota(jnp.int32, (rows, HEAD_PAIR), 1)
    return (lane >= head * C_HEAD_DIM) & (lane < (head + 1) * C_HEAD_DIM)


def _ctx_attn_kernel(q_ref, k_ref, v_ref, o_ref):
    rows = q_ref.shape[1]
    for pair in range(N_HEAD_PAIRS):
        lanes = slice(pair * HEAD_PAIR, (pair + 1) * HEAD_PAIR)
        q = q_ref[0, :, lanes]
        k = k_ref[0, :, lanes].astype(_BF16)
        v = v_ref[0, :, lanes].astype(_BF16)
        out = jnp.zeros((rows, HEAD_PAIR), _F32)
        for head in range(2):
            sel = _head_lane_mask(rows, head)
            s = _dot_nt(jnp.where(sel, q, jnp.zeros_like(q)), k)
            p = jnp.exp(s - jnp.max(s, axis=-1, keepdims=True))
            o = _dot(p.astype(_BF16), v) / jnp.sum(p, axis=-1, keepdims=True)
            out = jnp.where(sel, o, out)
        o_ref[0, :, lanes] = out.astype(o_ref.dtype)


def _ctx_attention(q, k, v):
    B, L, _ = q.shape
    spec = pl.BlockSpec((1, L, D_C), lambda b: (b, 0, 0))
    return pl.pallas_call(
        _ctx_attn_kernel,
        out_shape=jax.ShapeDtypeStruct((B, L, D_C), _BF16),
        grid=(B,),
        in_specs=[spec, spec, spec],
        out_specs=spec,
        compiler_params=_params(("parallel",)),
        name="ctx_attention",
    )(q, k, v)


def _na_kernel(q_ref, k_ref, v_ref, kc_ref, vc_ref, bias_ref, o_ref, *, n_rows):
    kc = kc_ref[...]
    vc = vc_ref[...]
    win_keys = NA_ROWS * GRID_W
    pair_rows = 2 * GRID_W
    n_q = n_rows * GRID_W

    windows = []
    for r in range(n_rows):
        rs = min(max(r - NA_ROWS // 2, 0), n_rows - NA_ROWS)
        windows.append((slice(rs * GRID_W, rs * GRID_W + win_keys), rs - r + NA_ROWS - 1))

    q = q_ref[0]
    sel0 = _head_lane_mask(n_q, 0)
    q_heads = (jnp.where(sel0, q, jnp.zeros_like(q)), jnp.where(sel0, jnp.zeros_like(q), q))
    lhs = jnp.concatenate([qh[r * GRID_W:(r + 1) * GRID_W]
                           for r in range(n_rows) for qh in q_heads], axis=0)
    s_ctx = _dot_nt(lhs, kc)
    s_loc = []
    for r, (win, base) in enumerate(windows):
        bias = jnp.concatenate(
            [jnp.concatenate([bias_ref[head, base + 2 * t] for t in range(NA_ROWS // 2)], axis=1)
             for head in range(2)], axis=0)
        s_loc.append(_dot_nt(lhs[r * pair_rows:(r + 1) * pair_rows], k_ref[0, win, :]) + bias)
    s_loc = jnp.concatenate(s_loc, axis=0)
    m = jnp.maximum(jnp.max(s_loc, axis=-1, keepdims=True), jnp.max(s_ctx, axis=-1, keepdims=True))
    p_loc = jnp.exp(s_loc - m)
    p_ctx = jnp.exp(s_ctx - m)
    denom = jnp.sum(p_loc, axis=-1, keepdims=True) + jnp.sum(p_ctx, axis=-1, keepdims=True)
    p_loc = p_loc.astype(_BF16)
    o_loc = jnp.concatenate(
        [_dot(p_loc[r * pair_rows:(r + 1) * pair_rows], v_ref[0, win, :])
         for r, (win, _) in enumerate(windows)], axis=0)
    o = (o_loc + _dot(p_ctx.astype(_BF16), vc)) / denom
    row_sel0 = _head_lane_mask(GRID_W, 0)
    o_ref[0] = jnp.concatenate(
        [jnp.where(row_sel0, o[r * pair_rows:r * pair_rows + GRID_W],
                   o[r * pair_rows + GRID_W:(r + 1) * pair_rows])
         for r in range(n_rows)], axis=0).astype(o_ref.dtype)


def _na_bias_kernel(v_ref, o_ref):
    n_ri = 2 * NA_ROWS - 1
    qcol = lax.broadcasted_iota(jnp.int32, (GRID_W, LANES), 0)
    lane = lax.broadcasted_iota(jnp.int32, (GRID_W, LANES), 1)
    kcol = lane & (GRID_W - 1)
    start = jnp.clip(qcol - NA_KW // 2, 0, GRID_W - NA_KW)
    in_win = (kcol >= start) & (kcol < start + NA_KW)
    rows = [jnp.broadcast_to(v_ref[0, ri], (GRID_W, LANES)) for ri in range(n_ri)]
    for ri in range(n_ri - 1):
        left = pltpu.roll(rows[ri], GRID_W + 1, 1, stride=1, stride_axis=0)
        right = pltpu.roll(rows[ri + 1], 1, 1, stride=1, stride_axis=0)
        o_ref[0, ri] = jnp.where(in_win, jnp.where(lane < GRID_W, left, right), -jnp.inf)


def _na_bias_table(rpb):
    n_ri = 2 * NA_ROWS - 1
    lo = GRID_W - NA_KW
    v = jnp.pad(rpb, ((0, 0), (0, 0), (0, 0), (lo, LANES - lo - (2 * NA_KW - 1))))
    v = v.reshape(DEPTH * C_HEADS, n_ri, 1, LANES)
    return pl.pallas_call(
        _na_bias_kernel,
        out_shape=jax.ShapeDtypeStruct((DEPTH * C_HEADS, n_ri - 1, GRID_W, LANES), _F32),
        grid=(DEPTH * C_HEADS,),
        in_specs=[pl.BlockSpec((1, n_ri, 1, LANES), lambda i: (i, 0, 0, 0))],
        out_specs=pl.BlockSpec((1, n_ri - 1, GRID_W, LANES), lambda i: (i, 0, 0, 0)),
        compiler_params=_params(("parallel",)),
        name="na_bias",
    )(v)


def _na_attention(q, k, v, kc, vc, bias, *, layer):
    B, L, _ = q.shape
    Lc = kc.shape[2]
    n_rows = L // GRID_W
    assert n_rows >= NA_ROWS and 2 * GRID_W == LANES
    lat = pl.BlockSpec((1, L, HEAD_PAIR), lambda p, b: (b, 0, p))
    ctx = pl.BlockSpec((None, None, Lc, HEAD_PAIR), lambda p, b: (b, layer, 0, p))
    return pl.pallas_call(
        functools.partial(_na_kernel, n_rows=n_rows),
        out_shape=jax.ShapeDtypeStruct((B, L, D_C), _BF16),
        grid=(N_HEAD_PAIRS, B),
        in_specs=[lat, lat, lat, ctx, ctx,
                  pl.BlockSpec((2, 2 * NA_ROWS - 2, GRID_W, LANES),
                               lambda p, b: (layer * N_HEAD_PAIRS + p, 0, 0, 0))],
        out_specs=lat,
        compiler_params=_params(("parallel", "parallel")),
        name="na_attention",
    )(q, k, v, kc, vc, bias)


def _out_kernel(x_ref, u_ref, vn_ref, f_ref, a_ref, mod_ref, ws_ref, bs_ref, fw_ref, wo_ref,
                g_ref, b_ref, x1_ref, h2_ref):
    tm = x_ref.shape[1]
    lane = lax.broadcasted_iota(jnp.int32, (CHUNK, D_A), 1)
    m = mod_ref[...]
    for s in range(tm // OUT_SUB_ROWS):
        sub = slice(s * OUT_SUB_ROWS, (s + 1) * OUT_SUB_ROWS)
        ya = []
        for n in range(OUT_SUB_ROWS // CHUNK):
            rows = slice(s * OUT_SUB_ROWS + n * CHUNK, s * OUT_SUB_ROWS + (n + 1) * CHUNK)
            r = _dot(ws_ref[...], vn_ref[0, rows, :])
            mixed = r[:CHUNK]
            for hd in range(1, A_HEADS):
                mixed = jnp.where(lane >= hd * A_HEAD_DIM, r[hd * CHUNK:(hd + 1) * CHUNK], mixed)
            ya.append(u_ref[0, rows, :] * (mixed + bs_ref[...]))
        ya = jnp.concatenate(ya, axis=0).astype(_BF16)
        yb = _dot(f_ref[sub, :], fw_ref[...]).astype(_BF16)
        y = _dot(jnp.concatenate([ya, yb, a_ref[0, sub, :]], axis=1), wo_ref[...])
        x1 = _norm(ALPHA * x_ref[0, sub, :] + m[2:3] * y) * g_ref[...] + b_ref[...]
        x1_ref[0, sub, :] = x1
        h2_ref[0, sub, :] = (_norm(x1) * (1.0 + m[4:5]) + m[3:4]).astype(_BF16)


def _out_proj(x, u, vn, four, attn, mod, ws, bs, fw, w_out, ln_g, ln_b, *, layer, mod_row, per_seq_mod):
    B, L, _ = x.shape
    tm = min(L, OUT_ROW_TILE)
    row_spec = lambda w: pl.BlockSpec((1, tm, w), lambda b, i: (b, i, 0))
    mod_idx = (lambda l, r: (lambda b, i: (l, r + b, 0, 0))) if per_seq_mod else None
    return pl.pallas_call(
        _out_kernel,
        out_shape=(jax.ShapeDtypeStruct((B, L, D_MODEL), _F32),
                   jax.ShapeDtypeStruct((B, L, D_MODEL), _BF16)),
        grid=(B, L // tm),
        in_specs=[
            row_spec(D_MODEL), row_spec(D_A), row_spec(D_A),
            pl.BlockSpec((tm, D_B), lambda b, i: (i, b)),
            row_spec(D_C),
            _mod_spec(layer, mod_row, mod_idx),
            _layer_spec((A_HEADS * CHUNK, CHUNK), layer),
            _layer_spec((CHUNK, D_A), layer),
            _layer_spec((D_B, D_B), layer),
            _layer_spec((D_MODEL, D_MODEL), layer),
            _layer_spec((1, D_MODEL), layer),
            _layer_spec((1, D_MODEL), layer),
        ],
        out_specs=(row_spec(D_MODEL), row_spec(D_MODEL)),
        compiler_params=_params(("parallel", "parallel")),
        name="out_proj",
    )(x, u, vn, four, attn, mod, ws, bs, fw, w_out, ln_g, ln_b)


def _ffn_kernel(h_ref, hp_ref, hn_ref, x_ref, mod_ref, wgu_ref, cw_ref, cb_ref, wd_ref,
                g_ref, b_ref, o_ref, hcat_ref, act_ref, *, seq_len):
    tm = h_ref.shape[0]
    ext = tm + 2 * HALO
    inside_one_seq = seq_len % tm == 0
    if inside_one_seq:
        t = pl.program_id(0) % (seq_len // tm)
        hcat_ref[:HALO, :] = jnp.where(t > 0, hp_ref[...], jnp.zeros_like(hp_ref[...]))
        hcat_ref[HALO + tm:, :] = jnp.where(t < seq_len // tm - 1, hn_ref[...],
                                            jnp.zeros_like(hn_ref[...]))
    else:
        assert seq_len & (seq_len - 1) == 0
        hcat_ref[:HALO, :] = hp_ref[...]
        hcat_ref[HALO + tm:, :] = hn_ref[...]
        pos = lax.broadcasted_iota(jnp.int32, (tm, 1), 0) & (seq_len - 1)
        has_prev = pos != 0
        has_next = pos != seq_len - 1
    hcat_ref[HALO:HALO + tm, :] = h_ref[...]
    for j in range(D_FF // FF_CHUNK):
        cols = slice(j * FF_CHUNK, (j + 1) * FF_CHUNK)
        up_cols = slice(D_FF + j * FF_CHUNK, D_FF + (j + 1) * FF_CHUNK)
        g = _dot(hcat_ref[...], wgu_ref[:, cols])
        g_prev = pltpu.roll(g, 1, 0)[HALO:HALO + tm]
        g_next = pltpu.roll(g, ext - 1, 0)[HALO:HALO + tm]
        if not inside_one_seq:
            g_prev = jnp.where(has_prev, g_prev, 0.0)
            g_next = jnp.where(has_next, g_next, 0.0)
        gc = (cb_ref[:, cols] + g_prev * cw_ref[0:1, cols] + g[HALO:HALO + tm] * cw_ref[1:2, cols]
              + g_next * cw_ref[2:3, cols])
        up = _dot(h_ref[...], wgu_ref[:, up_cols])
        act_ref[:, cols] = ((gc * jax.nn.sigmoid(gc)) * up).astype(_BF16)
    y = _dot(act_ref[...], wd_ref[...])
    m = mod_ref[...]
    o_ref[...] = _norm(ALPHA * x_ref[...] + m[5:6] * y) * g_ref[...] + b_ref[...]


def _conv_ffn(h2, x1, mod, w_gu, conv_w, conv_b, w_down, ln_g, ln_b, *, layer, mod_row, per_seq_mod):
    B, L, _ = x1.shape
    n_tok = B * L
    tm = FFN_ROW_TILE
    assert n_tok % tm == 0 and (L % tm == 0 or (tm % L == 0 and not per_seq_mod))
    h2 = h2.reshape(n_tok, D_MODEL)
    x1 = x1.reshape(n_tok, D_MODEL)
    row_spec = pl.BlockSpec((tm, D_MODEL), lambda t: (t, 0))
    per = tm // HALO
    last = n_tok // HALO - 1
    prev_spec = pl.BlockSpec((HALO, D_MODEL), lambda t: (jnp.maximum(t * per - 1, 0), 0))
    next_spec = pl.BlockSpec((HALO, D_MODEL), lambda t: (jnp.minimum((t + 1) * per, last), 0))
    mod_idx = (lambda l, r: (lambda t: (l, r + (t * tm) // L, 0, 0))) if per_seq_mod else None
    out = pl.pallas_call(
        functools.partial(_ffn_kernel, seq_len=L),
        out_shape=jax.ShapeDtypeStruct((n_tok, D_MODEL), _F32),
        grid=(n_tok // tm,),
        in_specs=[
            row_spec, prev_spec, next_spec, row_spec,
            _mod_spec(layer, mod_row, mod_idx),
            _layer_spec((D_MODEL, 2 * D_FF), layer),
            _layer_spec((3, D_FF), layer),
            _layer_spec((1, D_FF), layer),
            _layer_spec((D_FF, D_MODEL), layer),
            _layer_spec((1, D_MODEL), layer),
            _layer_spec((1, D_MODEL), layer),
        ],
        out_specs=row_spec,
        scratch_shapes=[pltpu.VMEM((tm + 2 * HALO, D_MODEL), _BF16), pltpu.VMEM((tm, D_FF), _BF16)],
        compiler_params=_params(("parallel",)),
        name="conv_ffn",
    )(h2, h2, h2, x1, mod, w_gu, conv_w, conv_b, w_down, ln_g, ln_b)
    return out.reshape(B, L, D_MODEL)


def _dft_tables(n):
    idx = np.arange(n)
    ang = 2.0 * np.pi * ((idx[:, None] * idx[None, :]) % n) / n
    return np.cos(ang) / np.sqrt(n), np.sin(ang) / np.sqrt(n)


def _block_diag(blocks):
    n = len(blocks)
    d = blocks[0].shape[0]
    out = np.zeros((n * d, n * d), blocks[0].dtype)
    for g, blk in enumerate(blocks):
        out[g * d:(g + 1) * d, g * d:(g + 1) * d] = blk
    return out


def kernel(x_prompt, x_sample, cache_k, cache_v, c, c_ctx, w_ada, b_ada, w_in, a_ws, a_bs, f_w, rpb,
           w_out, ln1_g, ln1_b, w_gu, conv_w, conv_b, w_down, ln2_g, ln2_b):
    n_ctx_b, ctx_len, _ = x_prompt.shape
    n_lat_b, lat_len, _ = x_sample.shape
    past_len = cache_k.shape[2]

    cs = jnp.zeros((MOD_ROWS, D_MODEL), _F32).at[0].set(c_ctx).at[1:1 + n_lat_b].set(c)
    mod = _modulation(cs, w_ada, b_ada).reshape(DEPTH, MOD_ROWS, 6, D_MODEL)

    cc, sc = _dft_tables(B_GROUP_DIM)
    ch_cos = jnp.asarray(_block_diag([cc] * B_GROUPS), _F32).astype(_BF16)
    ch_sin = jnp.asarray(_block_diag([sc] * B_GROUPS), _F32).astype(_BF16)
    pos_tabs = {n: tuple(jnp.asarray(t, _F32).astype(_BF16) for t in _dft_tables(n))
                for n in (ctx_len, lat_len)}

    w_in_b = w_in.astype(_BF16)
    w_out_b = w_out.astype(_BF16)
    w_gu_b = w_gu.astype(_BF16)
    w_down_b = w_down.astype(_BF16)
    ws = a_ws.reshape(DEPTH, A_HEADS * CHUNK, CHUNK).astype(_BF16)
    bs = jnp.repeat(a_bs.transpose(0, 2, 1), A_HEAD_DIM, axis=2)
    group_eye = jnp.eye(B_GROUPS, dtype=_F32)[None, :, None, :, None]
    fw = (f_w[:, :, :, None, :] * group_eye).reshape(DEPTH, D_B, D_B).astype(_BF16)
    vec = lambda p: p.reshape(DEPTH, 1, p.shape[-1])
    g1, b1, g2, b2, cb = vec(ln1_g), vec(ln1_b), vec(ln2_g), vec(ln2_b), vec(conv_b)
    bias = _na_bias_table(rpb)
    kc = cache_k.reshape(n_lat_b, DEPTH, past_len, D_C).astype(_BF16)
    vc = cache_v.reshape(n_lat_b, DEPTH, past_len, D_C).astype(_BF16)

    xp, xs = x_prompt, x_sample
    new_k, new_v = [], []
    for l in range(DEPTH):
        def layer(x, attend, kv_dtype, **where):
            u, vn, zc, zs, q, k, v = _in_proj(x, mod, w_in_b, ch_cos, ch_sin, kv_dtype=kv_dtype, **where)
            four = _pos_dft(*pos_tabs[x.shape[1]], zc, zs)
            attn = attend(q, k, v)
            x1, h2 = _out_proj(x, u, vn, four, attn, mod, ws, bs, fw, w_out_b, g1, b1, **where)
            x2 = _conv_ffn(h2, x1, mod, w_gu_b, conv_w, cb, w_down_b, g2, b2, **where)
            return x2, k, v

        xp, k_l, v_l = layer(xp, _ctx_attention, _F32, layer=l, mod_row=0, per_seq_mod=False)
        new_k.append(k_l)
        new_v.append(v_l)
        xs, _, _ = layer(xs, functools.partial(_na_attention, kc=kc, vc=vc, bias=bias, layer=l), _BF16,
                         layer=l, mod_row=1, per_seq_mod=True)

    kv_shape = (n_ctx_b, DEPTH, ctx_len, C_HEADS, C_HEAD_DIM)
    return (xp, xs, jnp.stack(new_k, axis=1).reshape(kv_shape), jnp.stack(new_v, axis=1).reshape(kv_shape))
```

```python
import functools

import numpy as np
import jax
import jax.numpy as jnp
from jax import lax
from jax.experimental import pallas as pl
from jax.experimental.pallas import tpu as pltpu

D_MODEL = 1024
DEPTH = 2
GRID_W = 64
D_A = 256
A_HEADS = 4
A_HEAD_DIM = 64
CHUNK = 128
D_B = 256
B_GROUPS = 4
B_GROUP_DIM = 64
D_C = 512
C_HEAD_DIM = 64
C_HEADS = 8
P_IN = 2 * D_A + D_B + 3 * D_C
NA_ROWS = 8
NA_KW = 16
D_FF = 2816
ALPHA = (2 * DEPTH) ** 0.25
LN_EPS = 1e-6

LANES = 128
HEAD_PAIR = 2 * C_HEAD_DIM
N_HEAD_PAIRS = C_HEADS // 2
MOD_ROWS = 16
IN_ROW_TILE = 512
OUT_ROW_TILE = 1024
OUT_SUB_ROWS = 256
FFN_ROW_TILE = 1024
FFN_SUB_ROWS = 256
FF_CHUNK = 256
HALO = 16
VMEM_LIMIT = 56 * 1024 * 1024

_F32 = jnp.float32
_BF16 = jnp.bfloat16


def _dot(a, b):
    return jnp.dot(a, b, preferred_element_type=_F32)


def _dot_nt(a, b):
    return lax.dot_general(a, b, (((1,), (1,)), ((), ())), preferred_element_type=_F32)


def _norm(x):
    mu = jnp.mean(x, axis=-1, keepdims=True)
    xc = x - mu
    var = jnp.mean(xc * xc, axis=-1, keepdims=True)
    return xc * lax.rsqrt(var + LN_EPS)


def _params(sem):
    return pltpu.CompilerParams(dimension_semantics=sem, vmem_limit_bytes=VMEM_LIMIT)


def _const_spec(shape):
    nd = len(shape)
    return pl.BlockSpec(shape, lambda *_: (0,) * nd, pipeline_mode=pl.Buffered(1))


def _layer_spec(shape, layer):
    nd = len(shape)
    return pl.BlockSpec((None,) + shape, lambda *_: (layer,) + (0,) * nd, pipeline_mode=pl.Buffered(1))


def _mod_spec(layer, first_row, rows_per_seq):
    if rows_per_seq is None:
        return pl.BlockSpec((None, None, 6, D_MODEL), lambda *_: (layer, first_row, 0, 0))
    return pl.BlockSpec((None, None, 6, D_MODEL), rows_per_seq(layer, first_row))


def _mod_kernel(c_ref, w_ref, b_ref, o_ref):
    c = c_ref[...]
    s = c * jax.nn.sigmoid(c)
    o_ref[0] = _dot(s.astype(_BF16), w_ref[0].astype(_BF16)) + b_ref[0]


def _modulation(cs, w_ada, b_ada):
    n = cs.shape[0]
    tn = 1536
    return pl.pallas_call(
        _mod_kernel,
        out_shape=jax.ShapeDtypeStruct((DEPTH, n, 6 * D_MODEL), _F32),
        grid=(DEPTH, 6 * D_MODEL // tn),
        in_specs=[
            pl.BlockSpec((n, D_MODEL), lambda l, j: (0, 0)),
            pl.BlockSpec((1, D_MODEL, tn), lambda l, j: (l, 0, j)),
            pl.BlockSpec((1, 1, tn), lambda l, j: (l, 0, j)),
        ],
        out_specs=pl.BlockSpec((1, n, tn), lambda l, j: (l, 0, j)),
        compiler_params=_params(("parallel", "parallel")),
        name="adaln_mod",
    )(cs, w_ada, b_ada.reshape(DEPTH, 1, 6 * D_MODEL))


def _in_kernel(x_ref, mod_ref, w_ref, cc_ref, cs_ref,
               u_ref, vn_ref, zc_ref, zs_ref, q_ref, k_ref, v_ref, *, kv_transposed):
    x = x_ref[0]
    m = mod_ref[...]
    h = _norm(x) * (1.0 + m[1:2]) + m[0:1]
    z = _dot(h.astype(_BF16), w_ref[...])
    za = jax.nn.gelu(z[:, :2 * D_A])
    u_ref[0] = za[:, :D_A]
    vn_ref[0] = _norm(za[:, D_A:]).astype(_BF16)
    o1 = 2 * D_A
    zb = z[:, o1:o1 + D_B].astype(_BF16)
    zc_ref[...] = _dot(zb, cc_ref[...]).astype(_BF16)
    zs_ref[...] = _dot(zb, cs_ref[...]).astype(_BF16)
    o2 = o1 + D_B
    q_ref[0] = (z[:, o2:o2 + D_C] * (C_HEAD_DIM ** -0.5)).astype(q_ref.dtype)
    k = z[:, o2 + D_C:o2 + 2 * D_C]
    v = z[:, o2 + 2 * D_C:o2 + 3 * D_C]
    if kv_transposed:
        k, v = k.T, v.T
    k_ref[0] = k.astype(k_ref.dtype)
    v_ref[0] = v.astype(v_ref.dtype)


def _in_proj(x, mod, w_in, ch_cos, ch_sin, *, layer, mod_row, per_seq_mod, kv_transposed):
    B, L, _ = x.shape
    tm = min(L, IN_ROW_TILE)
    row = lambda w, dt: jax.ShapeDtypeStruct((B, L, w), dt)
    row_spec = lambda w: pl.BlockSpec((1, tm, w), lambda b, i: (b, i, 0))
    four = jax.ShapeDtypeStruct((L, B * D_B), _BF16)
    four_spec = pl.BlockSpec((tm, D_B), lambda b, i: (i, b))
    mod_idx = (lambda l, r: (lambda b, i: (l, r + b, 0, 0))) if per_seq_mod else None
    if kv_transposed:
        kv = jax.ShapeDtypeStruct((B, D_C, L), _F32)
        kv_spec = pl.BlockSpec((1, D_C, tm), lambda b, i: (b, 0, i))
    else:
        kv = row(D_C, _BF16)
        kv_spec = row_spec(D_C)
    return pl.pallas_call(
        functools.partial(_in_kernel, kv_transposed=kv_transposed),
        out_shape=(row(D_A, _F32), row(D_A, _BF16), four, four, row(D_C, _BF16), kv, kv),
        grid=(B, L // tm),
        in_specs=[
            row_spec(D_MODEL),
            _mod_spec(layer, mod_row, mod_idx),
            _layer_spec((D_MODEL, P_IN), layer),
            _const_spec((D_B, D_B)),
            _const_spec((D_B, D_B)),
        ],
        out_specs=(row_spec(D_A), row_spec(D_A), four_spec, four_spec, row_spec(D_C), kv_spec, kv_spec),
        compiler_params=_params(("parallel", "parallel")),
        name="in_proj",
    )(x, mod, w_in, ch_cos, ch_sin)


def _dft_kernel(c_ref, s_ref, zc_ref, zs_ref, o_ref):
    o_ref[...] = (_dot(c_ref[...], zc_ref[...]) - _dot(s_ref[...], zs_ref[...])).astype(o_ref.dtype)


def _pos_dft(pos_cos, pos_sin, zc, zs):
    L, N = zc.shape
    ti = min(L, 512)
    tj = 512 if L > 512 else 1024
    return pl.pallas_call(
        _dft_kernel,
        out_shape=jax.ShapeDtypeStruct((L, N), _BF16),
        grid=(L // ti, N // tj),
        in_specs=[
            pl.BlockSpec((ti, L), lambda i, j: (i, 0)),
            pl.BlockSpec((ti, L), lambda i, j: (i, 0)),
            pl.BlockSpec((L, tj), lambda i, j: (0, j)),
            pl.BlockSpec((L, tj), lambda i, j: (0, j)),
        ],
        out_specs=pl.BlockSpec((ti, tj), lambda i, j: (i, j)),
        compiler_params=_params(("parallel", "parallel")),
        name="pos_dft",
    )(pos_cos, pos_sin, zc, zs)


def _head_lane_mask(rows, head):
    lane = lax.broadcasted_iota(jnp.int32, (rows, HEAD_PAIR), 1)
    return (lane >= head * C_HEAD_DIM) & (lane < (head + 1) * C_HEAD_DIM)


def _ctx_attn_kernel(q_ref, kt_ref, vt_ref, o_ref):
    rows = q_ref.shape[1]
    for pair in range(N_HEAD_PAIRS):
        lanes = slice(pair * HEAD_PAIR, (pair + 1) * HEAD_PAIR)
        q = q_ref[0, :, lanes]
        kt = kt_ref[0, lanes, :].astype(_BF16)
        vt = vt_ref[0, lanes, :].astype(_BF16)
        out = jnp.zeros((rows, HEAD_PAIR), _F32)
        for head in range(2):
            sel = _head_lane_mask(rows, head)
            s = _dot(jnp.where(sel, q, jnp.zeros_like(q)), kt)
            p = jnp.exp(s - jnp.max(s, axis=-1, keepdims=True))
            o = _dot_nt(p.astype(_BF16), vt) / jnp.sum(p, axis=-1, keepdims=True)
            out = jnp.where(sel, o, out)
        o_ref[0, :, lanes] = out.astype(o_ref.dtype)


def _ctx_attention(q, kt, vt):
    B, L, _ = q.shape
    spec = pl.BlockSpec((1, L, D_C), lambda b: (b, 0, 0))
    spec_t = pl.BlockSpec((1, D_C, L), lambda b: (b, 0, 0))
    return pl.pallas_call(
        _ctx_attn_kernel,
        out_shape=jax.ShapeDtypeStruct((B, L, D_C), _BF16),
        grid=(B,),
        in_specs=[spec, spec_t, spec_t],
        out_specs=spec,
        compiler_params=_params(("parallel",)),
        name="ctx_attention",
    )(q, kt, vt)


def _na_kernel(q_ref, k_ref, v_ref, kct_ref, vct_ref, bias_ref, o_ref, *, n_rows):
    kct = kct_ref[...].astype(_BF16)
    vct = vct_ref[...].astype(_BF16)
    win_keys = NA_ROWS * GRID_W
    pair_rows = 2 * GRID_W
    n_q = n_rows * GRID_W

    windows = []
    for r in range(n_rows):
        rs = min(max(r - NA_ROWS // 2, 0), n_rows - NA_ROWS)
        windows.append((slice(rs * GRID_W, rs * GRID_W + win_keys), rs - r + NA_ROWS - 1))

    q = q_ref[0]
    sel0 = _head_lane_mask(n_q, 0)
    q_heads = (jnp.where(sel0, q, jnp.zeros_like(q)), jnp.where(sel0, jnp.zeros_like(q), q))
    lhs = jnp.concatenate([qh[r * GRID_W:(r + 1) * GRID_W]
                           for r in range(n_rows) for qh in q_heads], axis=0)
    s_ctx = _dot(lhs, kct)
    s_loc = []
    for r, (win, base) in enumerate(windows):
        bias = jnp.concatenate(
            [jnp.concatenate([bias_ref[head, base + 2 * t] for t in range(NA_ROWS // 2)], axis=1)
             for head in range(2)], axis=0)
        s_loc.append(_dot_nt(lhs[r * pair_rows:(r + 1) * pair_rows], k_ref[0, win, :]) + bias)
    s_loc = jnp.concatenate(s_loc, axis=0)
    m = jnp.maximum(jnp.max(s_loc, axis=-1, keepdims=True), jnp.max(s_ctx, axis=-1, keepdims=True))
    p_loc = jnp.exp(s_loc - m)
    p_ctx = jnp.exp(s_ctx - m)
    denom = jnp.sum(p_loc, axis=-1, keepdims=True) + jnp.sum(p_ctx, axis=-1, keepdims=True)
    p_loc = p_loc.astype(_BF16)
    o_loc = jnp.concatenate(
        [_dot(p_loc[r * pair_rows:(r + 1) * pair_rows], v_ref[0, win, :])
         for r, (win, _) in enumerate(windows)], axis=0)
    o = (o_loc + _dot_nt(p_ctx.astype(_BF16), vct)) / denom
    row_sel0 = _head_lane_mask(GRID_W, 0)
    o_ref[0] = jnp.concatenate(
        [jnp.where(row_sel0, o[r * pair_rows:r * pair_rows + GRID_W],
                   o[r * pair_rows + GRID_W:(r + 1) * pair_rows])
         for r in range(n_rows)], axis=0).astype(o_ref.dtype)


def _na_bias_kernel(v_ref, o_ref):
    n_ri = 2 * NA_ROWS - 1
    qcol = lax.broadcasted_iota(jnp.int32, (GRID_W, LANES), 0)
    lane = lax.broadcasted_iota(jnp.int32, (GRID_W, LANES), 1)
    kcol = lane & (GRID_W - 1)
    start = jnp.clip(qcol - NA_KW // 2, 0, GRID_W - NA_KW)
    in_win = (kcol >= start) & (kcol < start + NA_KW)
    rows = [jnp.broadcast_to(v_ref[0, ri], (GRID_W, LANES)) for ri in range(n_ri)]
    for ri in range(n_ri - 1):
        left = pltpu.roll(rows[ri], GRID_W + 1, 1, stride=1, stride_axis=0)
        right = pltpu.roll(rows[ri + 1], 1, 1, stride=1, stride_axis=0)
        o_ref[0, ri] = jnp.where(in_win, jnp.where(lane < GRID_W, left, right), -jnp.inf)


def _na_bias_table(rpb):
    n_ri = 2 * NA_ROWS - 1
    lo = GRID_W - NA_KW
    v = jnp.pad(rpb, ((0, 0), (0, 0), (0, 0), (lo, LANES - lo - (2 * NA_KW - 1))))
    v = v.reshape(DEPTH * C_HEADS, n_ri, 1, LANES)
    return pl.pallas_call(
        _na_bias_kernel,
        out_shape=jax.ShapeDtypeStruct((DEPTH * C_HEADS, n_ri - 1, GRID_W, LANES), _F32),
        grid=(DEPTH * C_HEADS,),
        in_specs=[pl.BlockSpec((1, n_ri, 1, LANES), lambda i: (i, 0, 0, 0))],
        out_specs=pl.BlockSpec((1, n_ri - 1, GRID_W, LANES), lambda i: (i, 0, 0, 0)),
        compiler_params=_params(("parallel",)),
        name="na_bias",
    )(v)


def _na_attention(q, k, v, kct, vct, bias, *, layer):
    B, L, _ = q.shape
    Lc = kct.shape[3]
    n_rows = L // GRID_W
    assert n_rows >= NA_ROWS and 2 * GRID_W == LANES
    lat = pl.BlockSpec((1, L, HEAD_PAIR), lambda p, b: (b, 0, p))
    ctx = pl.BlockSpec((None, None, HEAD_PAIR, Lc), lambda p, b: (b, layer, p, 0))
    return pl.pallas_call(
        functools.partial(_na_kernel, n_rows=n_rows),
        out_shape=jax.ShapeDtypeStruct((B, L, D_C), _BF16),
        grid=(N_HEAD_PAIRS, B),
        in_specs=[lat, lat, lat, ctx, ctx,
                  pl.BlockSpec((2, 2 * NA_ROWS - 2, GRID_W, LANES),
                               lambda p, b: (layer * N_HEAD_PAIRS + p, 0, 0, 0))],
        out_specs=lat,
        compiler_params=_params(("parallel", "parallel")),
        name="na_attention",
    )(q, k, v, kct, vct, bias)


def _out_kernel(x_ref, u_ref, vn_ref, f_ref, a_ref, mod_ref, ws_ref, bs_ref, fw_ref, wo_ref,
                g_ref, b_ref, x1_ref, h2_ref):
    tm = x_ref.shape[1]
    lane = lax.broadcasted_iota(jnp.int32, (CHUNK, D_A), 1)
    m = mod_ref[...]
    for s in range(tm // OUT_SUB_ROWS):
        sub = slice(s * OUT_SUB_ROWS, (s + 1) * OUT_SUB_ROWS)
        ya = []
        for n in range(OUT_SUB_ROWS // CHUNK):
            rows = slice(s * OUT_SUB_ROWS + n * CHUNK, s * OUT_SUB_ROWS + (n + 1) * CHUNK)
            r = _dot(ws_ref[...], vn_ref[0, rows, :])
            mixed = r[:CHUNK]
            for hd in range(1, A_HEADS):
                mixed = jnp.where(lane >= hd * A_HEAD_DIM, r[hd * CHUNK:(hd + 1) * CHUNK], mixed)
            ya.append(u_ref[0, rows, :] * (mixed + bs_ref[...]))
        ya = jnp.concatenate(ya, axis=0).astype(_BF16)
        yb = _dot(f_ref[sub, :], fw_ref[...]).astype(_BF16)
        y = _dot(jnp.concatenate([ya, yb, a_ref[0, sub, :]], axis=1), wo_ref[...])
        x1 = _norm(ALPHA * x_ref[0, sub, :] + m[2:3] * y) * g_ref[...] + b_ref[...]
        x1_ref[0, sub, :] = x1
        h2_ref[0, sub, :] = (_norm(x1) * (1.0 + m[4:5]) + m[3:4]).astype(_BF16)


def _out_proj(x, u, vn, four, attn, mod, ws, bs, fw, w_out, ln_g, ln_b, *, layer, mod_row, per_seq_mod):
    B, L, _ = x.shape
    tm = min(L, OUT_ROW_TILE)
    row_spec = lambda w: pl.BlockSpec((1, tm, w), lambda b, i: (b, i, 0))
    mod_idx = (lambda l, r: (lambda b, i: (l, r + b, 0, 0))) if per_seq_mod else None
    return pl.pallas_call(
        _out_kernel,
        out_shape=(jax.ShapeDtypeStruct((B, L, D_MODEL), _F32),
                   jax.ShapeDtypeStruct((B, L, D_MODEL), _BF16)),
        grid=(B, L // tm),
        in_specs=[
            row_spec(D_MODEL), row_spec(D_A), row_spec(D_A),
            pl.BlockSpec((tm, D_B), lambda b, i: (i, b)),
            row_spec(D_C),
            _mod_spec(layer, mod_row, mod_idx),
            _layer_spec((A_HEADS * CHUNK, CHUNK), layer),
            _layer_spec((CHUNK, D_A), layer),
            _layer_spec((D_B, D_B), layer),
            _layer_spec((D_MODEL, D_MODEL), layer),
            _layer_spec((1, D_MODEL), layer),
            _layer_spec((1, D_MODEL), layer),
        ],
        out_specs=(row_spec(D_MODEL), row_spec(D_MODEL)),
        compiler_params=_params(("parallel", "parallel")),
        name="out_proj",
    )(x, u, vn, four, attn, mod, ws, bs, fw, w_out, ln_g, ln_b)


def _ffn_kernel(h_ref, hp_ref, hn_ref, x_ref, mod_ref, wgu_ref, cw_ref, cb_ref, wd_ref,
                g_ref, b_ref, o_ref, hcat_ref, act_ref, *, seq_len):
    tm = h_ref.shape[0]
    ext = tm + 2 * HALO
    inside_one_seq = seq_len % tm == 0
    if inside_one_seq:
        t = pl.program_id(0) % (seq_len // tm)
        hcat_ref[:HALO, :] = jnp.where(t > 0, hp_ref[...], jnp.zeros_like(hp_ref[...]))
        hcat_ref[HALO + tm:, :] = jnp.where(t < seq_len // tm - 1, hn_ref[...],
                                            jnp.zeros_like(hn_ref[...]))
    else:
        assert seq_len & (seq_len - 1) == 0
        hcat_ref[:HALO, :] = hp_ref[...]
        hcat_ref[HALO + tm:, :] = hn_ref[...]
        pos = lax.broadcasted_iota(jnp.int32, (tm, 1), 0) & (seq_len - 1)
        has_prev = pos != 0
        has_next = pos != seq_len - 1
    hcat_ref[HALO:HALO + tm, :] = h_ref[...]
    for j in range(D_FF // FF_CHUNK):
        cols = slice(j * FF_CHUNK, (j + 1) * FF_CHUNK)
        up_cols = slice(D_FF + j * FF_CHUNK, D_FF + (j + 1) * FF_CHUNK)
        g = _dot(hcat_ref[...], wgu_ref[:, cols])
        g_prev = pltpu.roll(g, 1, 0)[HALO:HALO + tm]
        g_next = pltpu.roll(g, ext - 1, 0)[HALO:HALO + tm]
        if not inside_one_seq:
            g_prev = jnp.where(has_prev, g_prev, 0.0)
            g_next = jnp.where(has_next, g_next, 0.0)
        gc = (cb_ref[:, cols] + g_prev * cw_ref[0:1, cols] + g[HALO:HALO + tm] * cw_ref[1:2, cols]
              + g_next * cw_ref[2:3, cols])
        up = _dot(h_ref[...], wgu_ref[:, up_cols])
        act_ref[:, cols] = ((gc * jax.nn.sigmoid(gc)) * up).astype(_BF16)
    m = mod_ref[...]
    for s in range(tm // FFN_SUB_ROWS):
        sub = slice(s * FFN_SUB_ROWS, (s + 1) * FFN_SUB_ROWS)
        y = _dot(act_ref[sub, :], wd_ref[...])
        o_ref[sub, :] = _norm(ALPHA * x_ref[sub, :] + m[5:6] * y) * g_ref[...] + b_ref[...]


def _conv_ffn(h2, x1, mod, w_gu, conv_w, conv_b, w_down, ln_g, ln_b, *, layer, mod_row, per_seq_mod):
    B, L, _ = x1.shape
    n_tok = B * L
    tm = FFN_ROW_TILE
    assert n_tok % tm == 0 and (L % tm == 0 or (tm % L == 0 and not per_seq_mod))
    h2 = h2.reshape(n_tok, D_MODEL)
    x1 = x1.reshape(n_tok, D_MODEL)
    row_spec = pl.BlockSpec((tm, D_MODEL), lambda t: (t, 0))
    per = tm // HALO
    last = n_tok // HALO - 1
    prev_spec = pl.BlockSpec((HALO, D_MODEL), lambda t: (jnp.maximum(t * per - 1, 0), 0))
    next_spec = pl.BlockSpec((HALO, D_MODEL), lambda t: (jnp.minimum((t + 1) * per, last), 0))
    mod_idx = (lambda l, r: (lambda t: (l, r + (t * tm) // L, 0, 0))) if per_seq_mod else None
    out = pl.pallas_call(
        functools.partial(_ffn_kernel, seq_len=L),
        out_shape=jax.ShapeDtypeStruct((n_tok, D_MODEL), _F32),
        grid=(n_tok // tm,),
        in_specs=[
            row_spec, prev_spec, next_spec, row_spec,
            _mod_spec(layer, mod_row, mod_idx),
            _layer_spec((D_MODEL, 2 * D_FF), layer),
            _layer_spec((3, D_FF), layer),
            _layer_spec((1, D_FF), layer),
            _layer_spec((D_FF, D_MODEL), layer),
            _layer_spec((1, D_MODEL), layer),
            _layer_spec((1, D_MODEL), layer),
        ],
        out_specs=row_spec,
        scratch_shapes=[pltpu.VMEM((tm + 2 * HALO, D_MODEL), _BF16), pltpu.VMEM((tm, D_FF), _BF16)],
        compiler_params=_params(("parallel",)),
        name="conv_ffn",
    )(h2, h2, h2, x1, mod, w_gu, conv_w, conv_b, w_down, ln_g, ln_b)
    return out.reshape(B, L, D_MODEL)


def _dft_tables(n):
    idx = np.arange(n)
    ang = 2.0 * np.pi * ((idx[:, None] * idx[None, :]) % n) / n
    return np.cos(ang) / np.sqrt(n), np.sin(ang) / np.sqrt(n)


def _block_diag(blocks):
    n = len(blocks)
    d = blocks[0].shape[0]
    out = np.zeros((n * d, n * d), blocks[0].dtype)
    for g, blk in enumerate(blocks):
        out[g * d:(g + 1) * d, g * d:(g + 1) * d] = blk
    return out


def kernel(x_prompt, x_sample, cache_k, cache_v, c, c_ctx, w_ada, b_ada, w_in, a_ws, a_bs, f_w, rpb,
           w_out, ln1_g, ln1_b, w_gu, conv_w, conv_b, w_down, ln2_g, ln2_b):
    n_ctx_b, ctx_len, _ = x_prompt.shape
    n_lat_b, lat_len, _ = x_sample.shape
    past_len = cache_k.shape[2]

    cs = jnp.zeros((MOD_ROWS, D_MODEL), _F32).at[0].set(c_ctx).at[1:1 + n_lat_b].set(c)
    mod = _modulation(cs, w_ada, b_ada).reshape(DEPTH, MOD_ROWS, 6, D_MODEL)

    cc, sc = _dft_tables(B_GROUP_DIM)
    ch_cos = jnp.asarray(_block_diag([cc] * B_GROUPS), _F32).astype(_BF16)
    ch_sin = jnp.asarray(_block_diag([sc] * B_GROUPS), _F32).astype(_BF16)
    pos_tabs = {n: tuple(jnp.asarray(t, _F32).astype(_BF16) for t in _dft_tables(n))
                for n in (ctx_len, lat_len)}

    w_in_b = w_in.astype(_BF16)
    w_out_b = w_out.astype(_BF16)
    w_gu_b = w_gu.astype(_BF16)
    w_down_b = w_down.astype(_BF16)
    ws = a_ws.reshape(DEPTH, A_HEADS * CHUNK, CHUNK).astype(_BF16)
    bs = jnp.repeat(a_bs.transpose(0, 2, 1), A_HEAD_DIM, axis=2)
    group_eye = jnp.eye(B_GROUPS, dtype=_F32)[None, :, None, :, None]
    fw = (f_w[:, :, :, None, :] * group_eye).reshape(DEPTH, D_B, D_B).astype(_BF16)
    vec = lambda p: p.reshape(DEPTH, 1, p.shape[-1])
    g1, b1, g2, b2, cb = vec(ln1_g), vec(ln1_b), vec(ln2_g), vec(ln2_b), vec(conv_b)
    bias = _na_bias_table(rpb)
    heads_first = (0, 1, 3, 4, 2)
    kct = cache_k.transpose(heads_first).reshape(n_lat_b, DEPTH, D_C, past_len)
    vct = cache_v.transpose(heads_first).reshape(n_lat_b, DEPTH, D_C, past_len)

    xp, xs = x_prompt, x_sample
    new_k, new_v = [], []
    for l in range(DEPTH):
        def layer(x, attend, kv_transposed, **where):
            u, vn, zc, zs, q, k, v = _in_proj(x, mod, w_in_b, ch_cos, ch_sin, kv_transposed=kv_transposed,
                                              **where)
            four = _pos_dft(*pos_tabs[x.shape[1]], zc, zs)
            attn = attend(q, k, v)
            x1, h2 = _out_proj(x, u, vn, four, attn, mod, ws, bs, fw, w_out_b, g1, b1, **where)
            x2 = _conv_ffn(h2, x1, mod, w_gu_b, conv_w, cb, w_down_b, g2, b2, **where)
            return x2, k, v

        xp, k_l, v_l = layer(xp, _ctx_attention, True, layer=l, mod_row=0, per_seq_mod=False)
        new_k.append(k_l)
        new_v.append(v_l)
        xs, _, _ = layer(xs, functools.partial(_na_attention, kct=kct, vct=vct, bias=bias, layer=l), False,
                         layer=l, mod_row=1, per_seq_mod=True)

    def kv_out(per_layer):
        t = jnp.stack(per_layer, axis=1).reshape(n_ctx_b, DEPTH, C_HEADS, C_HEAD_DIM, ctx_len)
        return t.transpose(0, 1, 4, 2, 3)

    return (xp, xs, kv_out(new_k), kv_out(new_v))
```

```python
import functools

import numpy as np
import jax
import jax.numpy as jnp
from jax import lax
from jax.experimental import pallas as pl
from jax.experimental.pallas import tpu as pltpu

D_MODEL = 1024
DEPTH = 2
GRID_W = 64
D_A = 256
A_HEADS = 4
A_HEAD_DIM = 64
CHUNK = 128
D_B = 256
B_GROUPS = 4
B_GROUP_DIM = 64
D_C = 512
C_HEAD_DIM = 64
C_HEADS = 8
P_IN = 2 * D_A + D_B + 3 * D_C
NA_ROWS = 8
NA_KW = 16
D_FF = 2816
ALPHA = (2 * DEPTH) ** 0.25
LN_EPS = 1e-6
LOG2_E = 1.4426950408889634
Q_SCALE = C_HEAD_DIM ** -0.5 * LOG2_E

LANES = 128
HEAD_PAIR = 2 * C_HEAD_DIM
N_HEAD_PAIRS = C_HEADS // 2
MOD_ROWS = 16
IN_ROW_TILE = 1024
OUT_ROW_TILE = 1024
OUT_SUB_ROWS = 256
FFN_ROW_TILE = 1024
FFN_SUB_ROWS = 256
FF_CHUNK = 256
HALO = 16
VMEM_LIMIT = 56 * 1024 * 1024

_F32 = jnp.float32
_BF16 = jnp.bfloat16


def _dot(a, b):
    return jnp.dot(a, b, preferred_element_type=_F32)


def _dot_nt(a, b):
    return lax.dot_general(a, b, (((1,), (1,)), ((), ())), preferred_element_type=_F32)


def _norm(x):
    mu = jnp.mean(x, axis=-1, keepdims=True)
    xc = x - mu
    var = jnp.mean(xc * xc, axis=-1, keepdims=True)
    return xc * lax.rsqrt(var + LN_EPS)


def _params(sem):
    return pltpu.CompilerParams(dimension_semantics=sem, vmem_limit_bytes=VMEM_LIMIT)


def _const_spec(shape):
    nd = len(shape)
    return pl.BlockSpec(shape, lambda *_: (0,) * nd, pipeline_mode=pl.Buffered(1))


def _layer_spec(shape, layer):
    nd = len(shape)
    return pl.BlockSpec((None,) + shape, lambda *_: (layer,) + (0,) * nd, pipeline_mode=pl.Buffered(1))


def _mod_spec(layer, first_row, rows_per_seq):
    if rows_per_seq is None:
        return pl.BlockSpec((None, None, 6, D_MODEL), lambda *_: (layer, first_row, 0, 0))
    return pl.BlockSpec((None, None, 6, D_MODEL), rows_per_seq(layer, first_row))


def _mod_kernel(c_ref, w_ref, b_ref, o_ref):
    c = c_ref[...]
    s = c * jax.nn.sigmoid(c)
    o_ref[0] = _dot(s.astype(_BF16), w_ref[0].astype(_BF16)) + b_ref[0]


def _modulation(cs, w_ada, b_ada):
    n = cs.shape[0]
    tn = 1536
    return pl.pallas_call(
        _mod_kernel,
        out_shape=jax.ShapeDtypeStruct((DEPTH, n, 6 * D_MODEL), _F32),
        grid=(DEPTH, 6 * D_MODEL // tn),
        in_specs=[
            pl.BlockSpec((n, D_MODEL), lambda l, j: (0, 0)),
            pl.BlockSpec((1, D_MODEL, tn), lambda l, j: (l, 0, j)),
            pl.BlockSpec((1, 1, tn), lambda l, j: (l, 0, j)),
        ],
        out_specs=pl.BlockSpec((1, n, tn), lambda l, j: (l, 0, j)),
        compiler_params=_params(("parallel", "parallel")),
        name="adaln_mod",
    )(cs, w_ada, b_ada.reshape(DEPTH, 1, 6 * D_MODEL))


def _in_kernel(x_ref, mod_ref, w_ref, cc_ref, cs_ref,
               u_ref, vn_ref, zc_ref, zs_ref, q_ref, k_ref, v_ref, *, kv_transposed):
    x = x_ref[0]
    m = mod_ref[...]
    h = _norm(x) * (1.0 + m[1:2]) + m[0:1]
    z = _dot(h.astype(_BF16), w_ref[...])
    za = jax.nn.gelu(z[:, :2 * D_A])
    u_ref[0] = za[:, :D_A]
    vn_ref[0] = _norm(za[:, D_A:]).astype(_BF16)
    o1 = 2 * D_A
    zb = z[:, o1:o1 + D_B].astype(_BF16)
    zc_ref[...] = _dot(zb, cc_ref[...]).astype(_BF16)
    zs_ref[...] = _dot(zb, cs_ref[...]).astype(_BF16)
    o2 = o1 + D_B
    q_ref[0] = (z[:, o2:o2 + D_C] * Q_SCALE).astype(q_ref.dtype)
    k = z[:, o2 + D_C:o2 + 2 * D_C]
    v = z[:, o2 + 2 * D_C:o2 + 3 * D_C]
    if kv_transposed:
        k, v = k.T, v.T
    k_ref[0] = k.astype(k_ref.dtype)
    v_ref[0] = v.astype(v_ref.dtype)


def _in_proj(x, mod, w_in, ch_cos, ch_sin, *, layer, mod_row, per_seq_mod, kv_transposed):
    B, L, _ = x.shape
    tm = min(L, IN_ROW_TILE)
    row = lambda w, dt: jax.ShapeDtypeStruct((B, L, w), dt)
    row_spec = lambda w: pl.BlockSpec((1, tm, w), lambda b, i: (b, i, 0))
    four = jax.ShapeDtypeStruct((L, B * D_B), _BF16)
    four_spec = pl.BlockSpec((tm, D_B), lambda b, i: (i, b))
    mod_idx = (lambda l, r: (lambda b, i: (l, r + b, 0, 0))) if per_seq_mod else None
    if kv_transposed:
        kv = jax.ShapeDtypeStruct((B, D_C, L), _F32)
        kv_spec = pl.BlockSpec((1, D_C, tm), lambda b, i: (b, 0, i))
    else:
        kv = row(D_C, _BF16)
        kv_spec = row_spec(D_C)
    return pl.pallas_call(
        functools.partial(_in_kernel, kv_transposed=kv_transposed),
        out_shape=(row(D_A, _F32), row(D_A, _BF16), four, four, row(D_C, _BF16), kv, kv),
        grid=(B, L // tm),
        in_specs=[
            row_spec(D_MODEL),
            _mod_spec(layer, mod_row, mod_idx),
            _layer_spec((D_MODEL, P_IN), layer),
            _const_spec((D_B, D_B)),
            _const_spec((D_B, D_B)),
        ],
        out_specs=(row_spec(D_A), row_spec(D_A), four_spec, four_spec, row_spec(D_C), kv_spec, kv_spec),
        compiler_params=_params(("parallel", "parallel")),
        name="in_proj",
    )(x, mod, w_in, ch_cos, ch_sin)


def _dft_kernel(c_ref, s_ref, zc_ref, zs_ref, rev_ref, top_ref, bot_ref):
    p = _dot(c_ref[0], zc_ref[...])
    q = _dot(s_ref[0], zs_ref[...])
    top_ref[...] = (p - q)[:top_ref.shape[0]].astype(top_ref.dtype)
    bot_ref[...] = _dot(rev_ref[...], (p + q).astype(_BF16)).astype(bot_ref.dtype)


def _dft_tile_rows(L):
    return min(L // 2, 512)


def _pos_dft(pos_cos, pos_sin, rev, zc, zs):
    L, N = zc.shape
    ti = _dft_tile_rows(L)
    n_t = (L // 2) // ti
    tj = 512 if L > 512 else 1024
    half = jax.ShapeDtypeStruct((L // 2, N), _BF16)
    tab_spec = pl.BlockSpec((1, ti + HALO, L), lambda t, j: (t, 0, 0))
    return pl.pallas_call(
        _dft_kernel,
        out_shape=(half, half),
        grid=(n_t, N // tj),
        in_specs=[
            tab_spec, tab_spec,
            pl.BlockSpec((L, tj), lambda t, j: (0, j)),
            pl.BlockSpec((L, tj), lambda t, j: (0, j)),
            pl.BlockSpec((ti, ti + HALO), lambda t, j: (0, 0)),
        ],
        out_specs=(pl.BlockSpec((ti, tj), lambda t, j: (t, j)),
                   pl.BlockSpec((ti, tj), lambda t, j: (n_t - 1 - t, j))),
        compiler_params=_params(("parallel", "parallel")),
        name="pos_dft",
    )(pos_cos, pos_sin, zc, zs, rev)


def _head_lane_mask(rows, head):
    lane = lax.broadcasted_iota(jnp.int32, (rows, HEAD_PAIR), 1)
    return (lane >= head * C_HEAD_DIM) & (lane < (head + 1) * C_HEAD_DIM)


def _ctx_attn_kernel(q_ref, kt_ref, vt_ref, o_ref):
    rows = q_ref.shape[1]
    for pair in range(N_HEAD_PAIRS):
        lanes = slice(pair * HEAD_PAIR, (pair + 1) * HEAD_PAIR)
        q = q_ref[0, :, lanes]
        kt = kt_ref[0, lanes, :].astype(_BF16)
        vt = vt_ref[0, lanes, :].astype(_BF16)
        out = jnp.zeros((rows, HEAD_PAIR), _F32)
        for head in range(2):
            sel = _head_lane_mask(rows, head)
            s = _dot(jnp.where(sel, q, jnp.zeros_like(q)), kt)
            p = jnp.exp2(s - jnp.max(s, axis=-1, keepdims=True))
            o = _dot_nt(p.astype(_BF16), vt) / jnp.sum(p, axis=-1, keepdims=True)
            out = jnp.where(sel, o, out)
        o_ref[0, :, lanes] = out.astype(o_ref.dtype)


def _ctx_attention(q, kt, vt):
    B, L, _ = q.shape
    spec = pl.BlockSpec((1, L, D_C), lambda b: (b, 0, 0))
    spec_t = pl.BlockSpec((1, D_C, L), lambda b: (b, 0, 0))
    return pl.pallas_call(
        _ctx_attn_kernel,
        out_shape=jax.ShapeDtypeStruct((B, L, D_C), _BF16),
        grid=(B,),
        in_specs=[spec, spec_t, spec_t],
        out_specs=spec,
        compiler_params=_params(("parallel",)),
        name="ctx_attention",
    )(q, kt, vt)


def _na_kernel(q_ref, k_ref, v_ref, kct_ref, vct_ref, bias_ref, o_ref, *, n_rows):
    kct = kct_ref[...].astype(_BF16)
    vct = vct_ref[...].astype(_BF16)
    win_keys = NA_ROWS * GRID_W
    pair_rows = 2 * GRID_W
    n_q = n_rows * GRID_W

    windows = []
    for r in range(n_rows):
        rs = min(max(r - NA_ROWS // 2, 0), n_rows - NA_ROWS)
        windows.append((slice(rs * GRID_W, rs * GRID_W + win_keys), rs - r + NA_ROWS - 1))

    q = q_ref[0]
    sel0 = _head_lane_mask(n_q, 0)
    q_heads = (jnp.where(sel0, q, jnp.zeros_like(q)), jnp.where(sel0, jnp.zeros_like(q), q))
    lhs = jnp.concatenate([qh[r * GRID_W:(r + 1) * GRID_W]
                           for r in range(n_rows) for qh in q_heads], axis=0)
    s_ctx = _dot(lhs, kct)
    s_loc = []
    for r, (win, base) in enumerate(windows):
        bias = jnp.concatenate(
            [jnp.concatenate([bias_ref[head, base + 2 * t] for t in range(NA_ROWS // 2)], axis=1)
             for head in range(2)], axis=0)
        s_loc.append(_dot_nt(lhs[r * pair_rows:(r + 1) * pair_rows], k_ref[0, win, :]) + bias)
    s_loc = jnp.concatenate(s_loc, axis=0)
    m = jnp.maximum(jnp.max(s_loc, axis=-1, keepdims=True), jnp.max(s_ctx, axis=-1, keepdims=True))
    p_loc = jnp.exp2(s_loc - m)
    p_ctx = jnp.exp2(s_ctx - m)
    denom = jnp.sum(p_loc, axis=-1, keepdims=True) + jnp.sum(p_ctx, axis=-1, keepdims=True)
    p_loc = p_loc.astype(_BF16)
    o_loc = jnp.concatenate(
        [_dot(p_loc[r * pair_rows:(r + 1) * pair_rows], v_ref[0, win, :])
         for r, (win, _) in enumerate(windows)], axis=0)
    o = (o_loc + _dot_nt(p_ctx.astype(_BF16), vct)) / denom
    row_sel0 = _head_lane_mask(GRID_W, 0)
    o_ref[0] = jnp.concatenate(
        [jnp.where(row_sel0, o[r * pair_rows:r * pair_rows + GRID_W],
                   o[r * pair_rows + GRID_W:(r + 1) * pair_rows])
         for r in range(n_rows)], axis=0).astype(o_ref.dtype)


def _na_bias_kernel(v_ref, o_ref):
    n_ri = 2 * NA_ROWS - 1
    qcol = lax.broadcasted_iota(jnp.int32, (GRID_W, LANES), 0)
    lane = lax.broadcasted_iota(jnp.int32, (GRID_W, LANES), 1)
    kcol = lane & (GRID_W - 1)
    start = jnp.clip(qcol - NA_KW // 2, 0, GRID_W - NA_KW)
    in_win = (kcol >= start) & (kcol < start + NA_KW)
    rows = [jnp.broadcast_to(v_ref[0, ri], (GRID_W, LANES)) for ri in range(n_ri)]
    for ri in range(n_ri - 1):
        left = pltpu.roll(rows[ri], GRID_W + 1, 1, stride=1, stride_axis=0)
        right = pltpu.roll(rows[ri + 1], 1, 1, stride=1, stride_axis=0)
        o_ref[0, ri] = jnp.where(in_win, jnp.where(lane < GRID_W, left, right) * LOG2_E, -jnp.inf)


def _na_bias_table(rpb):
    n_ri = 2 * NA_ROWS - 1
    lo = GRID_W - NA_KW
    v = jnp.pad(rpb, ((0, 0), (0, 0), (0, 0), (lo, LANES - lo - (2 * NA_KW - 1))))
    v = v.reshape(DEPTH * C_HEADS, n_ri, 1, LANES)
    return pl.pallas_call(
        _na_bias_kernel,
        out_shape=jax.ShapeDtypeStruct((DEPTH * C_HEADS, n_ri - 1, GRID_W, LANES), _F32),
        grid=(DEPTH * C_HEADS,),
        in_specs=[pl.BlockSpec((1, n_ri, 1, LANES), lambda i: (i, 0, 0, 0))],
        out_specs=pl.BlockSpec((1, n_ri - 1, GRID_W, LANES), lambda i: (i, 0, 0, 0)),
        compiler_params=_params(("parallel",)),
        name="na_bias",
    )(v)


def _na_attention(q, k, v, kct, vct, bias, *, layer):
    B, L, _ = q.shape
    Lc = kct.shape[3]
    n_rows = L // GRID_W
    assert n_rows >= NA_ROWS and 2 * GRID_W == LANES
    lat = pl.BlockSpec((1, L, HEAD_PAIR), lambda p, b: (b, 0, p))
    ctx = pl.BlockSpec((None, None, HEAD_PAIR, Lc), lambda p, b: (b, layer, p, 0))
    return pl.pallas_call(
        functools.partial(_na_kernel, n_rows=n_rows),
        out_shape=jax.ShapeDtypeStruct((B, L, D_C), _BF16),
        grid=(N_HEAD_PAIRS, B),
        in_specs=[lat, lat, lat, ctx, ctx,
                  pl.BlockSpec((2, 2 * NA_ROWS - 2, GRID_W, LANES),
                               lambda p, b: (layer * N_HEAD_PAIRS + p, 0, 0, 0))],
        out_specs=lat,
        compiler_params=_params(("parallel", "parallel")),
        name="na_attention",
    )(q, k, v, kct, vct, bias)


def _out_kernel(x_ref, u_ref, vn_ref, ft_ref, fb_ref, a_ref, mod_ref, ws_ref, bs_ref, fw_ref, wo_ref,
                g_ref, b_ref, x1_ref, h2_ref, *, tiles_per_half):
    tm = x_ref.shape[1]
    lane = lax.broadcasted_iota(jnp.int32, (CHUNK, D_A), 1)
    m = mod_ref[...]
    if tiles_per_half == 0:
        four = jnp.concatenate([ft_ref[...], fb_ref[...]], axis=0)
    else:
        four = jnp.where(pl.program_id(1) < tiles_per_half, ft_ref[...], fb_ref[...])
    for s in range(tm // OUT_SUB_ROWS):
        sub = slice(s * OUT_SUB_ROWS, (s + 1) * OUT_SUB_ROWS)
        ya = []
        for n in range(OUT_SUB_ROWS // CHUNK):
            rows = slice(s * OUT_SUB_ROWS + n * CHUNK, s * OUT_SUB_ROWS + (n + 1) * CHUNK)
            r = _dot(ws_ref[...], vn_ref[0, rows, :])
            mixed = r[:CHUNK]
            for hd in range(1, A_HEADS):
                mixed = jnp.where(lane >= hd * A_HEAD_DIM, r[hd * CHUNK:(hd + 1) * CHUNK], mixed)
            ya.append(u_ref[0, rows, :] * (mixed + bs_ref[...]))
        ya = jnp.concatenate(ya, axis=0).astype(_BF16)
        yb = _dot(four[sub], fw_ref[...]).astype(_BF16)
        y = _dot(jnp.concatenate([ya, yb, a_ref[0, sub, :]], axis=1), wo_ref[...])
        x1 = _norm(ALPHA * x_ref[0, sub, :] + m[2:3] * y) * g_ref[...] + b_ref[...]
        x1_ref[0, sub, :] = x1
        h2_ref[0, sub, :] = (_norm(x1) * (1.0 + m[4:5]) + m[3:4]).astype(_BF16)


def _out_proj(x, u, vn, four_top, four_bot, attn, mod, ws, bs, fw, w_out, ln_g, ln_b, *,
              layer, mod_row, per_seq_mod):
    B, L, _ = x.shape
    tm = min(L, OUT_ROW_TILE)
    row_spec = lambda w: pl.BlockSpec((1, tm, w), lambda b, i: (b, i, 0))
    mod_idx = (lambda l, r: (lambda b, i: (l, r + b, 0, 0))) if per_seq_mod else None
    nh = (L // 2) // tm
    assert tm == L or (L // 2) % tm == 0
    if nh == 0:
        top_spec = bot_spec = pl.BlockSpec((L // 2, D_B), lambda b, i: (0, b))
    else:
        top_spec = pl.BlockSpec((tm, D_B), lambda b, i: (jnp.minimum(i, nh - 1), b))
        bot_spec = pl.BlockSpec((tm, D_B), lambda b, i: (jnp.maximum(i - nh, 0), b))
    return pl.pallas_call(
        functools.partial(_out_kernel, tiles_per_half=nh),
        out_shape=(jax.ShapeDtypeStruct((B, L, D_MODEL), _F32),
                   jax.ShapeDtypeStruct((B, L, D_MODEL), _BF16)),
        grid=(B, L // tm),
        in_specs=[
            row_spec(D_MODEL), row_spec(D_A), row_spec(D_A), top_spec, bot_spec, row_spec(D_C),
            _mod_spec(layer, mod_row, mod_idx),
            _layer_spec((A_HEADS * CHUNK, CHUNK), layer),
            _layer_spec((CHUNK, D_A), layer),
            _layer_spec((D_B, D_B), layer),
            _layer_spec((D_MODEL, D_MODEL), layer),
            _layer_spec((1, D_MODEL), layer),
            _layer_spec((1, D_MODEL), layer),
        ],
        out_specs=(row_spec(D_MODEL), row_spec(D_MODEL)),
        compiler_params=_params(("parallel", "parallel")),
        name="out_proj",
    )(x, u, vn, four_top, four_bot, attn, mod, ws, bs, fw, w_out, ln_g, ln_b)


def _ffn_kernel(h_ref, hp_ref, hn_ref, x_ref, mod_ref, wgu_ref, cw_ref, cb_ref, wd_ref,
                g_ref, b_ref, o_ref, hcat_ref, act_ref, *, seq_len):
    tm = h_ref.shape[0]
    ext = tm + 2 * HALO
    inside_one_seq = seq_len % tm == 0
    if inside_one_seq:
        t = pl.program_id(0) % (seq_len // tm)
        hcat_ref[:HALO, :] = jnp.where(t > 0, hp_ref[...], jnp.zeros_like(hp_ref[...]))
        hcat_ref[HALO + tm:, :] = jnp.where(t < seq_len // tm - 1, hn_ref[...],
                                            jnp.zeros_like(hn_ref[...]))
    else:
        assert seq_len & (seq_len - 1) == 0
        hcat_ref[:HALO, :] = hp_ref[...]
        hcat_ref[HALO + tm:, :] = hn_ref[...]
        pos = lax.broadcasted_iota(jnp.int32, (tm, 1), 0) & (seq_len - 1)
        has_prev = pos != 0
        has_next = pos != seq_len - 1
    hcat_ref[HALO:HALO + tm, :] = h_ref[...]
    for j in range(D_FF // FF_CHUNK):
        cols = slice(j * FF_CHUNK, (j + 1) * FF_CHUNK)
        up_cols = slice(D_FF + j * FF_CHUNK, D_FF + (j + 1) * FF_CHUNK)
        g = _dot(hcat_ref[...], wgu_ref[:, cols])
        g_prev = pltpu.roll(g, 1, 0)[HALO:HALO + tm]
        g_next = pltpu.roll(g, ext - 1, 0)[HALO:HALO + tm]
        if not inside_one_seq:
            g_prev = jnp.where(has_prev, g_prev, 0.0)
            g_next = jnp.where(has_next, g_next, 0.0)
        gc = (cb_ref[:, cols] + g_prev * cw_ref[0:1, cols] + g[HALO:HALO + tm] * cw_ref[1:2, cols]
              + g_next * cw_ref[2:3, cols])
        up = _dot(h_ref[...], wgu_ref[:, up_cols])
        act_ref[:, cols] = ((gc * jax.nn.sigmoid(gc)) * up).astype(_BF16)
    m = mod_ref[...]
    for s in range(tm // FFN_SUB_ROWS):
        sub = slice(s * FFN_SUB_ROWS, (s + 1) * FFN_SUB_ROWS)
        y = _dot(act_ref[sub, :], wd_ref[...])
        o_ref[sub, :] = _norm(ALPHA * x_ref[sub, :] + m[5:6] * y) * g_ref[...] + b_ref[...]


def _conv_ffn(h2, x1, mod, w_gu, conv_w, conv_b, w_down, ln_g, ln_b, *, layer, mod_row, per_seq_mod):
    B, L, _ = x1.shape
    n_tok = B * L
    tm = FFN_ROW_TILE
    assert n_tok % tm == 0 and (L % tm == 0 or (tm % L == 0 and not per_seq_mod))
    h2 = h2.reshape(n_tok, D_MODEL)
    x1 = x1.reshape(n_tok, D_MODEL)
    row_spec = pl.BlockSpec((tm, D_MODEL), lambda t: (t, 0))
    per = tm // HALO
    last = n_tok // HALO - 1
    prev_spec = pl.BlockSpec((HALO, D_MODEL), lambda t: (jnp.maximum(t * per - 1, 0), 0))
    next_spec = pl.BlockSpec((HALO, D_MODEL), lambda t: (jnp.minimum((t + 1) * per, last), 0))
    mod_idx = (lambda l, r: (lambda t: (l, r + (t * tm) // L, 0, 0))) if per_seq_mod else None
    out = pl.pallas_call(
        functools.partial(_ffn_kernel, seq_len=L),
        out_shape=jax.ShapeDtypeStruct((n_tok, D_MODEL), _F32),
        grid=(n_tok // tm,),
        in_specs=[
            row_spec, prev_spec, next_spec, row_spec,
            _mod_spec(layer, mod_row, mod_idx),
            _layer_spec((D_MODEL, 2 * D_FF), layer),
            _layer_spec((3, D_FF), layer),
            _layer_spec((1, D_FF), layer),
            _layer_spec((D_FF, D_MODEL), layer),
            _layer_spec((1, D_MODEL), layer),
            _layer_spec((1, D_MODEL), layer),
        ],
        out_specs=row_spec,
        scratch_shapes=[pltpu.VMEM((tm + 2 * HALO, D_MODEL), _BF16), pltpu.VMEM((tm, D_FF), _BF16)],
        compiler_params=_params(("parallel",)),
        name="conv_ffn",
    )(h2, h2, h2, x1, mod, w_gu, conv_w, conv_b, w_down, ln_g, ln_b)
    return out.reshape(B, L, D_MODEL)


def _dft_tables(n):
    idx = np.arange(n)
    ang = 2.0 * np.pi * ((idx[:, None] * idx[None, :]) % n) / n
    return np.cos(ang) / np.sqrt(n), np.sin(ang) / np.sqrt(n)


def _pos_dft_tables(n):
    cos, sin = _dft_tables(n)
    ti = _dft_tile_rows(n)
    tiles = lambda tab: np.stack([tab[t * ti:t * ti + ti + HALO] for t in range((n // 2) // ti)])
    rev = np.zeros((ti, ti + HALO))
    rev[np.arange(ti), ti - np.arange(ti)] = 1.0
    return tiles(cos), tiles(sin), rev


def _block_diag(blocks):
    n = len(blocks)
    d = blocks[0].shape[0]
    out = np.zeros((n * d, n * d), blocks[0].dtype)
    for g, blk in enumerate(blocks):
        out[g * d:(g + 1) * d, g * d:(g + 1) * d] = blk
    return out


def kernel(x_prompt, x_sample, cache_k, cache_v, c, c_ctx, w_ada, b_ada, w_in, a_ws, a_bs, f_w, rpb,
           w_out, ln1_g, ln1_b, w_gu, conv_w, conv_b, w_down, ln2_g, ln2_b):
    n_ctx_b, ctx_len, _ = x_prompt.shape
    n_lat_b, lat_len, _ = x_sample.shape
    past_len = cache_k.shape[2]

    cs = jnp.zeros((MOD_ROWS, D_MODEL), _F32).at[0].set(c_ctx).at[1:1 + n_lat_b].set(c)
    mod = _modulation(cs, w_ada, b_ada).reshape(DEPTH, MOD_ROWS, 6, D_MODEL)

    cc, sc = _dft_tables(B_GROUP_DIM)
    ch_cos = jnp.asarray(_block_diag([cc] * B_GROUPS), _F32).astype(_BF16)
    ch_sin = jnp.asarray(_block_diag([sc] * B_GROUPS), _F32).astype(_BF16)
    pos_tabs = {n: tuple(jnp.asarray(t, _F32).astype(_BF16) for t in _pos_dft_tables(n))
                for n in (ctx_len, lat_len)}

    w_in_b = w_in.astype(_BF16)
    w_out_b = w_out.astype(_BF16)
    w_gu_b = w_gu.astype(_BF16)
    w_down_b = w_down.astype(_BF16)
    ws = a_ws.reshape(DEPTH, A_HEADS * CHUNK, CHUNK).astype(_BF16)
    bs = jnp.repeat(a_bs.transpose(0, 2, 1), A_HEAD_DIM, axis=2)
    group_eye = jnp.eye(B_GROUPS, dtype=_F32)[None, :, None, :, None]
    fw = (f_w[:, :, :, None, :] * group_eye).reshape(DEPTH, D_B, D_B).astype(_BF16)
    vec = lambda p: p.reshape(DEPTH, 1, p.shape[-1])
    g1, b1, g2, b2, cb = vec(ln1_g), vec(ln1_b), vec(ln2_g), vec(ln2_b), vec(conv_b)
    bias = _na_bias_table(rpb)
    heads_first = (0, 1, 3, 4, 2)
    kct = cache_k.transpose(heads_first).reshape(n_lat_b, DEPTH, D_C, past_len)
    vct = cache_v.transpose(heads_first).reshape(n_lat_b, DEPTH, D_C, past_len)

    xp, xs = x_prompt, x_sample
    new_k, new_v = [], []
    for l in range(DEPTH):
        def layer(x, attend, kv_transposed, **where):
            u, vn, zc, zs, q, k, v = _in_proj(x, mod, w_in_b, ch_cos, ch_sin, kv_transposed=kv_transposed,
                                              **where)
            four_top, four_bot = _pos_dft(*pos_tabs[x.shape[1]], zc, zs)
            attn = attend(q, k, v)
            x1, h2 = _out_proj(x, u, vn, four_top, four_bot, attn, mod, ws, bs, fw, w_out_b, g1, b1, **where)
            x2 = _conv_ffn(h2, x1, mod, w_gu_b, conv_w, cb, w_down_b, g2, b2, **where)
            return x2, k, v

        xp, k_l, v_l = layer(xp, _ctx_attention, True, layer=l, mod_row=0, per_seq_mod=False)
        new_k.append(k_l)
        new_v.append(v_l)
        xs, _, _ = layer(xs, functools.partial(_na_attention, kct=kct, vct=vct, bias=bias, layer=l), False,
                         layer=l, mod_row=1, per_seq_mod=True)

    def kv_out(per_layer):
        t = jnp.stack(per_layer, axis=1).reshape(n_ctx_b, DEPTH, C_HEADS, C_HEAD_DIM, ctx_len)
        return t.transpose(0, 1, 4, 2, 3)

    return (xp, xs, kv_out(new_k), kv_out(new_v))
```

```python
import functools

import numpy as np
import jax
import jax.numpy as jnp
from jax import lax
from jax.experimental import pallas as pl
from jax.experimental.pallas import tpu as pltpu

D_MODEL = 1024
DEPTH = 2
GRID_W = 64
D_A = 256
A_HEADS = 4
A_HEAD_DIM = 64
CHUNK = 128
D_B = 256
B_GROUPS = 4
B_GROUP_DIM = 64
D_C = 512
C_HEAD_DIM = 64
C_HEADS = 8
P_IN = 2 * D_A + D_B + 3 * D_C
NA_ROWS = 8
NA_KW = 16
D_FF = 2816
ALPHA = (2 * DEPTH) ** 0.25
LN_EPS = 1e-6
LOG2_E = 1.4426950408889634
Q_SCALE = C_HEAD_DIM ** -0.5 * LOG2_E

LANES = 128
HEAD_PAIR = 2 * C_HEAD_DIM
N_HEAD_PAIRS = C_HEADS // 2
MOD_ROWS = 16
IN_ROW_TILE = 1024
OUT_ROW_TILE = 1024
OUT_SUB_ROWS = 256
FFN_ROW_TILE = 1024
FFN_SUB_ROWS = 256
FF_CHUNK = 256
HALO = 16
VMEM_LIMIT = 56 * 1024 * 1024

_F32 = jnp.float32
_BF16 = jnp.bfloat16


def _dot(a, b):
    return jnp.dot(a, b, preferred_element_type=_F32)


def _dot_nt(a, b):
    return lax.dot_general(a, b, (((1,), (1,)), ((), ())), preferred_element_type=_F32)


def _norm(x):
    mu = jnp.mean(x, axis=-1, keepdims=True)
    xc = x - mu
    var = jnp.mean(xc * xc, axis=-1, keepdims=True)
    return xc * lax.rsqrt(var + LN_EPS)


def _params(sem):
    return pltpu.CompilerParams(dimension_semantics=sem, vmem_limit_bytes=VMEM_LIMIT)


def _const_spec(shape):
    nd = len(shape)
    return pl.BlockSpec(shape, lambda *_: (0,) * nd, pipeline_mode=pl.Buffered(1))


def _layer_spec(shape, layer):
    nd = len(shape)
    return pl.BlockSpec((None,) + shape, lambda *_: (layer,) + (0,) * nd, pipeline_mode=pl.Buffered(1))


def _mod_spec(layer, first_row, rows_per_seq):
    if rows_per_seq is None:
        return pl.BlockSpec((None, None, 6, D_MODEL), lambda *_: (layer, first_row, 0, 0))
    return pl.BlockSpec((None, None, 6, D_MODEL), rows_per_seq(layer, first_row))


def _mod_kernel(c_ref, w_ref, b_ref, o_ref):
    c = c_ref[...]
    s = c * jax.nn.sigmoid(c)
    o_ref[0] = _dot(s.astype(_BF16), w_ref[0].astype(_BF16)) + b_ref[0]


def _modulation(cs, w_ada, b_ada):
    n = cs.shape[0]
    tn = 1536
    return pl.pallas_call(
        _mod_kernel,
        out_shape=jax.ShapeDtypeStruct((DEPTH, n, 6 * D_MODEL), _F32),
        grid=(DEPTH, 6 * D_MODEL // tn),
        in_specs=[
            pl.BlockSpec((n, D_MODEL), lambda l, j: (0, 0)),
            pl.BlockSpec((1, D_MODEL, tn), lambda l, j: (l, 0, j)),
            pl.BlockSpec((1, 1, tn), lambda l, j: (l, 0, j)),
        ],
        out_specs=pl.BlockSpec((1, n, tn), lambda l, j: (l, 0, j)),
        compiler_params=_params(("parallel", "parallel")),
        name="adaln_mod",
    )(cs, w_ada, b_ada.reshape(DEPTH, 1, 6 * D_MODEL))


def _in_kernel(x_ref, mod_ref, w_ref, cc_ref, cs_ref,
               u_ref, vn_ref, zc_ref, zs_ref, q_ref, k_ref, v_ref, *, kv_transposed):
    x = x_ref[0]
    m = mod_ref[...]
    h = _norm(x) * (1.0 + m[1:2]) + m[0:1]
    z = _dot(h.astype(_BF16), w_ref[...])
    za = jax.nn.gelu(z[:, :2 * D_A])
    u_ref[0] = za[:, :D_A]
    vn_ref[0] = _norm(za[:, D_A:]).astype(_BF16)
    o1 = 2 * D_A
    zb = z[:, o1:o1 + D_B].astype(_BF16)
    zc_ref[...] = _dot(zb, cc_ref[...]).astype(_BF16)
    zs_ref[...] = _dot(zb, cs_ref[...]).astype(_BF16)
    o2 = o1 + D_B
    q_ref[0] = (z[:, o2:o2 + D_C] * Q_SCALE).astype(q_ref.dtype)
    k = z[:, o2 + D_C:o2 + 2 * D_C]
    v = z[:, o2 + 2 * D_C:o2 + 3 * D_C]
    if kv_transposed:
        k, v = k.T, v.T
    k_ref[0] = k.astype(k_ref.dtype)
    v_ref[0] = v.astype(v_ref.dtype)


def _in_proj(x, mod, w_in, ch_cos, ch_sin, *, layer, mod_row, per_seq_mod, kv_transposed):
    B, L, _ = x.shape
    tm = min(L, IN_ROW_TILE)
    row = lambda w, dt: jax.ShapeDtypeStruct((B, L, w), dt)
    row_spec = lambda w: pl.BlockSpec((1, tm, w), lambda b, i: (b, i, 0))
    four = jax.ShapeDtypeStruct((L, B * D_B), _BF16)
    four_spec = pl.BlockSpec((tm, D_B), lambda b, i: (i, b))
    mod_idx = (lambda l, r: (lambda b, i: (l, r + b, 0, 0))) if per_seq_mod else None
    if kv_transposed:
        kv = jax.ShapeDtypeStruct((B, D_C, L), _F32)
        kv_spec = pl.BlockSpec((1, D_C, tm), lambda b, i: (b, 0, i))
    else:
        kv = row(D_C, _BF16)
        kv_spec = row_spec(D_C)
    return pl.pallas_call(
        functools.partial(_in_kernel, kv_transposed=kv_transposed),
        out_shape=(row(D_A, _F32), row(D_A, _BF16), four, four, row(D_C, _BF16), kv, kv),
        grid=(B, L // tm),
        in_specs=[
            row_spec(D_MODEL),
            _mod_spec(layer, mod_row, mod_idx),
            _layer_spec((D_MODEL, P_IN), layer),
            _const_spec((D_B, D_B)),
            _const_spec((D_B, D_B)),
        ],
        out_specs=(row_spec(D_A), row_spec(D_A), four_spec, four_spec, row_spec(D_C), kv_spec, kv_spec),
        compiler_params=_params(("parallel", "parallel")),
        name="in_proj",
    )(x, mod, w_in, ch_cos, ch_sin)


def _dft_kernel(c_ref, s_ref, zc_ref, zs_ref, rev_ref, top_ref, bot_ref):
    p = _dot(c_ref[0], zc_ref[...])
    q = _dot(s_ref[0], zs_ref[...])
    top_ref[...] = (p - q)[:top_ref.shape[0]].astype(top_ref.dtype)
    bot_ref[...] = _dot(rev_ref[...], (p + q).astype(_BF16)).astype(bot_ref.dtype)


def _dft_tile_rows(L):
    return min(L // 2, 512)


def _pos_dft(pos_cos, pos_sin, rev, zc, zs):
    L, N = zc.shape
    ti = _dft_tile_rows(L)
    n_t = (L // 2) // ti
    tj = 512 if L > 512 else 1024
    half = jax.ShapeDtypeStruct((L // 2, N), _BF16)
    tab_spec = pl.BlockSpec((1, ti + HALO, L), lambda t, j: (t, 0, 0))
    return pl.pallas_call(
        _dft_kernel,
        out_shape=(half, half),
        grid=(n_t, N // tj),
        in_specs=[
            tab_spec, tab_spec,
            pl.BlockSpec((L, tj), lambda t, j: (0, j)),
            pl.BlockSpec((L, tj), lambda t, j: (0, j)),
            pl.BlockSpec((ti, ti + HALO), lambda t, j: (0, 0)),
        ],
        out_specs=(pl.BlockSpec((ti, tj), lambda t, j: (t, j)),
                   pl.BlockSpec((ti, tj), lambda t, j: (n_t - 1 - t, j))),
        compiler_params=_params(("parallel", "parallel")),
        name="pos_dft",
    )(pos_cos, pos_sin, zc, zs, rev)


def _head_lane_mask(rows, head):
    lane = lax.broadcasted_iota(jnp.int32, (rows, HEAD_PAIR), 1)
    return (lane >= head * C_HEAD_DIM) & (lane < (head + 1) * C_HEAD_DIM)


def _ctx_attn_kernel(q_ref, kt_ref, vt_ref, o_ref):
    rows = q_ref.shape[1]
    for pair in range(N_HEAD_PAIRS):
        lanes = slice(pair * HEAD_PAIR, (pair + 1) * HEAD_PAIR)
        q = q_ref[0, :, lanes]
        kt = kt_ref[0, lanes, :].astype(_BF16)
        vt = vt_ref[0, lanes, :].astype(_BF16)
        out = jnp.zeros((rows, HEAD_PAIR), _F32)
        for head in range(2):
            sel = _head_lane_mask(rows, head)
            s = _dot(jnp.where(sel, q, jnp.zeros_like(q)), kt)
            p = jnp.exp2(s - jnp.max(s, axis=-1, keepdims=True))
            o = _dot_nt(p.astype(_BF16), vt) / jnp.sum(p, axis=-1, keepdims=True)
            out = jnp.where(sel, o, out)
        o_ref[0, :, lanes] = out.astype(o_ref.dtype)


def _ctx_attention(q, kt, vt):
    B, L, _ = q.shape
    spec = pl.BlockSpec((1, L, D_C), lambda b: (b, 0, 0))
    spec_t = pl.BlockSpec((1, D_C, L), lambda b: (b, 0, 0))
    return pl.pallas_call(
        _ctx_attn_kernel,
        out_shape=jax.ShapeDtypeStruct((B, L, D_C), _BF16),
        grid=(B,),
        in_specs=[spec, spec_t, spec_t],
        out_specs=spec,
        compiler_params=_params(("parallel",)),
        name="ctx_attention",
    )(q, kt, vt)


def _na_kernel(q_ref, k_ref, v_ref, kct_ref, vct_ref, bias_ref, o_ref, *, n_rows):
    kct = kct_ref[...].astype(_BF16)
    vct = vct_ref[...].astype(_BF16)
    win_keys = NA_ROWS * GRID_W
    pair_rows = 2 * GRID_W
    n_q = n_rows * GRID_W

    windows = []
    for r in range(n_rows):
        rs = min(max(r - NA_ROWS // 2, 0), n_rows - NA_ROWS)
        windows.append((slice(rs * GRID_W, rs * GRID_W + win_keys), rs - r + NA_ROWS - 1))

    q = q_ref[0]
    sel0 = _head_lane_mask(n_q, 0)
    q_heads = (jnp.where(sel0, q, jnp.zeros_like(q)), jnp.where(sel0, jnp.zeros_like(q), q))
    lhs = jnp.concatenate([qh[r * GRID_W:(r + 1) * GRID_W]
                           for r in range(n_rows) for qh in q_heads], axis=0)
    s_ctx = _dot(lhs, kct)
    s_loc = []
    for r, (win, base) in enumerate(windows):
        bias = jnp.concatenate(
            [jnp.concatenate([bias_ref[head, base + 2 * t] for t in range(NA_ROWS // 2)], axis=1)
             for head in range(2)], axis=0)
        s_loc.append(_dot_nt(lhs[r * pair_rows:(r + 1) * pair_rows], k_ref[0, win, :]) + bias)
    s_loc = jnp.concatenate(s_loc, axis=0)
    m = jnp.maximum(jnp.max(s_loc, axis=-1, keepdims=True), jnp.max(s_ctx, axis=-1, keepdims=True))
    p_loc = jnp.exp2(s_loc - m)
    p_ctx = jnp.exp2(s_ctx - m)
    denom = jnp.sum(p_loc, axis=-1, keepdims=True) + jnp.sum(p_ctx, axis=-1, keepdims=True)
    p_loc = p_loc.astype(_BF16)
    o_loc = jnp.concatenate(
        [_dot(p_loc[r * pair_rows:(r + 1) * pair_rows], v_ref[0, win, :])
         for r, (win, _) in enumerate(windows)], axis=0)
    o = (o_loc + _dot_nt(p_ctx.astype(_BF16), vct)) / denom
    row_sel0 = _head_lane_mask(GRID_W, 0)
    o_ref[0] = jnp.concatenate(
        [jnp.where(row_sel0, o[r * pair_rows:r * pair_rows + GRID_W],
                   o[r * pair_rows + GRID_W:(r + 1) * pair_rows])
         for r in range(n_rows)], axis=0).astype(o_ref.dtype)


def _na_bias_kernel(v_ref, o_ref):
    n_ri = 2 * NA_ROWS - 1
    qcol = lax.broadcasted_iota(jnp.int32, (GRID_W, LANES), 0)
    lane = lax.broadcasted_iota(jnp.int32, (GRID_W, LANES), 1)
    kcol = lane & (GRID_W - 1)
    start = jnp.clip(qcol - NA_KW // 2, 0, GRID_W - NA_KW)
    in_win = (kcol >= start) & (kcol < start + NA_KW)
    rows = [jnp.broadcast_to(v_ref[0, ri], (GRID_W, LANES)) for ri in range(n_ri)]
    for ri in range(n_ri - 1):
        left = pltpu.roll(rows[ri], GRID_W + 1, 1, stride=1, stride_axis=0)
        right = pltpu.roll(rows[ri + 1], 1, 1, stride=1, stride_axis=0)
        o_ref[0, ri] = jnp.where(in_win, jnp.where(lane < GRID_W, left, right) * LOG2_E, -jnp.inf)


def _na_bias_table(rpb):
    n_ri = 2 * NA_ROWS - 1
    lo = GRID_W - NA_KW
    v = jnp.pad(rpb, ((0, 0), (0, 0), (0, 0), (lo, LANES - lo - (2 * NA_KW - 1))))
    v = v.reshape(DEPTH * C_HEADS, n_ri, 1, LANES)
    return pl.pallas_call(
        _na_bias_kernel,
        out_shape=jax.ShapeDtypeStruct((DEPTH * C_HEADS, n_ri - 1, GRID_W, LANES), _F32),
        grid=(DEPTH * C_HEADS,),
        in_specs=[pl.BlockSpec((1, n_ri, 1, LANES), lambda i: (i, 0, 0, 0))],
        out_specs=pl.BlockSpec((1, n_ri - 1, GRID_W, LANES), lambda i: (i, 0, 0, 0)),
        compiler_params=_params(("parallel",)),
        name="na_bias",
    )(v)


def _na_attention(q, k, v, kct, vct, bias, *, layer):
    B, L, _ = q.shape
    Lc = kct.shape[3]
    n_rows = L // GRID_W
    assert n_rows >= NA_ROWS and 2 * GRID_W == LANES
    lat = pl.BlockSpec((1, L, HEAD_PAIR), lambda p, b: (b, 0, p))
    ctx = pl.BlockSpec((None, None, HEAD_PAIR, Lc), lambda p, b: (b, layer, p, 0))
    return pl.pallas_call(
        functools.partial(_na_kernel, n_rows=n_rows),
        out_shape=jax.ShapeDtypeStruct((B, L, D_C), _BF16),
        grid=(N_HEAD_PAIRS, B),
        in_specs=[lat, lat, lat, ctx, ctx,
                  pl.BlockSpec((2, 2 * NA_ROWS - 2, GRID_W, LANES),
                               lambda p, b: (layer * N_HEAD_PAIRS + p, 0, 0, 0))],
        out_specs=lat,
        compiler_params=_params(("parallel", "parallel")),
        name="na_attention",
    )(q, k, v, kct, vct, bias)


def _out_kernel(x_ref, u_ref, vn_ref, ft_ref, fb_ref, a_ref, mod_ref, ws_ref, bs_ref, fw_ref, wo_ref,
                g_ref, b_ref, x1_ref, *, tiles_per_half):
    tm = x_ref.shape[1]
    lane = lax.broadcasted_iota(jnp.int32, (CHUNK, D_A), 1)
    m = mod_ref[...]
    if tiles_per_half == 0:
        four = jnp.concatenate([ft_ref[...], fb_ref[...]], axis=0)
    else:
        four = jnp.where(pl.program_id(1) < tiles_per_half, ft_ref[...], fb_ref[...])
    for s in range(tm // OUT_SUB_ROWS):
        sub = slice(s * OUT_SUB_ROWS, (s + 1) * OUT_SUB_ROWS)
        ya = []
        for n in range(OUT_SUB_ROWS // CHUNK):
            rows = slice(s * OUT_SUB_ROWS + n * CHUNK, s * OUT_SUB_ROWS + (n + 1) * CHUNK)
            r = _dot(ws_ref[...], vn_ref[0, rows, :])
            mixed = r[:CHUNK]
            for hd in range(1, A_HEADS):
                mixed = jnp.where(lane >= hd * A_HEAD_DIM, r[hd * CHUNK:(hd + 1) * CHUNK], mixed)
            ya.append(u_ref[0, rows, :] * (mixed + bs_ref[...]))
        ya = jnp.concatenate(ya, axis=0).astype(_BF16)
        yb = _dot(four[sub], fw_ref[...]).astype(_BF16)
        y = _dot(jnp.concatenate([ya, yb, a_ref[0, sub, :]], axis=1), wo_ref[...])
        x1_ref[0, sub, :] = _norm(ALPHA * x_ref[0, sub, :] + m[2:3] * y) * g_ref[...] + b_ref[...]


def _out_proj(x, u, vn, four_top, four_bot, attn, mod, ws, bs, fw, w_out, ln_g, ln_b, *,
              layer, mod_row, per_seq_mod):
    B, L, _ = x.shape
    tm = min(L, OUT_ROW_TILE)
    row_spec = lambda w: pl.BlockSpec((1, tm, w), lambda b, i: (b, i, 0))
    mod_idx = (lambda l, r: (lambda b, i: (l, r + b, 0, 0))) if per_seq_mod else None
    nh = (L // 2) // tm
    assert tm == L or (L // 2) % tm == 0
    if nh == 0:
        top_spec = bot_spec = pl.BlockSpec((L // 2, D_B), lambda b, i: (0, b))
    else:
        top_spec = pl.BlockSpec((tm, D_B), lambda b, i: (jnp.minimum(i, nh - 1), b))
        bot_spec = pl.BlockSpec((tm, D_B), lambda b, i: (jnp.maximum(i - nh, 0), b))
    return pl.pallas_call(
        functools.partial(_out_kernel, tiles_per_half=nh),
        out_shape=jax.ShapeDtypeStruct((B, L, D_MODEL), _F32),
        grid=(B, L // tm),
        in_specs=[
            row_spec(D_MODEL), row_spec(D_A), row_spec(D_A), top_spec, bot_spec, row_spec(D_C),
            _mod_spec(layer, mod_row, mod_idx),
            _layer_spec((A_HEADS * CHUNK, CHUNK), layer),
            _layer_spec((CHUNK, D_A), layer),
            _layer_spec((D_B, D_B), layer),
            _layer_spec((D_MODEL, D_MODEL), layer),
            _layer_spec((1, D_MODEL), layer),
            _layer_spec((1, D_MODEL), layer),
        ],
        out_specs=row_spec(D_MODEL),
        compiler_params=_params(("parallel", "parallel")),
        name="out_proj",
    )(x, u, vn, four_top, four_bot, attn, mod, ws, bs, fw, w_out, ln_g, ln_b)


def _ffn_kernel(x_ref, xp_ref, xn_ref, mod_ref, wgu_ref, cw_ref, cb_ref, wd_ref,
                g_ref, b_ref, o_ref, hcat_ref, act_ref, *, seq_len):
    tm = x_ref.shape[0]
    ext = tm + 2 * HALO
    inside_one_seq = seq_len % tm == 0
    m = mod_ref[...]

    def pre_norm(x):
        return (_norm(x) * (1.0 + m[4:5]) + m[3:4]).astype(_BF16)

    h_prev = pre_norm(xp_ref[...])
    h_next = pre_norm(xn_ref[...])
    if inside_one_seq:
        t = pl.program_id(0) % (seq_len // tm)
        hcat_ref[:HALO, :] = jnp.where(t > 0, h_prev, jnp.zeros_like(h_prev))
        hcat_ref[HALO + tm:, :] = jnp.where(t < seq_len // tm - 1, h_next, jnp.zeros_like(h_next))
    else:
        assert seq_len & (seq_len - 1) == 0
        hcat_ref[:HALO, :] = h_prev
        hcat_ref[HALO + tm:, :] = h_next
        pos = lax.broadcasted_iota(jnp.int32, (tm, 1), 0) & (seq_len - 1)
        has_prev = pos != 0
        has_next = pos != seq_len - 1
    for s in range(tm // FFN_SUB_ROWS):
        sub = slice(s * FFN_SUB_ROWS, (s + 1) * FFN_SUB_ROWS)
        hcat_ref[HALO + s * FFN_SUB_ROWS:HALO + (s + 1) * FFN_SUB_ROWS, :] = pre_norm(x_ref[sub, :])
    for j in range(D_FF // FF_CHUNK):
        cols = slice(j * FF_CHUNK, (j + 1) * FF_CHUNK)
        up_cols = slice(D_FF + j * FF_CHUNK, D_FF + (j + 1) * FF_CHUNK)
        g = _dot(hcat_ref[...], wgu_ref[:, cols])
        g_prev = pltpu.roll(g, 1, 0)[HALO:HALO + tm]
        g_next = pltpu.roll(g, ext - 1, 0)[HALO:HALO + tm]
        if not inside_one_seq:
            g_prev = jnp.where(has_prev, g_prev, 0.0)
            g_next = jnp.where(has_next, g_next, 0.0)
        gc = (cb_ref[:, cols] + g_prev * cw_ref[0:1, cols] + g[HALO:HALO + tm] * cw_ref[1:2, cols]
              + g_next * cw_ref[2:3, cols])
        up = _dot(hcat_ref[HALO:HALO + tm, :], wgu_ref[:, up_cols])
        act_ref[:, cols] = ((gc * jax.nn.sigmoid(gc)) * up).astype(_BF16)
    for s in range(tm // FFN_SUB_ROWS):
        sub = slice(s * FFN_SUB_ROWS, (s + 1) * FFN_SUB_ROWS)
        y = _dot(act_ref[sub, :], wd_ref[...])
        o_ref[sub, :] = _norm(ALPHA * x_ref[sub, :] + m[5:6] * y) * g_ref[...] + b_ref[...]


def _conv_ffn(x1, mod, w_gu, conv_w, conv_b, w_down, ln_g, ln_b, *, layer, mod_row, per_seq_mod):
    B, L, _ = x1.shape
    n_tok = B * L
    tm = FFN_ROW_TILE
    assert n_tok % tm == 0 and (L % tm == 0 or (tm % L == 0 and not per_seq_mod))
    x1 = x1.reshape(n_tok, D_MODEL)
    row_spec = pl.BlockSpec((tm, D_MODEL), lambda t: (t, 0))
    per = tm // HALO
    last = n_tok // HALO - 1
    prev_spec = pl.BlockSpec((HALO, D_MODEL), lambda t: (jnp.maximum(t * per - 1, 0), 0))
    next_spec = pl.BlockSpec((HALO, D_MODEL), lambda t: (jnp.minimum((t + 1) * per, last), 0))
    mod_idx = (lambda l, r: (lambda t: (l, r + (t * tm) // L, 0, 0))) if per_seq_mod else None
    out = pl.pallas_call(
        functools.partial(_ffn_kernel, seq_len=L),
        out_shape=jax.ShapeDtypeStruct((n_tok, D_MODEL), _F32),
        grid=(n_tok // tm,),
        in_specs=[
            row_spec, prev_spec, next_spec,
            _mod_spec(layer, mod_row, mod_idx),
            _layer_spec((D_MODEL, 2 * D_FF), layer),
            _layer_spec((3, D_FF), layer),
            _layer_spec((1, D_FF), layer),
            _layer_spec((D_FF, D_MODEL), layer),
            _layer_spec((1, D_MODEL), layer),
            _layer_spec((1, D_MODEL), layer),
        ],
        out_specs=row_spec,
        scratch_shapes=[pltpu.VMEM((tm + 2 * HALO, D_MODEL), _BF16), pltpu.VMEM((tm, D_FF), _BF16)],
        compiler_params=_params(("parallel",)),
        name="conv_ffn",
    )(x1, x1, x1, mod, w_gu, conv_w, conv_b, w_down, ln_g, ln_b)
    return out.reshape(B, L, D_MODEL)


def _dft_tables(n):
    idx = np.arange(n)
    ang = 2.0 * np.pi * ((idx[:, None] * idx[None, :]) % n) / n
    return np.cos(ang) / np.sqrt(n), np.sin(ang) / np.sqrt(n)


def _pos_dft_tables(n):
    cos, sin = _dft_tables(n)
    ti = _dft_tile_rows(n)
    tiles = lambda tab: np.stack([tab[t * ti:t * ti + ti + HALO] for t in range((n // 2) // ti)])
    rev = np.zeros((ti, ti + HALO))
    rev[np.arange(ti), ti - np.arange(ti)] = 1.0
    return tiles(cos), tiles(sin), rev


def _block_diag(blocks):
    n = len(blocks)
    d = blocks[0].shape[0]
    out = np.zeros((n * d, n * d), blocks[0].dtype)
    for g, blk in enumerate(blocks):
        out[g * d:(g + 1) * d, g * d:(g + 1) * d] = blk
    return out


def kernel(x_prompt, x_sample, cache_k, cache_v, c, c_ctx, w_ada, b_ada, w_in, a_ws, a_bs, f_w, rpb,
           w_out, ln1_g, ln1_b, w_gu, conv_w, conv_b, w_down, ln2_g, ln2_b):
    n_ctx_b, ctx_len, _ = x_prompt.shape
    n_lat_b, lat_len, _ = x_sample.shape
    past_len = cache_k.shape[2]

    cs = jnp.zeros((MOD_ROWS, D_MODEL), _F32).at[0].set(c_ctx).at[1:1 + n_lat_b].set(c)
    mod = _modulation(cs, w_ada, b_ada).reshape(DEPTH, MOD_ROWS, 6, D_MODEL)

    cc, sc = _dft_tables(B_GROUP_DIM)
    ch_cos = jnp.asarray(_block_diag([cc] * B_GROUPS), _F32).astype(_BF16)
    ch_sin = jnp.asarray(_block_diag([sc] * B_GROUPS), _F32).astype(_BF16)
    pos_tabs = {n: tuple(jnp.asarray(t, _F32).astype(_BF16) for t in _pos_dft_tables(n))
                for n in (ctx_len, lat_len)}

    w_in_b = w_in.astype(_BF16)
    w_out_b = w_out.astype(_BF16)
    w_gu_b = w_gu.astype(_BF16)
    w_down_b = w_down.astype(_BF16)
    ws = a_ws.reshape(DEPTH, A_HEADS * CHUNK, CHUNK).astype(_BF16)
    bs = jnp.repeat(a_bs.transpose(0, 2, 1), A_HEAD_DIM, axis=2)
    group_eye = jnp.eye(B_GROUPS, dtype=_F32)[None, :, None, :, None]
    fw = (f_w[:, :, :, None, :] * group_eye).reshape(DEPTH, D_B, D_B).astype(_BF16)
    vec = lambda p: p.reshape(DEPTH, 1, p.shape[-1])
    g1, b1, g2, b2, cb = vec(ln1_g), vec(ln1_b), vec(ln2_g), vec(ln2_b), vec(conv_b)
    bias = _na_bias_table(rpb)
    heads_first = (0, 1, 3, 4, 2)
    kct = cache_k.transpose(heads_first).reshape(n_lat_b, DEPTH, D_C, past_len)
    vct = cache_v.transpose(heads_first).reshape(n_lat_b, DEPTH, D_C, past_len)

    xp, xs = x_prompt, x_sample
    new_k, new_v = [], []
    for l in range(DEPTH):
        def layer(x, attend, kv_transposed, **where):
            u, vn, zc, zs, q, k, v = _in_proj(x, mod, w_in_b, ch_cos, ch_sin, kv_transposed=kv_transposed,
                                              **where)
            four_top, four_bot = _pos_dft(*pos_tabs[x.shape[1]], zc, zs)
            attn = attend(q, k, v)
            x1 = _out_proj(x, u, vn, four_top, four_bot, attn, mod, ws, bs, fw, w_out_b, g1, b1, **where)
            x2 = _conv_ffn(x1, mod, w_gu_b, conv_w, cb, w_down_b, g2, b2, **where)
            return x2, k, v

        xp, k_l, v_l = layer(xp, _ctx_attention, True, layer=l, mod_row=0, per_seq_mod=False)
        new_k.append(k_l)
        new_v.append(v_l)
        xs, _, _ = layer(xs, functools.partial(_na_attention, kct=kct, vct=vct, bias=bias, layer=l), False,
                         layer=l, mod_row=1, per_seq_mod=True)

    def kv_out(per_layer):
        t = jnp.stack(per_layer, axis=1).reshape(n_ctx_b, DEPTH, C_HEADS, C_HEAD_DIM, ctx_len)
        return t.transpose(0, 1, 4, 2, 3)

    return (xp, xs, kv_out(new_k), kv_out(new_v))
```

```python
import functools

import numpy as np
import jax
import jax.numpy as jnp
from jax import lax
from jax.experimental import pallas as pl
from jax.experimental.pallas import tpu as pltpu

D_MODEL = 1024
DEPTH = 2
GRID_W = 64
D_A = 256
A_HEADS = 4
A_HEAD_DIM = 64
CHUNK = 128
D_B = 256
B_GROUPS = 4
B_GROUP_DIM = 64
D_C = 512
C_HEAD_DIM = 64
C_HEADS = 8
P_IN = 2 * D_A + D_B + 3 * D_C
NA_ROWS = 8
NA_KW = 16
D_FF = 2816
ALPHA = (2 * DEPTH) ** 0.25
LN_EPS = 1e-6
LOG2_E = 1.4426950408889634
Q_SCALE = C_HEAD_DIM ** -0.5 * LOG2_E

LANES = 128
HEAD_PAIR = 2 * C_HEAD_DIM
N_HEAD_PAIRS = C_HEADS // 2
MOD_ROWS = 16
IN_ROW_TILE = 1024
OUT_ROW_TILE = 1024
OUT_SUB_ROWS = 256
FFN_ROW_TILE = 1024
FFN_SUB_ROWS = 256
FF_CHUNK = 256
HALO = 16
VMEM_LIMIT = 56 * 1024 * 1024

_F32 = jnp.float32
_BF16 = jnp.bfloat16


def _dot(a, b):
    return jnp.dot(a, b, preferred_element_type=_F32)


def _dot_nt(a, b):
    return lax.dot_general(a, b, (((1,), (1,)), ((), ())), preferred_element_type=_F32)


def _norm(x):
    mu = jnp.mean(x, axis=-1, keepdims=True)
    xc = x - mu
    var = jnp.mean(xc * xc, axis=-1, keepdims=True)
    return xc * lax.rsqrt(var + LN_EPS)


def _params(sem):
    return pltpu.CompilerParams(dimension_semantics=sem, vmem_limit_bytes=VMEM_LIMIT)


def _const_spec(shape):
    nd = len(shape)
    return pl.BlockSpec(shape, lambda *_: (0,) * nd, pipeline_mode=pl.Buffered(1))


def _layer_spec(shape, layer):
    nd = len(shape)
    return pl.BlockSpec((None,) + shape, lambda *_: (layer,) + (0,) * nd, pipeline_mode=pl.Buffered(1))


def _mod_spec(layer, first_row, rows_per_seq):
    if rows_per_seq is None:
        return pl.BlockSpec((None, None, 6, D_MODEL), lambda *_: (layer, first_row, 0, 0))
    return pl.BlockSpec((None, None, 6, D_MODEL), rows_per_seq(layer, first_row))


def _mod_kernel(c_ref, w_ref, b_ref, o_ref):
    c = c_ref[...]
    s = c * jax.nn.sigmoid(c)
    o_ref[0] = _dot(s.astype(_BF16), w_ref[0].astype(_BF16)) + b_ref[0]


def _modulation(cs, w_ada, b_ada):
    n = cs.shape[0]
    tn = 1536
    return pl.pallas_call(
        _mod_kernel,
        out_shape=jax.ShapeDtypeStruct((DEPTH, n, 6 * D_MODEL), _F32),
        grid=(DEPTH, 6 * D_MODEL // tn),
        in_specs=[
            pl.BlockSpec((n, D_MODEL), lambda l, j: (0, 0)),
            pl.BlockSpec((1, D_MODEL, tn), lambda l, j: (l, 0, j)),
            pl.BlockSpec((1, 1, tn), lambda l, j: (l, 0, j)),
        ],
        out_specs=pl.BlockSpec((1, n, tn), lambda l, j: (l, 0, j)),
        compiler_params=_params(("parallel", "parallel")),
        name="adaln_mod",
    )(cs, w_ada, b_ada.reshape(DEPTH, 1, 6 * D_MODEL))


def _tile_split(B, L, tile_rows):
    if L >= tile_rows:
        assert L % tile_rows == 0
        return 1, tile_rows
    nb = min(tile_rows // L, B)
    assert B % nb == 0
    return nb, L


def _in_kernel(x_ref, mod_ref, w_ref, cc_ref, cs_ref,
               u_ref, vn_ref, zc_ref, zs_ref, q_ref, k_ref, v_ref, *, kv_transposed):
    nb, tr, _ = x_ref.shape
    x = x_ref[...].reshape(nb * tr, D_MODEL)
    m = mod_ref[...]
    h = _norm(x) * (1.0 + m[1:2]) + m[0:1]
    z = _dot(h.astype(_BF16), w_ref[...])
    za = jax.nn.gelu(z[:, :2 * D_A])
    u_ref[...] = za[:, :D_A].reshape(nb, tr, D_A)
    vn_ref[...] = _norm(za[:, D_A:]).astype(_BF16).reshape(nb, tr, D_A)
    o1 = 2 * D_A
    zb = z[:, o1:o1 + D_B].astype(_BF16)
    zc = _dot(zb, cc_ref[...]).astype(_BF16)
    zs = _dot(zb, cs_ref[...]).astype(_BF16)
    o2 = o1 + D_B
    q_ref[...] = (z[:, o2:o2 + D_C] * Q_SCALE).astype(q_ref.dtype).reshape(nb, tr, D_C)
    k = z[:, o2 + D_C:o2 + 2 * D_C]
    v = z[:, o2 + 2 * D_C:o2 + 3 * D_C]
    for s in range(nb):
        rows = slice(s * tr, (s + 1) * tr)
        zc_ref[:, s * D_B:(s + 1) * D_B] = zc[rows]
        zs_ref[:, s * D_B:(s + 1) * D_B] = zs[rows]
        if kv_transposed:
            k_ref[s] = k[rows].T.astype(k_ref.dtype)
            v_ref[s] = v[rows].T.astype(v_ref.dtype)
        else:
            k_ref[s] = k[rows].astype(k_ref.dtype)
            v_ref[s] = v[rows].astype(v_ref.dtype)


def _in_proj(x, mod, w_in, ch_cos, ch_sin, *, layer, mod_row, per_seq_mod, kv_transposed):
    B, L, _ = x.shape
    nb, tr = _tile_split(B, L, IN_ROW_TILE)
    assert nb == 1 or not per_seq_mod
    row = lambda w, dt: jax.ShapeDtypeStruct((B, L, w), dt)
    row_spec = lambda w: pl.BlockSpec((nb, tr, w), lambda g, i: (g, i, 0))
    four = jax.ShapeDtypeStruct((L, B * D_B), _BF16)
    four_spec = pl.BlockSpec((tr, nb * D_B), lambda g, i: (i, g))
    mod_idx = (lambda l, r: (lambda g, i: (l, r + g, 0, 0))) if per_seq_mod else None
    if kv_transposed:
        kv = jax.ShapeDtypeStruct((B, D_C, L), _F32)
        kv_spec = pl.BlockSpec((nb, D_C, tr), lambda g, i: (g, 0, i))
    else:
        kv = row(D_C, _BF16)
        kv_spec = row_spec(D_C)
    return pl.pallas_call(
        functools.partial(_in_kernel, kv_transposed=kv_transposed),
        out_shape=(row(D_A, _F32), row(D_A, _BF16), four, four, row(D_C, _BF16), kv, kv),
        grid=(B // nb, L // tr),
        in_specs=[
            row_spec(D_MODEL),
            _mod_spec(layer, mod_row, mod_idx),
            _layer_spec((D_MODEL, P_IN), layer),
            _const_spec((D_B, D_B)),
            _const_spec((D_B, D_B)),
        ],
        out_specs=(row_spec(D_A), row_spec(D_A), four_spec, four_spec, row_spec(D_C), kv_spec, kv_spec),
        compiler_params=_params(("parallel", "parallel")),
        name="in_proj",
    )(x, mod, w_in, ch_cos, ch_sin)


def _dft_kernel(c_ref, s_ref, zc_ref, zs_ref, rev_ref, top_ref, bot_ref):
    p = _dot(c_ref[0], zc_ref[...])
    q = _dot(s_ref[0], zs_ref[...])
    top_ref[...] = (p - q)[:top_ref.shape[0]].astype(top_ref.dtype)
    bot_ref[...] = _dot(rev_ref[...], (p + q).astype(_BF16)).astype(bot_ref.dtype)


def _dft_tile_rows(L):
    return min(L // 2, 512)


def _pos_dft(pos_cos, pos_sin, rev, zc, zs):
    L, N = zc.shape
    ti = _dft_tile_rows(L)
    n_t = (L // 2) // ti
    tj = 512 if L > 512 else 1024
    half = jax.ShapeDtypeStruct((L // 2, N), _BF16)
    tab_spec = pl.BlockSpec((1, ti + HALO, L), lambda t, j: (t, 0, 0))
    return pl.pallas_call(
        _dft_kernel,
        out_shape=(half, half),
        grid=(n_t, N // tj),
        in_specs=[
            tab_spec, tab_spec,
            pl.BlockSpec((L, tj), lambda t, j: (0, j)),
            pl.BlockSpec((L, tj), lambda t, j: (0, j)),
            pl.BlockSpec((ti, ti + HALO), lambda t, j: (0, 0)),
        ],
        out_specs=(pl.BlockSpec((ti, tj), lambda t, j: (t, j)),
                   pl.BlockSpec((ti, tj), lambda t, j: (n_t - 1 - t, j))),
        compiler_params=_params(("parallel", "parallel")),
        name="pos_dft",
    )(pos_cos, pos_sin, zc, zs, rev)


def _head_lane_mask(rows, head):
    lane = lax.broadcasted_iota(jnp.int32, (rows, HEAD_PAIR), 1)
    return (lane >= head * C_HEAD_DIM) & (lane < (head + 1) * C_HEAD_DIM)


def _ctx_attn_kernel(q_ref, kt_ref, vt_ref, o_ref):
    rows = q_ref.shape[1]
    for pair in range(N_HEAD_PAIRS):
        lanes = slice(pair * HEAD_PAIR, (pair + 1) * HEAD_PAIR)
        q = q_ref[0, :, lanes]
        kt = kt_ref[0, lanes, :].astype(_BF16)
        vt = vt_ref[0, lanes, :].astype(_BF16)
        out = jnp.zeros((rows, HEAD_PAIR), _F32)
        for head in range(2):
            sel = _head_lane_mask(rows, head)
            s = _dot(jnp.where(sel, q, jnp.zeros_like(q)), kt)
            p = jnp.exp2(s - jnp.max(s, axis=-1, keepdims=True))
            o = _dot_nt(p.astype(_BF16), vt) / jnp.sum(p, axis=-1, keepdims=True)
            out = jnp.where(sel, o, out)
        o_ref[0, :, lanes] = out.astype(o_ref.dtype)


def _ctx_attention(q, kt, vt):
    B, L, _ = q.shape
    spec = pl.BlockSpec((1, L, D_C), lambda b: (b, 0, 0))
    spec_t = pl.BlockSpec((1, D_C, L), lambda b: (b, 0, 0))
    return pl.pallas_call(
        _ctx_attn_kernel,
        out_shape=jax.ShapeDtypeStruct((B, L, D_C), _BF16),
        grid=(B,),
        in_specs=[spec, spec_t, spec_t],
        out_specs=spec,
        compiler_params=_params(("parallel",)),
        name="ctx_attention",
    )(q, kt, vt)


def _na_kernel(q_ref, k_ref, v_ref, kct_ref, vct_ref, bias_ref, o_ref, *, n_rows):
    kct = kct_ref[...].astype(_BF16)
    vct = vct_ref[...].astype(_BF16)
    win_keys = NA_ROWS * GRID_W
    pair_rows = 2 * GRID_W
    n_q = n_rows * GRID_W

    windows = []
    for r in range(n_rows):
        rs = min(max(r - NA_ROWS // 2, 0), n_rows - NA_ROWS)
        windows.append((slice(rs * GRID_W, rs * GRID_W + win_keys), rs - r + NA_ROWS - 1))

    q = q_ref[0]
    sel0 = _head_lane_mask(n_q, 0)
    q_heads = (jnp.where(sel0, q, jnp.zeros_like(q)), jnp.where(sel0, jnp.zeros_like(q), q))
    lhs = jnp.concatenate([qh[r * GRID_W:(r + 1) * GRID_W]
                           for r in range(n_rows) for qh in q_heads], axis=0)
    s_ctx = _dot(lhs, kct)
    s_loc = []
    for r, (win, base) in enumerate(windows):
        bias = jnp.concatenate(
            [jnp.concatenate([bias_ref[head, base + 2 * t] for t in range(NA_ROWS // 2)], axis=1)
             for head in range(2)], axis=0)
        s_loc.append(_dot_nt(lhs[r * pair_rows:(r + 1) * pair_rows], k_ref[0, win, :]) + bias)
    s_loc = jnp.concatenate(s_loc, axis=0)
    m = jnp.maximum(jnp.max(s_loc, axis=-1, keepdims=True), jnp.max(s_ctx, axis=-1, keepdims=True))
    p_loc = jnp.exp2(s_loc - m)
    p_ctx = jnp.exp2(s_ctx - m)
    denom = jnp.sum(p_loc, axis=-1, keepdims=True) + jnp.sum(p_ctx, axis=-1, keepdims=True)
    p_loc = p_loc.astype(_BF16)
    o_loc = jnp.concatenate(
        [_dot(p_loc[r * pair_rows:(r + 1) * pair_rows], v_ref[0, win, :])
         for r, (win, _) in enumerate(windows)], axis=0)
    o = (o_loc + _dot_nt(p_ctx.astype(_BF16), vct)) / denom
    row_sel0 = _head_lane_mask(GRID_W, 0)
    o_ref[0] = jnp.concatenate(
        [jnp.where(row_sel0, o[r * pair_rows:r * pair_rows + GRID_W],
                   o[r * pair_rows + GRID_W:(r + 1) * pair_rows])
         for r in range(n_rows)], axis=0).astype(o_ref.dtype)


def _na_bias_kernel(v_ref, o_ref):
    n_ri = 2 * NA_ROWS - 1
    qcol = lax.broadcasted_iota(jnp.int32, (GRID_W, LANES), 0)
    lane = lax.broadcasted_iota(jnp.int32, (GRID_W, LANES), 1)
    kcol = lane & (GRID_W - 1)
    start = jnp.clip(qcol - NA_KW // 2, 0, GRID_W - NA_KW)
    in_win = (kcol >= start) & (kcol < start + NA_KW)
    for head in range(v_ref.shape[0]):
        rows = [jnp.broadcast_to(v_ref[head, ri], (GRID_W, LANES)) for ri in range(n_ri)]
        for ri in range(n_ri - 1):
            left = pltpu.roll(rows[ri], GRID_W + 1, 1, stride=1, stride_axis=0)
            right = pltpu.roll(rows[ri + 1], 1, 1, stride=1, stride_axis=0)
            o_ref[head, ri] = jnp.where(in_win, jnp.where(lane < GRID_W, left, right) * LOG2_E, -jnp.inf)


def _na_bias_table(rpb):
    n_ri = 2 * NA_ROWS - 1
    lo = GRID_W - NA_KW
    v = jnp.pad(rpb, ((0, 0), (0, 0), (0, 0), (lo, LANES - lo - (2 * NA_KW - 1))))
    v = v.reshape(DEPTH * C_HEADS, n_ri, 1, LANES)
    return pl.pallas_call(
        _na_bias_kernel,
        out_shape=jax.ShapeDtypeStruct((DEPTH * C_HEADS, n_ri - 1, GRID_W, LANES), _F32),
        grid=(DEPTH,),
        in_specs=[pl.BlockSpec((C_HEADS, n_ri, 1, LANES), lambda i: (i, 0, 0, 0))],
        out_specs=pl.BlockSpec((C_HEADS, n_ri - 1, GRID_W, LANES), lambda i: (i, 0, 0, 0)),
        compiler_params=_params(("parallel",)),
        name="na_bias",
    )(v)


def _na_attention(q, k, v, kct, vct, bias, *, layer):
    B, L, _ = q.shape
    Lc = kct.shape[3]
    n_rows = L // GRID_W
    assert n_rows >= NA_ROWS and 2 * GRID_W == LANES
    lat = pl.BlockSpec((1, L, HEAD_PAIR), lambda p, b: (b, 0, p))
    ctx = pl.BlockSpec((None, None, HEAD_PAIR, Lc), lambda p, b: (b, layer, p, 0))
    return pl.pallas_call(
        functools.partial(_na_kernel, n_rows=n_rows),
        out_shape=jax.ShapeDtypeStruct((B, L, D_C), _BF16),
        grid=(N_HEAD_PAIRS, B),
        in_specs=[lat, lat, lat, ctx, ctx,
                  pl.BlockSpec((2, 2 * NA_ROWS - 2, GRID_W, LANES),
                               lambda p, b: (layer * N_HEAD_PAIRS + p, 0, 0, 0))],
        out_specs=lat,
        compiler_params=_params(("parallel", "parallel")),
        name="na_attention",
    )(q, k, v, kct, vct, bias)


def _out_kernel(x_ref, u_ref, vn_ref, ft_ref, fb_ref, a_ref, mod_ref, ws_ref, bs_ref, fw_ref, wo_ref,
                g_ref, b_ref, x1_ref, *, tiles_per_half):
    nb, tr, _ = x_ref.shape
    lane = lax.broadcasted_iota(jnp.int32, (CHUNK, D_A), 1)
    m = mod_ref[...]
    if tiles_per_half == 0:
        four = jnp.concatenate([half[:, s * D_B:(s + 1) * D_B]
                                for s in range(nb) for half in (ft_ref, fb_ref)], axis=0)
    else:
        four = jnp.where(pl.program_id(1) < tiles_per_half, ft_ref[...], fb_ref[...])

    def rows_of(ref, start, n):
        return ref[start // tr, start % tr:start % tr + n, :]

    for s in range(nb * tr // OUT_SUB_ROWS):
        sub = slice(s * OUT_SUB_ROWS, (s + 1) * OUT_SUB_ROWS)
        ya = []
        for n in range(OUT_SUB_ROWS // CHUNK):
            start = s * OUT_SUB_ROWS + n * CHUNK
            r = _dot(ws_ref[...], rows_of(vn_ref, start, CHUNK))
            mixed = r[:CHUNK]
            for hd in range(1, A_HEADS):
                mixed = jnp.where(lane >= hd * A_HEAD_DIM, r[hd * CHUNK:(hd + 1) * CHUNK], mixed)
            ya.append(rows_of(u_ref, start, CHUNK) * (mixed + bs_ref[...]))
        ya = jnp.concatenate(ya, axis=0).astype(_BF16)
        yb = _dot(four[sub], fw_ref[...]).astype(_BF16)
        y = _dot(jnp.concatenate([ya, yb, rows_of(a_ref, s * OUT_SUB_ROWS, OUT_SUB_ROWS)], axis=1), wo_ref[...])
        x1 = (_norm(ALPHA * rows_of(x_ref, s * OUT_SUB_ROWS, OUT_SUB_ROWS) + m[2:3] * y) * g_ref[...]
              + b_ref[...])
        off = (s * OUT_SUB_ROWS) % tr
        x1_ref[(s * OUT_SUB_ROWS) // tr, off:off + OUT_SUB_ROWS, :] = x1


def _out_proj(x, u, vn, four_top, four_bot, attn, mod, ws, bs, fw, w_out, ln_g, ln_b, *,
              layer, mod_row, per_seq_mod):
    B, L, _ = x.shape
    nb, tr = _tile_split(B, L, OUT_ROW_TILE)
    assert (nb == 1 or not per_seq_mod) and tr % OUT_SUB_ROWS == 0
    row_spec = lambda w: pl.BlockSpec((nb, tr, w), lambda g, i: (g, i, 0))
    mod_idx = (lambda l, r: (lambda g, i: (l, r + g, 0, 0))) if per_seq_mod else None
    nh = (L // 2) // tr
    assert tr == L or (L // 2) % tr == 0
    if nh == 0:
        top_spec = bot_spec = pl.BlockSpec((L // 2, nb * D_B), lambda g, i: (0, g))
    else:
        top_spec = pl.BlockSpec((tr, D_B), lambda g, i: (jnp.minimum(i, nh - 1), g))
        bot_spec = pl.BlockSpec((tr, D_B), lambda g, i: (jnp.maximum(i - nh, 0), g))
    return pl.pallas_call(
        functools.partial(_out_kernel, tiles_per_half=nh),
        out_shape=jax.ShapeDtypeStruct((B, L, D_MODEL), _F32),
        grid=(B // nb, L // tr),
        in_specs=[
            row_spec(D_MODEL), row_spec(D_A), row_spec(D_A), top_spec, bot_spec, row_spec(D_C),
            _mod_spec(layer, mod_row, mod_idx),
            _layer_spec((A_HEADS * CHUNK, CHUNK), layer),
            _layer_spec((CHUNK, D_A), layer),
            _layer_spec((D_B, D_B), layer),
            _layer_spec((D_MODEL, D_MODEL), layer),
            _layer_spec((1, D_MODEL), layer),
            _layer_spec((1, D_MODEL), layer),
        ],
        out_specs=row_spec(D_MODEL),
        compiler_params=_params(("parallel", "parallel")),
        name="out_proj",
    )(x, u, vn, four_top, four_bot, attn, mod, ws, bs, fw, w_out, ln_g, ln_b)


def _ffn_kernel(x_ref, xp_ref, xn_ref, mod_ref, wgu_ref, cw_ref, cb_ref, wd_ref,
                g_ref, b_ref, o_ref, hcat_ref, act_ref, *, seq_len):
    tm = x_ref.shape[0]
    ext = tm + 2 * HALO
    inside_one_seq = seq_len % tm == 0
    m = mod_ref[...]

    def pre_norm(x):
        return (_norm(x) * (1.0 + m[4:5]) + m[3:4]).astype(_BF16)

    h_prev = pre_norm(xp_ref[...])
    h_next = pre_norm(xn_ref[...])
    if inside_one_seq:
        t = pl.program_id(0) % (seq_len // tm)
        hcat_ref[:HALO, :] = jnp.where(t > 0, h_prev, jnp.zeros_like(h_prev))
        hcat_ref[HALO + tm:, :] = jnp.where(t < seq_len // tm - 1, h_next, jnp.zeros_like(h_next))
    else:
        assert seq_len & (seq_len - 1) == 0
        hcat_ref[:HALO, :] = h_prev
        hcat_ref[HALO + tm:, :] = h_next
        pos = lax.broadcasted_iota(jnp.int32, (tm, 1), 0) & (seq_len - 1)
        has_prev = pos != 0
        has_next = pos != seq_len - 1
    for s in range(tm // FFN_SUB_ROWS):
        sub = slice(s * FFN_SUB_ROWS, (s + 1) * FFN_SUB_ROWS)
        hcat_ref[HALO + s * FFN_SUB_ROWS:HALO + (s + 1) * FFN_SUB_ROWS, :] = pre_norm(x_ref[sub, :])
    for j in range(D_FF // FF_CHUNK):
        cols = slice(j * FF_CHUNK, (j + 1) * FF_CHUNK)
        up_cols = slice(D_FF + j * FF_CHUNK, D_FF + (j + 1) * FF_CHUNK)
        g = _dot(hcat_ref[...], wgu_ref[:, cols])
        g_prev = pltpu.roll(g, 1, 0)[HALO:HALO + tm]
        g_next = pltpu.roll(g, ext - 1, 0)[HALO:HALO + tm]
        if not inside_one_seq:
            g_prev = jnp.where(has_prev, g_prev, 0.0)
            g_next = jnp.where(has_next, g_next, 0.0)
        gc = (cb_ref[:, cols] + g_prev * cw_ref[0:1, cols] + g[HALO:HALO + tm] * cw_ref[1:2, cols]
              + g_next * cw_ref[2:3, cols])
        up = _dot(hcat_ref[HALO:HALO + tm, :], wgu_ref[:, up_cols])
        act_ref[:, cols] = ((gc * jax.nn.sigmoid(gc)) * up).astype(_BF16)
    for s in range(tm // FFN_SUB_ROWS):
        sub = slice(s * FFN_SUB_ROWS, (s + 1) * FFN_SUB_ROWS)
        y = _dot(act_ref[sub, :], wd_ref[...])
        o_ref[sub, :] = _norm(ALPHA * x_ref[sub, :] + m[5:6] * y) * g_ref[...] + b_ref[...]


def _conv_ffn(x1, mod, w_gu, conv_w, conv_b, w_down, ln_g, ln_b, *, layer, mod_row, per_seq_mod):
    B, L, _ = x1.shape
    n_tok = B * L
    tm = FFN_ROW_TILE
    assert n_tok % tm == 0 and (L % tm == 0 or (tm % L == 0 and not per_seq_mod))
    x1 = x1.reshape(n_tok, D_MODEL)
    row_spec = pl.BlockSpec((tm, D_MODEL), lambda t: (t, 0))
    per = tm // HALO
    last = n_tok // HALO - 1
    prev_spec = pl.BlockSpec((HALO, D_MODEL), lambda t: (jnp.maximum(t * per - 1, 0), 0))
    next_spec = pl.BlockSpec((HALO, D_MODEL), lambda t: (jnp.minimum((t + 1) * per, last), 0))
    mod_idx = (lambda l, r: (lambda t: (l, r + (t * tm) // L, 0, 0))) if per_seq_mod else None
    out = pl.pallas_call(
        functools.partial(_ffn_kernel, seq_len=L),
        out_shape=jax.ShapeDtypeStruct((n_tok, D_MODEL), _F32),
        grid=(n_tok // tm,),
        in_specs=[
            row_spec, prev_spec, next_spec,
            _mod_spec(layer, mod_row, mod_idx),
            _layer_spec((D_MODEL, 2 * D_FF), layer),
            _layer_spec((3, D_FF), layer),
            _layer_spec((1, D_FF), layer),
            _layer_spec((D_FF, D_MODEL), layer),
            _layer_spec((1, D_MODEL), layer),
            _layer_spec((1, D_MODEL), layer),
        ],
        out_specs=row_spec,
        scratch_shapes=[pltpu.VMEM((tm + 2 * HALO, D_MODEL), _BF16), pltpu.VMEM((tm, D_FF), _BF16)],
        compiler_params=_params(("parallel",)),
        name="conv_ffn",
    )(x1, x1, x1, mod, w_gu, conv_w, conv_b, w_down, ln_g, ln_b)
    return out.reshape(B, L, D_MODEL)


def _dft_tables(n):
    idx = np.arange(n)
    ang = 2.0 * np.pi * ((idx[:, None] * idx[None, :]) % n) / n
    return np.cos(ang) / np.sqrt(n), np.sin(ang) / np.sqrt(n)


def _pos_dft_tables(n):
    cos, sin = _dft_tables(n)
    ti = _dft_tile_rows(n)
    tiles = lambda tab: np.stack([tab[t * ti:t * ti + ti + HALO] for t in range((n // 2) // ti)])
    rev = np.zeros((ti, ti + HALO))
    rev[np.arange(ti), ti - np.arange(ti)] = 1.0
    return tiles(cos), tiles(sin), rev


def _block_diag(blocks):
    n = len(blocks)
    d = blocks[0].shape[0]
    out = np.zeros((n * d, n * d), blocks[0].dtype)
    for g, blk in enumerate(blocks):
        out[g * d:(g + 1) * d, g * d:(g + 1) * d] = blk
    return out


def kernel(x_prompt, x_sample, cache_k, cache_v, c, c_ctx, w_ada, b_ada, w_in, a_ws, a_bs, f_w, rpb,
           w_out, ln1_g, ln1_b, w_gu, conv_w, conv_b, w_down, ln2_g, ln2_b):
    n_ctx_b, ctx_len, _ = x_prompt.shape
    n_lat_b, lat_len, _ = x_sample.shape
    past_len = cache_k.shape[2]

    cs = jnp.zeros((MOD_ROWS, D_MODEL), _F32).at[0].set(c_ctx).at[1:1 + n_lat_b].set(c)
    mod = _modulation(cs, w_ada, b_ada).reshape(DEPTH, MOD_ROWS, 6, D_MODEL)

    cc, sc = _dft_tables(B_GROUP_DIM)
    ch_cos = jnp.asarray(_block_diag([cc] * B_GROUPS), _F32).astype(_BF16)
    ch_sin = jnp.asarray(_block_diag([sc] * B_GROUPS), _F32).astype(_BF16)
    pos_tabs = {n: tuple(jnp.asarray(t, _F32).astype(_BF16) for t in _pos_dft_tables(n))
                for n in (ctx_len, lat_len)}

    w_in_b = w_in.astype(_BF16)
    w_out_b = w_out.astype(_BF16)
    w_gu_b = w_gu.astype(_BF16)
    w_down_b = w_down.astype(_BF16)
    ws = a_ws.reshape(DEPTH, A_HEADS * CHUNK, CHUNK).astype(_BF16)
    bs = jnp.repeat(a_bs.transpose(0, 2, 1), A_HEAD_DIM, axis=2)
    group_eye = jnp.eye(B_GROUPS, dtype=_F32)[None, :, None, :, None]
    fw = (f_w[:, :, :, None, :] * group_eye).reshape(DEPTH, D_B, D_B).astype(_BF16)
    vec = lambda p: p.reshape(DEPTH, 1, p.shape[-1])
    g1, b1, g2, b2, cb = vec(ln1_g), vec(ln1_b), vec(ln2_g), vec(ln2_b), vec(conv_b)
    bias = _na_bias_table(rpb)
    heads_first = (0, 1, 3, 4, 2)
    kct = cache_k.transpose(heads_first).reshape(n_lat_b, DEPTH, D_C, past_len)
    vct = cache_v.transpose(heads_first).reshape(n_lat_b, DEPTH, D_C, past_len)

    xp, xs = x_prompt, x_sample
    new_k, new_v = [], []
    for l in range(DEPTH):
        def layer(x, attend, kv_transposed, **where):
            u, vn, zc, zs, q, k, v = _in_proj(x, mod, w_in_b, ch_cos, ch_sin, kv_transposed=kv_transposed,
                                              **where)
            four_top, four_bot = _pos_dft(*pos_tabs[x.shape[1]], zc, zs)
            attn = attend(q, k, v)
            x1 = _out_proj(x, u, vn, four_top, four_bot, attn, mod, ws, bs, fw, w_out_b, g1, b1, **where)
            x2 = _conv_ffn(x1, mod, w_gu_b, conv_w, cb, w_down_b, g2, b2, **where)
            return x2, k, v

        xp, k_l, v_l = layer(xp, _ctx_attention, True, layer=l, mod_row=0, per_seq_mod=False)
        new_k.append(k_l)
        new_v.append(v_l)
        xs, _, _ = layer(xs, functools.partial(_na_attention, kct=kct, vct=vct, bias=bias, layer=l), False,
                         layer=l, mod_row=1, per_seq_mod=True)

    def kv_out(per_layer):
        t = jnp.stack(per_layer, axis=1).reshape(n_ctx_b, DEPTH, C_HEADS, C_HEAD_DIM, ctx_len)
        return t.transpose(0, 1, 4, 2, 3)

    return (xp, xs, kv_out(new_k), kv_out(new_v))
```

```python
import functools

import numpy as np
import jax
import jax.numpy as jnp
from jax import lax
from jax.experimental import pallas as pl
from jax.experimental.pallas import tpu as pltpu

D_MODEL = 1024
DEPTH = 2
GRID_W = 64
D_A = 256
A_HEADS = 4
A_HEAD_DIM = 64
CHUNK = 128
D_B = 256
B_GROUPS = 4
B_GROUP_DIM = 64
D_C = 512
C_HEAD_DIM = 64
C_HEADS = 8
P_IN = 2 * D_A + D_B + 3 * D_C
NA_ROWS = 8
NA_KW = 16
D_FF = 2816
ALPHA = (2 * DEPTH) ** 0.25
LN_EPS = 1e-6
LOG2_E = 1.4426950408889634
Q_SCALE = C_HEAD_DIM ** -0.5 * LOG2_E

LANES = 128
V7X_VMEM_BYTES = 64 * 1024 * 1024
VMEM_LIMIT = V7X_VMEM_BYTES * 7 // 8
HEAD_PAIR = 2 * C_HEAD_DIM
N_HEAD_PAIRS = C_HEADS // 2
MOD_ROWS = 16
MOD_COL_TILE = 1536
IN_ROW_TILE = 1024
IN_SUB_ROWS = 256
OUT_ROW_TILE = 1024
OUT_SUB_ROWS = 256
FFN_ROW_TILE = 1024
FFN_SUB_ROWS = 256
FF_CHUNK = 256
HALO = 16
DFT_ROW_TILE = 512
DFT_COL_TILE = 1024

_F32 = jnp.float32
_BF16 = jnp.bfloat16


def _dot(a, b):
    return jnp.dot(a, b, preferred_element_type=_F32)


def _dot_nt(a, b):
    return lax.dot_general(a, b, (((1,), (1,)), ((), ())), preferred_element_type=_F32)


def _norm(x):
    mu = jnp.mean(x, axis=-1, keepdims=True)
    xc = x - mu
    var = jnp.mean(xc * xc, axis=-1, keepdims=True)
    return xc * lax.rsqrt(var + LN_EPS)


def _params(sem):
    return pltpu.CompilerParams(dimension_semantics=sem, vmem_limit_bytes=VMEM_LIMIT)


def _const_spec(shape):
    nd = len(shape)
    return pl.BlockSpec(shape, lambda *_: (0,) * nd, pipeline_mode=pl.Buffered(1))


def _layer_spec(shape, layer):
    nd = len(shape)
    return pl.BlockSpec((None,) + shape, lambda *_: (layer,) + (0,) * nd, pipeline_mode=pl.Buffered(1))


def _mod_spec(layer, first_row, rows_per_seq):
    if rows_per_seq is None:
        return pl.BlockSpec((None, None, 6, D_MODEL), lambda *_: (layer, first_row, 0, 0))
    return pl.BlockSpec((None, None, 6, D_MODEL), rows_per_seq(layer, first_row))


def _mod_kernel(c_ref, w_ref, b_ref, o_ref):
    c = c_ref[...]
    s = c * jax.nn.sigmoid(c)
    o_ref[0] = _dot(s.astype(_BF16), w_ref[0].astype(_BF16)) + b_ref[0]


def _modulation(cs, w_ada, b_ada):
    n = cs.shape[0]
    tn = MOD_COL_TILE
    return pl.pallas_call(
        _mod_kernel,
        out_shape=jax.ShapeDtypeStruct((DEPTH, n, 6 * D_MODEL), _F32),
        grid=(DEPTH, 6 * D_MODEL // tn),
        in_specs=[
            pl.BlockSpec((n, D_MODEL), lambda l, j: (0, 0)),
            pl.BlockSpec((1, D_MODEL, tn), lambda l, j: (l, 0, j)),
            pl.BlockSpec((1, 1, tn), lambda l, j: (l, 0, j)),
        ],
        out_specs=pl.BlockSpec((1, n, tn), lambda l, j: (l, 0, j)),
        compiler_params=_params(("parallel", "parallel")),
        name="adaln_mod",
    )(cs, w_ada, b_ada.reshape(DEPTH, 1, 6 * D_MODEL))


def _tile_split(B, L, tile_rows):
    if L >= tile_rows:
        assert L % tile_rows == 0
        return 1, tile_rows
    nb = min(tile_rows // L, B)
    assert B % nb == 0
    return nb, L


def _in_kernel(x_ref, mod_ref, w_ref, cc_ref, cs_ref,
               u_ref, vn_ref, zc_ref, zs_ref, q_ref, k_ref, v_ref, *, kv_transposed):
    nb, tr, _ = x_ref.shape
    m = mod_ref[...]
    o1 = 2 * D_A
    o2 = o1 + D_B
    for t in range(nb * tr // IN_SUB_ROWS):
        seq = (t * IN_SUB_ROWS) // tr
        rows = slice((t * IN_SUB_ROWS) % tr, (t * IN_SUB_ROWS) % tr + IN_SUB_ROWS)
        h = _norm(x_ref[seq, rows, :]) * (1.0 + m[1:2]) + m[0:1]
        z = _dot(h.astype(_BF16), w_ref[...])
        za = jax.nn.gelu(z[:, :o1])
        u_ref[seq, rows, :] = za[:, :D_A]
        vn_ref[seq, rows, :] = _norm(za[:, D_A:]).astype(_BF16)
        zb = z[:, o1:o2].astype(_BF16)
        zc_ref[rows, seq * D_B:(seq + 1) * D_B] = _dot(zb, cc_ref[...]).astype(_BF16)
        zs_ref[rows, seq * D_B:(seq + 1) * D_B] = _dot(zb, cs_ref[...]).astype(_BF16)
        q_ref[seq, rows, :] = (z[:, o2:o2 + D_C] * Q_SCALE).astype(q_ref.dtype)
        k = z[:, o2 + D_C:o2 + 2 * D_C]
        v = z[:, o2 + 2 * D_C:o2 + 3 * D_C]
        if kv_transposed:
            k_ref[seq, :, rows] = k.T.astype(k_ref.dtype)
            v_ref[seq, :, rows] = v.T.astype(v_ref.dtype)
        else:
            k_ref[seq, rows, :] = k.astype(k_ref.dtype)
            v_ref[seq, rows, :] = v.astype(v_ref.dtype)


def _in_proj(x, mod, w_in, ch_cos, ch_sin, *, layer, mod_row, per_seq_mod, kv_transposed):
    B, L, _ = x.shape
    nb, tr = _tile_split(B, L, IN_ROW_TILE)
    assert nb == 1 or not per_seq_mod
    row = lambda w, dt: jax.ShapeDtypeStruct((B, L, w), dt)
    row_spec = lambda w: pl.BlockSpec((nb, tr, w), lambda g, i: (g, i, 0))
    four = jax.ShapeDtypeStruct((L, B * D_B), _BF16)
    four_spec = pl.BlockSpec((tr, nb * D_B), lambda g, i: (i, g))
    mod_idx = (lambda l, r: (lambda g, i: (l, r + g, 0, 0))) if per_seq_mod else None
    if kv_transposed:
        kv = jax.ShapeDtypeStruct((B, D_C, L), _F32)
        kv_spec = pl.BlockSpec((nb, D_C, tr), lambda g, i: (g, 0, i))
    else:
        kv = row(D_C, _BF16)
        kv_spec = row_spec(D_C)
    return pl.pallas_call(
        functools.partial(_in_kernel, kv_transposed=kv_transposed),
        out_shape=(row(D_A, _F32), row(D_A, _BF16), four, four, row(D_C, _BF16), kv, kv),
        grid=(B // nb, L // tr),
        in_specs=[
            row_spec(D_MODEL),
            _mod_spec(layer, mod_row, mod_idx),
            _layer_spec((D_MODEL, P_IN), layer),
            _const_spec((D_B, D_B)),
            _const_spec((D_B, D_B)),
        ],
        out_specs=(row_spec(D_A), row_spec(D_A), four_spec, four_spec, row_spec(D_C), kv_spec, kv_spec),
        compiler_params=_params(("parallel", "parallel")),
        name="in_proj",
    )(x, mod, w_in, ch_cos, ch_sin)


def _dft_kernel(c_ref, s_ref, zc_ref, zs_ref, rev_ref, top_ref, bot_ref):
    p = _dot(c_ref[0], zc_ref[...])
    q = _dot(s_ref[0], zs_ref[...])
    top_ref[...] = (p - q)[:top_ref.shape[0]].astype(top_ref.dtype)
    bot_ref[...] = _dot(rev_ref[...], (p + q).astype(_BF16)).astype(bot_ref.dtype)


def _dft_tile_rows(L):
    return min(L // 2, DFT_ROW_TILE)


def _pos_dft(pos_cos, pos_sin, rev, zc, zs):
    L, N = zc.shape
    ti = _dft_tile_rows(L)
    n_t = (L // 2) // ti
    tj = DFT_COL_TILE
    half = jax.ShapeDtypeStruct((L // 2, N), _BF16)
    tab_spec = pl.BlockSpec((1, ti + HALO, L), lambda t, j: (t, 0, 0))
    return pl.pallas_call(
        _dft_kernel,
        out_shape=(half, half),
        grid=(n_t, N // tj),
        in_specs=[
            tab_spec, tab_spec,
            pl.BlockSpec((L, tj), lambda t, j: (0, j)),
            pl.BlockSpec((L, tj), lambda t, j: (0, j)),
            pl.BlockSpec((ti, ti + HALO), lambda t, j: (0, 0)),
        ],
        out_specs=(pl.BlockSpec((ti, tj), lambda t, j: (t, j)),
                   pl.BlockSpec((ti, tj), lambda t, j: (n_t - 1 - t, j))),
        compiler_params=_params(("parallel", "parallel")),
        name="pos_dft",
    )(pos_cos, pos_sin, zc, zs, rev)


def _head_lane_mask(rows, head):
    lane = lax.broadcasted_iota(jnp.int32, (rows, HEAD_PAIR), 1)
    return (lane >= head * C_HEAD_DIM) & (lane < (head + 1) * C_HEAD_DIM)


def _ctx_attn_kernel(q_ref, kt_ref, vt_ref, o_ref):
    rows = q_ref.shape[1]
    for pair in range(N_HEAD_PAIRS):
        lanes = slice(pair * HEAD_PAIR, (pair + 1) * HEAD_PAIR)
        q = q_ref[0, :, lanes]
        kt = kt_ref[0, lanes, :].astype(_BF16)
        vt = vt_ref[0, lanes, :].astype(_BF16)
        out = jnp.zeros((rows, HEAD_PAIR), _F32)
        for head in range(2):
            sel = _head_lane_mask(rows, head)
            s = _dot(jnp.where(sel, q, jnp.zeros_like(q)), kt)
            p = jnp.exp2(s - jnp.max(s, axis=-1, keepdims=True))
            o = _dot_nt(p.astype(_BF16), vt) / jnp.sum(p, axis=-1, keepdims=True)
            out = jnp.where(sel, o, out)
        o_ref[0, :, lanes] = out.astype(o_ref.dtype)


def _ctx_attention(q, kt, vt):
    B, L, _ = q.shape
    spec = pl.BlockSpec((1, L, D_C), lambda b: (b, 0, 0))
    spec_t = pl.BlockSpec((1, D_C, L), lambda b: (b, 0, 0))
    return pl.pallas_call(
        _ctx_attn_kernel,
        out_shape=jax.ShapeDtypeStruct((B, L, D_C), _BF16),
        grid=(B,),
        in_specs=[spec, spec_t, spec_t],
        out_specs=spec,
        compiler_params=_params(("parallel",)),
        name="ctx_attention",
    )(q, kt, vt)


def _na_kernel(q_ref, k_ref, v_ref, kct_ref, vct_ref, bias_ref, o_ref, *, n_rows):
    kct = kct_ref[...].astype(_BF16)
    vct = vct_ref[...].astype(_BF16)
    win_keys = NA_ROWS * GRID_W
    pair_rows = 2 * GRID_W
    n_q = n_rows * GRID_W

    windows = []
    for r in range(n_rows):
        rs = min(max(r - NA_ROWS // 2, 0), n_rows - NA_ROWS)
        windows.append((slice(rs * GRID_W, rs * GRID_W + win_keys), rs - r + NA_ROWS - 1))

    q = q_ref[0]
    sel0 = _head_lane_mask(n_q, 0)
    q_heads = (jnp.where(sel0, q, jnp.zeros_like(q)), jnp.where(sel0, jnp.zeros_like(q), q))
    lhs = jnp.concatenate([qh[r * GRID_W:(r + 1) * GRID_W]
                           for r in range(n_rows) for qh in q_heads], axis=0)
    s_ctx = _dot(lhs, kct)
    s_loc = []
    for r, (win, base) in enumerate(windows):
        bias = jnp.concatenate(
            [jnp.concatenate([bias_ref[head, base + 2 * t] for t in range(NA_ROWS // 2)], axis=1)
             for head in range(2)], axis=0)
        s_loc.append(_dot_nt(lhs[r * pair_rows:(r + 1) * pair_rows], k_ref[0, win, :]) + bias)
    s_loc = jnp.concatenate(s_loc, axis=0)
    m = jnp.maximum(jnp.max(s_loc, axis=-1, keepdims=True), jnp.max(s_ctx, axis=-1, keepdims=True))
    p_loc = jnp.exp2(s_loc - m)
    p_ctx = jnp.exp2(s_ctx - m)
    denom = jnp.sum(p_loc, axis=-1, keepdims=True) + jnp.sum(p_ctx, axis=-1, keepdims=True)
    p_loc = p_loc.astype(_BF16)
    o_loc = jnp.concatenate(
        [_dot(p_loc[r * pair_rows:(r + 1) * pair_rows], v_ref[0, win, :])
         for r, (win, _) in enumerate(windows)], axis=0)
    o = (o_loc + _dot_nt(p_ctx.astype(_BF16), vct)) / denom
    row_sel0 = _head_lane_mask(GRID_W, 0)
    o_ref[0] = jnp.concatenate(
        [jnp.where(row_sel0, o[r * pair_rows:r * pair_rows + GRID_W],
                   o[r * pair_rows + GRID_W:(r + 1) * pair_rows])
         for r in range(n_rows)], axis=0).astype(o_ref.dtype)


def _na_bias_kernel(v_ref, o_ref):
    n_ri = 2 * NA_ROWS - 1
    qcol = lax.broadcasted_iota(jnp.int32, (GRID_W, LANES), 0)
    lane = lax.broadcasted_iota(jnp.int32, (GRID_W, LANES), 1)
    kcol = lane & (GRID_W - 1)
    start = jnp.clip(qcol - NA_KW // 2, 0, GRID_W - NA_KW)
    in_win = (kcol >= start) & (kcol < start + NA_KW)
    for head in range(v_ref.shape[0]):
        rows = [jnp.broadcast_to(v_ref[head, ri], (GRID_W, LANES)) for ri in range(n_ri)]
        for ri in range(n_ri - 1):
            left = pltpu.roll(rows[ri], GRID_W + 1, 1, stride=1, stride_axis=0)
            right = pltpu.roll(rows[ri + 1], 1, 1, stride=1, stride_axis=0)
            o_ref[head, ri] = jnp.where(in_win, jnp.where(lane < GRID_W, left, right) * LOG2_E, -jnp.inf)


def _na_bias_table(rpb):
    n_ri = 2 * NA_ROWS - 1
    lo = GRID_W - NA_KW
    v = jnp.pad(rpb, ((0, 0), (0, 0), (0, 0), (lo, LANES - lo - (2 * NA_KW - 1))))
    v = v.reshape(DEPTH * C_HEADS, n_ri, 1, LANES)
    return pl.pallas_call(
        _na_bias_kernel,
        out_shape=jax.ShapeDtypeStruct((DEPTH * C_HEADS, n_ri - 1, GRID_W, LANES), _F32),
        grid=(DEPTH,),
        in_specs=[pl.BlockSpec((C_HEADS, n_ri, 1, LANES), lambda i: (i, 0, 0, 0))],
        out_specs=pl.BlockSpec((C_HEADS, n_ri - 1, GRID_W, LANES), lambda i: (i, 0, 0, 0)),
        compiler_params=_params(("parallel",)),
        name="na_bias",
    )(v)


def _na_attention(q, k, v, kct, vct, bias, *, layer):
    B, L, _ = q.shape
    Lc = kct.shape[3]
    n_rows = L // GRID_W
    assert n_rows >= NA_ROWS and 2 * GRID_W == LANES
    lat = pl.BlockSpec((1, L, HEAD_PAIR), lambda p, b: (b, 0, p))
    ctx = pl.BlockSpec((None, None, HEAD_PAIR, Lc), lambda p, b: (b, layer, p, 0))
    return pl.pallas_call(
        functools.partial(_na_kernel, n_rows=n_rows),
        out_shape=jax.ShapeDtypeStruct((B, L, D_C), _BF16),
        grid=(N_HEAD_PAIRS, B),
        in_specs=[lat, lat, lat, ctx, ctx,
                  pl.BlockSpec((2, 2 * NA_ROWS - 2, GRID_W, LANES),
                               lambda p, b: (layer * N_HEAD_PAIRS + p, 0, 0, 0))],
        out_specs=lat,
        compiler_params=_params(("parallel", "parallel")),
        name="na_attention",
    )(q, k, v, kct, vct, bias)


def _out_kernel(x_ref, u_ref, vn_ref, ft_ref, fb_ref, a_ref, mod_ref, ws_ref, bs_ref, fw_ref, wo_ref,
                g_ref, b_ref, x1_ref, *, tiles_per_half):
    nb, tr, _ = x_ref.shape
    lane = lax.broadcasted_iota(jnp.int32, (CHUNK, D_A), 1)
    m = mod_ref[...]
    if tiles_per_half == 0:
        four = jnp.concatenate([half[:, s * D_B:(s + 1) * D_B]
                                for s in range(nb) for half in (ft_ref, fb_ref)], axis=0)
    else:
        four = jnp.where(pl.program_id(1) < tiles_per_half, ft_ref[...], fb_ref[...])

    def rows_of(ref, start, n):
        return ref[start // tr, start % tr:start % tr + n, :]

    for s in range(nb * tr // OUT_SUB_ROWS):
        sub = slice(s * OUT_SUB_ROWS, (s + 1) * OUT_SUB_ROWS)
        ya = []
        for n in range(OUT_SUB_ROWS // CHUNK):
            start = s * OUT_SUB_ROWS + n * CHUNK
            r = _dot(ws_ref[...], rows_of(vn_ref, start, CHUNK))
            mixed = r[:CHUNK]
            for hd in range(1, A_HEADS):
                mixed = jnp.where(lane >= hd * A_HEAD_DIM, r[hd * CHUNK:(hd + 1) * CHUNK], mixed)
            ya.append(rows_of(u_ref, start, CHUNK) * (mixed + bs_ref[...]))
        ya = jnp.concatenate(ya, axis=0).astype(_BF16)
        yb = _dot(four[sub], fw_ref[...]).astype(_BF16)
        y = _dot(jnp.concatenate([ya, yb, rows_of(a_ref, s * OUT_SUB_ROWS, OUT_SUB_ROWS)], axis=1), wo_ref[...])
        x1 = (_norm(ALPHA * rows_of(x_ref, s * OUT_SUB_ROWS, OUT_SUB_ROWS) + m[2:3] * y) * g_ref[...]
              + b_ref[...])
        off = (s * OUT_SUB_ROWS) % tr
        x1_ref[(s * OUT_SUB_ROWS) // tr, off:off + OUT_SUB_ROWS, :] = x1


def _out_proj(x, u, vn, four_top, four_bot, attn, mod, ws, bs, fw, w_out, ln_g, ln_b, *,
              layer, mod_row, per_seq_mod):
    B, L, _ = x.shape
    nb, tr = _tile_split(B, L, OUT_ROW_TILE)
    assert (nb == 1 or not per_seq_mod) and tr % OUT_SUB_ROWS == 0
    row_spec = lambda w: pl.BlockSpec((nb, tr, w), lambda g, i: (g, i, 0))
    mod_idx = (lambda l, r: (lambda g, i: (l, r + g, 0, 0))) if per_seq_mod else None
    nh = (L // 2) // tr
    assert tr == L or (L // 2) % tr == 0
    if nh == 0:
        top_spec = bot_spec = pl.BlockSpec((L // 2, nb * D_B), lambda g, i: (0, g))
    else:
        top_spec = pl.BlockSpec((tr, D_B), lambda g, i: (jnp.minimum(i, nh - 1), g))
        bot_spec = pl.BlockSpec((tr, D_B), lambda g, i: (jnp.maximum(i - nh, 0), g))
    return pl.pallas_call(
        functools.partial(_out_kernel, tiles_per_half=nh),
        out_shape=jax.ShapeDtypeStruct((B, L, D_MODEL), _F32),
        grid=(B // nb, L // tr),
        in_specs=[
            row_spec(D_MODEL), row_spec(D_A), row_spec(D_A), top_spec, bot_spec, row_spec(D_C),
            _mod_spec(layer, mod_row, mod_idx),
            _layer_spec((A_HEADS * CHUNK, CHUNK), layer),
            _layer_spec((CHUNK, D_A), layer),
            _layer_spec((D_B, D_B), layer),
            _layer_spec((D_MODEL, D_MODEL), layer),
            _layer_spec((1, D_MODEL), layer),
            _layer_spec((1, D_MODEL), layer),
        ],
        out_specs=row_spec(D_MODEL),
        compiler_params=_params(("parallel", "parallel")),
        name="out_proj",
    )(x, u, vn, four_top, four_bot, attn, mod, ws, bs, fw, w_out, ln_g, ln_b)


def _ffn_kernel(x_ref, xp_ref, xn_ref, mod_ref, wgu_ref, cw_ref, cb_ref, wd_ref,
                g_ref, b_ref, o_ref, hcat_ref, act_ref, *, seq_len):
    tm = x_ref.shape[0]
    ext = tm + 2 * HALO
    inside_one_seq = seq_len % tm == 0
    m = mod_ref[...]

    def pre_norm(x):
        return (_norm(x) * (1.0 + m[4:5]) + m[3:4]).astype(_BF16)

    h_prev = pre_norm(xp_ref[...])
    h_next = pre_norm(xn_ref[...])
    if inside_one_seq:
        t = pl.program_id(0) % (seq_len // tm)
        hcat_ref[:HALO, :] = jnp.where(t > 0, h_prev, jnp.zeros_like(h_prev))
        hcat_ref[HALO + tm:, :] = jnp.where(t < seq_len // tm - 1, h_next, jnp.zeros_like(h_next))
    else:
        assert seq_len & (seq_len - 1) == 0
        hcat_ref[:HALO, :] = h_prev
        hcat_ref[HALO + tm:, :] = h_next
        pos = lax.broadcasted_iota(jnp.int32, (tm, 1), 0) & (seq_len - 1)
        has_prev = pos != 0
        has_next = pos != seq_len - 1
    for s in range(tm // FFN_SUB_ROWS):
        sub = slice(s * FFN_SUB_ROWS, (s + 1) * FFN_SUB_ROWS)
        hcat_ref[HALO + s * FFN_SUB_ROWS:HALO + (s + 1) * FFN_SUB_ROWS, :] = pre_norm(x_ref[sub, :])
    for j in range(D_FF // FF_CHUNK):
        cols = slice(j * FF_CHUNK, (j + 1) * FF_CHUNK)
        up_cols = slice(D_FF + j * FF_CHUNK, D_FF + (j + 1) * FF_CHUNK)
        g = _dot(hcat_ref[...], wgu_ref[:, cols])
        g_prev = pltpu.roll(g, 1, 0)[HALO:HALO + tm]
        g_next = pltpu.roll(g, ext - 1, 0)[HALO:HALO + tm]
        if not inside_one_seq:
            g_prev = jnp.where(has_prev, g_prev, 0.0)
            g_next = jnp.where(has_next, g_next, 0.0)
        gc = (cb_ref[:, cols] + g_prev * cw_ref[0:1, cols] + g[HALO:HALO + tm] * cw_ref[1:2, cols]
              + g_next * cw_ref[2:3, cols])
        up = _dot(hcat_ref[HALO:HALO + tm, :], wgu_ref[:, up_cols])
        act_ref[:, cols] = ((gc * jax.nn.sigmoid(gc)) * up).astype(_BF16)
    for s in range(tm // FFN_SUB_ROWS):
        sub = slice(s * FFN_SUB_ROWS, (s + 1) * FFN_SUB_ROWS)
        y = _dot(act_ref[sub, :], wd_ref[...])
        o_ref[sub, :] = _norm(ALPHA * x_ref[sub, :] + m[5:6] * y) * g_ref[...] + b_ref[...]


def _conv_ffn(x1, mod, w_gu, conv_w, conv_b, w_down, ln_g, ln_b, *, layer, mod_row, per_seq_mod):
    B, L, _ = x1.shape
    n_tok = B * L
    tm = FFN_ROW_TILE
    assert n_tok % tm == 0 and (L % tm == 0 or (tm % L == 0 and not per_seq_mod))
    x1 = x1.reshape(n_tok, D_MODEL)
    row_spec = pl.BlockSpec((tm, D_MODEL), lambda t: (t, 0))
    per = tm // HALO
    last = n_tok // HALO - 1
    prev_spec = pl.BlockSpec((HALO, D_MODEL), lambda t: (jnp.maximum(t * per - 1, 0), 0))
    next_spec = pl.BlockSpec((HALO, D_MODEL), lambda t: (jnp.minimum((t + 1) * per, last), 0))
    mod_idx = (lambda l, r: (lambda t: (l, r + (t * tm) // L, 0, 0))) if per_seq_mod else None
    out = pl.pallas_call(
        functools.partial(_ffn_kernel, seq_len=L),
        out_shape=jax.ShapeDtypeStruct((n_tok, D_MODEL), _F32),
        grid=(n_tok // tm,),
        in_specs=[
            row_spec, prev_spec, next_spec,
            _mod_spec(layer, mod_row, mod_idx),
            _layer_spec((D_MODEL, 2 * D_FF), layer),
            _layer_spec((3, D_FF), layer),
            _layer_spec((1, D_FF), layer),
            _layer_spec((D_FF, D_MODEL), layer),
            _layer_spec((1, D_MODEL), layer),
            _layer_spec((1, D_MODEL), layer),
        ],
        out_specs=row_spec,
        scratch_shapes=[pltpu.VMEM((tm + 2 * HALO, D_MODEL), _BF16), pltpu.VMEM((tm, D_FF), _BF16)],
        compiler_params=_params(("parallel",)),
        name="conv_ffn",
    )(x1, x1, x1, mod, w_gu, conv_w, conv_b, w_down, ln_g, ln_b)
    return out.reshape(B, L, D_MODEL)


def _dft_tables(n):
    idx = np.arange(n)
    ang = 2.0 * np.pi * ((idx[:, None] * idx[None, :]) % n) / n
    return np.cos(ang) / np.sqrt(n), np.sin(ang) / np.sqrt(n)


def _pos_dft_tables(n):
    cos, sin = _dft_tables(n)
    ti = _dft_tile_rows(n)
    tiles = lambda tab: np.stack([tab[t * ti:t * ti + ti + HALO] for t in range((n // 2) // ti)])
    rev = np.zeros((ti, ti + HALO))
    rev[np.arange(ti), ti - np.arange(ti)] = 1.0
    return tiles(cos), tiles(sin), rev


def _block_diag(blocks):
    n = len(blocks)
    d = blocks[0].shape[0]
    out = np.zeros((n * d, n * d), blocks[0].dtype)
    for g, blk in enumerate(blocks):
        out[g * d:(g + 1) * d, g * d:(g + 1) * d] = blk
    return out


def kernel(x_prompt, x_sample, cache_k, cache_v, c, c_ctx, w_ada, b_ada, w_in, a_ws, a_bs, f_w, rpb,
           w_out, ln1_g, ln1_b, w_gu, conv_w, conv_b, w_down, ln2_g, ln2_b):
    n_ctx_b, ctx_len, _ = x_prompt.shape
    n_lat_b, lat_len, _ = x_sample.shape
    past_len = cache_k.shape[2]

    cs = jnp.zeros((MOD_ROWS, D_MODEL), _F32).at[0].set(c_ctx).at[1:1 + n_lat_b].set(c)
    mod = _modulation(cs, w_ada, b_ada).reshape(DEPTH, MOD_ROWS, 6, D_MODEL)

    cc, sc = _dft_tables(B_GROUP_DIM)
    ch_cos = jnp.asarray(_block_diag([cc] * B_GROUPS), _F32).astype(_BF16)
    ch_sin = jnp.asarray(_block_diag([sc] * B_GROUPS), _F32).astype(_BF16)
    pos_tabs = {n: tuple(jnp.asarray(t, _F32).astype(_BF16) for t in _pos_dft_tables(n))
                for n in (ctx_len, lat_len)}

    w_in_b = w_in.astype(_BF16)
    w_out_b = w_out.astype(_BF16)
    w_gu_b = w_gu.astype(_BF16)
    w_down_b = w_down.astype(_BF16)
    ws = a_ws.reshape(DEPTH, A_HEADS * CHUNK, CHUNK).astype(_BF16)
    bs = jnp.repeat(a_bs.transpose(0, 2, 1), A_HEAD_DIM, axis=2)
    group_eye = jnp.eye(B_GROUPS, dtype=_F32)[None, :, None, :, None]
    fw = (f_w[:, :, :, None, :] * group_eye).reshape(DEPTH, D_B, D_B).astype(_BF16)
    vec = lambda p: p.reshape(DEPTH, 1, p.shape[-1])
    g1, b1, g2, b2, cb = vec(ln1_g), vec(ln1_b), vec(ln2_g), vec(ln2_b), vec(conv_b)
    bias = _na_bias_table(rpb)
    heads_first = (0, 1, 3, 4, 2)
    kct = cache_k.transpose(heads_first).reshape(n_lat_b, DEPTH, D_C, past_len)
    vct = cache_v.transpose(heads_first).reshape(n_lat_b, DEPTH, D_C, past_len)

    xp, xs = x_prompt, x_sample
    new_k, new_v = [], []
    for l in range(DEPTH):
        def layer(x, attend, kv_transposed, **where):
            u, vn, zc, zs, q, k, v = _in_proj(x, mod, w_in_b, ch_cos, ch_sin, kv_transposed=kv_transposed,
                                              **where)
            four_top, four_bot = _pos_dft(*pos_tabs[x.shape[1]], zc, zs)
            attn = attend(q, k, v)
            x1 = _out_proj(x, u, vn, four_top, four_bot, attn, mod, ws, bs, fw, w_out_b, g1, b1, **where)
            x2 = _conv_ffn(x1, mod, w_gu_b, conv_w, cb, w_down_b, g2, b2, **where)
            return x2, k, v

        xp, k_l, v_l = layer(xp, _ctx_attention, True, layer=l, mod_row=0, per_seq_mod=False)
        new_k.append(k_l)
        new_v.append(v_l)
        xs, _, _ = layer(xs, functools.partial(_na_attention, kct=kct, vct=vct, bias=bias, layer=l), False,
                         layer=l, mod_row=1, per_seq_mod=True)

    def kv_out(per_layer):
        t = jnp.stack(per_layer, axis=1).reshape(n_ctx_b, DEPTH, C_HEADS, C_HEAD_DIM, ctx_len)
        return t.transpose(0, 1, 4, 2, 3)

    return (xp, xs, kv_out(new_k), kv_out(new_v))
```

```python
import functools

import numpy as np
import jax
import jax.numpy as jnp
from jax import lax
from jax.experimental import pallas as pl
from jax.experimental.pallas import tpu as pltpu

D_MODEL = 1024
DEPTH = 2
GRID_W = 64
D_A = 256
A_HEADS = 4
A_HEAD_DIM = 64
CHUNK = 128
D_B = 256
B_GROUPS = 4
B_GROUP_DIM = 64
D_C = 512
C_HEAD_DIM = 64
C_HEADS = 8
P_IN = 2 * D_A + D_B + 3 * D_C
NA_ROWS = 8
NA_KW = 16
D_FF = 2816
ALPHA = (2 * DEPTH) ** 0.25
LN_EPS = 1e-6
LOG2_E = 1.4426950408889634
Q_SCALE = C_HEAD_DIM ** -0.5 * LOG2_E

LANES = 128
V7X_VMEM_BYTES = 64 * 1024 * 1024
VMEM_LIMIT = V7X_VMEM_BYTES * 7 // 8
HEAD_PAIR = 2 * C_HEAD_DIM
N_HEAD_PAIRS = C_HEADS // 2
MOD_ROWS = 16
MOD_COL_TILE = 3072
CTX_ATTN_SEQS = 4
IN_ROW_TILE = 1024
IN_SUB_ROWS = 256
OUT_ROW_TILE = 1024
OUT_SUB_ROWS = 256
FFN_ROW_TILE = 1024
FFN_SUB_ROWS = 256
FF_CHUNK = 256
HALO = 16
DFT_ROW_TILE = 512
DFT_COL_TILE = 1024

_F32 = jnp.float32
_BF16 = jnp.bfloat16


def _dot(a, b):
    return jnp.dot(a, b, preferred_element_type=_F32)


def _dot_nt(a, b):
    return lax.dot_general(a, b, (((1,), (1,)), ((), ())), preferred_element_type=_F32)


def _norm(x):
    mu = jnp.mean(x, axis=-1, keepdims=True)
    xc = x - mu
    var = jnp.mean(xc * xc, axis=-1, keepdims=True)
    return xc * lax.rsqrt(var + LN_EPS)


def _params(sem):
    return pltpu.CompilerParams(dimension_semantics=sem, vmem_limit_bytes=VMEM_LIMIT)


def _const_spec(shape):
    nd = len(shape)
    return pl.BlockSpec(shape, lambda *_: (0,) * nd, pipeline_mode=pl.Buffered(1))


def _layer_spec(shape, layer):
    nd = len(shape)
    return pl.BlockSpec((None,) + shape, lambda *_: (layer,) + (0,) * nd, pipeline_mode=pl.Buffered(1))


def _mod_spec(layer, first_row, rows_per_seq):
    if rows_per_seq is None:
        return pl.BlockSpec((None, None, 6, D_MODEL), lambda *_: (layer, first_row, 0, 0))
    return pl.BlockSpec((None, None, 6, D_MODEL), rows_per_seq(layer, first_row))


def _mod_kernel(c_ref, w_ref, b_ref, o_ref):
    c = c_ref[...]
    s = c * jax.nn.sigmoid(c)
    o_ref[0] = _dot(s.astype(_BF16), w_ref[0].astype(_BF16)) + b_ref[0]


def _modulation(cs, w_ada, b_ada):
    n = cs.shape[0]
    tn = MOD_COL_TILE
    return pl.pallas_call(
        _mod_kernel,
        out_shape=jax.ShapeDtypeStruct((DEPTH, n, 6 * D_MODEL), _F32),
        grid=(DEPTH, 6 * D_MODEL // tn),
        in_specs=[
            pl.BlockSpec((n, D_MODEL), lambda l, j: (0, 0)),
            pl.BlockSpec((1, D_MODEL, tn), lambda l, j: (l, 0, j)),
            pl.BlockSpec((1, 1, tn), lambda l, j: (l, 0, j)),
        ],
        out_specs=pl.BlockSpec((1, n, tn), lambda l, j: (l, 0, j)),
        compiler_params=_params(("parallel", "parallel")),
        name="adaln_mod",
    )(cs, w_ada, b_ada.reshape(DEPTH, 1, 6 * D_MODEL))


def _tile_split(B, L, tile_rows):
    if L >= tile_rows:
        assert L % tile_rows == 0
        return 1, tile_rows
    nb = min(tile_rows // L, B)
    assert B % nb == 0
    return nb, L


def _in_kernel(x_ref, mod_ref, w_ref, cc_ref, cs_ref,
               u_ref, vn_ref, zc_ref, zs_ref, q_ref, k_ref, v_ref, *, kv_transposed):
    nb, tr, _ = x_ref.shape
    m = mod_ref[...]
    o1 = 2 * D_A
    o2 = o1 + D_B
    for t in range(nb * tr // IN_SUB_ROWS):
        seq = (t * IN_SUB_ROWS) // tr
        rows = slice((t * IN_SUB_ROWS) % tr, (t * IN_SUB_ROWS) % tr + IN_SUB_ROWS)
        h = _norm(x_ref[seq, rows, :]) * (1.0 + m[1:2]) + m[0:1]
        z = _dot(h.astype(_BF16), w_ref[...])
        za = jax.nn.gelu(z[:, :o1])
        u_ref[seq, rows, :] = za[:, :D_A]
        vn_ref[seq, rows, :] = _norm(za[:, D_A:]).astype(_BF16)
        zb = z[:, o1:o2].astype(_BF16)
        zc_ref[rows, seq * D_B:(seq + 1) * D_B] = _dot(zb, cc_ref[...]).astype(_BF16)
        zs_ref[rows, seq * D_B:(seq + 1) * D_B] = _dot(zb, cs_ref[...]).astype(_BF16)
        q_ref[seq, rows, :] = (z[:, o2:o2 + D_C] * Q_SCALE).astype(q_ref.dtype)
        k = z[:, o2 + D_C:o2 + 2 * D_C]
        v = z[:, o2 + 2 * D_C:o2 + 3 * D_C]
        if kv_transposed:
            k_ref[seq, :, rows] = k.T.astype(k_ref.dtype)
            v_ref[seq, :, rows] = v.T.astype(v_ref.dtype)
        else:
            k_ref[seq, rows, :] = k.astype(k_ref.dtype)
            v_ref[seq, rows, :] = v.astype(v_ref.dtype)


def _in_proj(x, mod, w_in, ch_cos, ch_sin, *, layer, mod_row, per_seq_mod, kv_transposed):
    B, L, _ = x.shape
    nb, tr = _tile_split(B, L, IN_ROW_TILE)
    assert nb == 1 or not per_seq_mod
    row = lambda w, dt: jax.ShapeDtypeStruct((B, L, w), dt)
    row_spec = lambda w: pl.BlockSpec((nb, tr, w), lambda g, i: (g, i, 0))
    four = jax.ShapeDtypeStruct((L, B * D_B), _BF16)
    four_spec = pl.BlockSpec((tr, nb * D_B), lambda g, i: (i, g))
    mod_idx = (lambda l, r: (lambda g, i: (l, r + g, 0, 0))) if per_seq_mod else None
    if kv_transposed:
        kv = jax.ShapeDtypeStruct((B, D_C, L), _F32)
        kv_spec = pl.BlockSpec((nb, D_C, tr), lambda g, i: (g, 0, i))
    else:
        kv = row(D_C, _BF16)
        kv_spec = row_spec(D_C)
    return pl.pallas_call(
        functools.partial(_in_kernel, kv_transposed=kv_transposed),
        out_shape=(row(D_A, _F32), row(D_A, _BF16), four, four, row(D_C, _BF16), kv, kv),
        grid=(B // nb, L // tr),
        in_specs=[
            row_spec(D_MODEL),
            _mod_spec(layer, mod_row, mod_idx),
            _layer_spec((D_MODEL, P_IN), layer),
            _const_spec((D_B, D_B)),
            _const_spec((D_B, D_B)),
        ],
        out_specs=(row_spec(D_A), row_spec(D_A), four_spec, four_spec, row_spec(D_C), kv_spec, kv_spec),
        compiler_params=_params(("parallel", "parallel")),
        name="in_proj",
    )(x, mod, w_in, ch_cos, ch_sin)


def _dft_kernel(c_ref, s_ref, zc_ref, zs_ref, rev_ref, top_ref, bot_ref):
    p = _dot(c_ref[0], zc_ref[...])
    q = _dot(s_ref[0], zs_ref[...])
    top_ref[...] = (p - q)[:top_ref.shape[0]].astype(top_ref.dtype)
    bot_ref[...] = _dot(rev_ref[...], (p + q).astype(_BF16)).astype(bot_ref.dtype)


def _dft_tile_rows(L):
    return min(L // 2, DFT_ROW_TILE)


def _pos_dft(pos_cos, pos_sin, rev, zc, zs):
    L, N = zc.shape
    ti = _dft_tile_rows(L)
    n_t = (L // 2) // ti
    tj = DFT_COL_TILE
    half = jax.ShapeDtypeStruct((L // 2, N), _BF16)
    tab_spec = pl.BlockSpec((1, ti + HALO, L), lambda t, j: (t, 0, 0))
    return pl.pallas_call(
        _dft_kernel,
        out_shape=(half, half),
        grid=(n_t, N // tj),
        in_specs=[
            tab_spec, tab_spec,
            pl.BlockSpec((L, tj), lambda t, j: (0, j)),
            pl.BlockSpec((L, tj), lambda t, j: (0, j)),
            pl.BlockSpec((ti, ti + HALO), lambda t, j: (0, 0)),
        ],
        out_specs=(pl.BlockSpec((ti, tj), lambda t, j: (t, j)),
                   pl.BlockSpec((ti, tj), lambda t, j: (n_t - 1 - t, j))),
        compiler_params=_params(("parallel", "parallel")),
        name="pos_dft",
    )(pos_cos, pos_sin, zc, zs, rev)


def _head_lane_mask(rows, head):
    lane = lax.broadcasted_iota(jnp.int32, (rows, HEAD_PAIR), 1)
    return (lane >= head * C_HEAD_DIM) & (lane < (head + 1) * C_HEAD_DIM)


def _ctx_attn_kernel(q_ref, kt_ref, vt_ref, o_ref):
    n_seq, rows, _ = q_ref.shape
    sel0 = _head_lane_mask(rows, 0)
    for seq in range(n_seq):
        for pair in range(N_HEAD_PAIRS):
            lanes = slice(pair * HEAD_PAIR, (pair + 1) * HEAD_PAIR)
            q = q_ref[seq, :, lanes]
            kt = kt_ref[seq, lanes, :].astype(_BF16)
            vt = vt_ref[seq, lanes, :].astype(_BF16)
            lhs = jnp.concatenate([jnp.where(sel0, q, jnp.zeros_like(q)),
                                   jnp.where(sel0, jnp.zeros_like(q), q)], axis=0)
            s = _dot(lhs, kt)
            p = jnp.exp2(s - jnp.max(s, axis=-1, keepdims=True))
            o = _dot_nt(p.astype(_BF16), vt) / jnp.sum(p, axis=-1, keepdims=True)
            o_ref[seq, :, lanes] = jnp.where(sel0, o[:rows], o[rows:]).astype(o_ref.dtype)


def _ctx_attention(q, kt, vt):
    B, L, _ = q.shape
    nb = min(CTX_ATTN_SEQS, B)
    assert B % nb == 0
    spec = pl.BlockSpec((nb, L, D_C), lambda b: (b, 0, 0))
    spec_t = pl.BlockSpec((nb, D_C, L), lambda b: (b, 0, 0))
    return pl.pallas_call(
        _ctx_attn_kernel,
        out_shape=jax.ShapeDtypeStruct((B, L, D_C), _BF16),
        grid=(B // nb,),
        in_specs=[spec, spec_t, spec_t],
        out_specs=spec,
        compiler_params=_params(("parallel",)),
        name="ctx_attention",
    )(q, kt, vt)


def _na_kernel(q_ref, k_ref, v_ref, kct_ref, vct_ref, bias_ref, o_ref, *, n_rows):
    kct = kct_ref[...].astype(_BF16)
    vct = vct_ref[...].astype(_BF16)
    win_keys = NA_ROWS * GRID_W
    pair_rows = 2 * GRID_W
    n_q = n_rows * GRID_W

    windows = []
    for r in range(n_rows):
        rs = min(max(r - NA_ROWS // 2, 0), n_rows - NA_ROWS)
        windows.append((slice(rs * GRID_W, rs * GRID_W + win_keys), rs - r + NA_ROWS - 1))

    q = q_ref[0]
    sel0 = _head_lane_mask(n_q, 0)
    q_heads = (jnp.where(sel0, q, jnp.zeros_like(q)), jnp.where(sel0, jnp.zeros_like(q), q))
    lhs = jnp.concatenate([qh[r * GRID_W:(r + 1) * GRID_W]
                           for r in range(n_rows) for qh in q_heads], axis=0)
    s_ctx = _dot(lhs, kct)
    s_loc = []
    for r, (win, base) in enumerate(windows):
        bias = jnp.concatenate(
            [jnp.concatenate([bias_ref[head, base + 2 * t] for t in range(NA_ROWS // 2)], axis=1)
             for head in range(2)], axis=0)
        s_loc.append(_dot_nt(lhs[r * pair_rows:(r + 1) * pair_rows], k_ref[0, win, :]) + bias)
    s_loc = jnp.concatenate(s_loc, axis=0)
    m = jnp.maximum(jnp.max(s_loc, axis=-1, keepdims=True), jnp.max(s_ctx, axis=-1, keepdims=True))
    p_loc = jnp.exp2(s_loc - m)
    p_ctx = jnp.exp2(s_ctx - m)
    denom = jnp.sum(p_loc, axis=-1, keepdims=True) + jnp.sum(p_ctx, axis=-1, keepdims=True)
    p_loc = p_loc.astype(_BF16)
    o_loc = jnp.concatenate(
        [_dot(p_loc[r * pair_rows:(r + 1) * pair_rows], v_ref[0, win, :])
         for r, (win, _) in enumerate(windows)], axis=0)
    o = (o_loc + _dot_nt(p_ctx.astype(_BF16), vct)) / denom
    row_sel0 = _head_lane_mask(GRID_W, 0)
    o_ref[0] = jnp.concatenate(
        [jnp.where(row_sel0, o[r * pair_rows:r * pair_rows + GRID_W],
                   o[r * pair_rows + GRID_W:(r + 1) * pair_rows])
         for r in range(n_rows)], axis=0).astype(o_ref.dtype)


def _na_bias_kernel(v_ref, o_ref):
    n_ri = 2 * NA_ROWS - 1
    qcol = lax.broadcasted_iota(jnp.int32, (GRID_W, LANES), 0)
    lane = lax.broadcasted_iota(jnp.int32, (GRID_W, LANES), 1)
    kcol = lane & (GRID_W - 1)
    start = jnp.clip(qcol - NA_KW // 2, 0, GRID_W - NA_KW)
    in_win = (kcol >= start) & (kcol < start + NA_KW)
    for head in range(v_ref.shape[0]):
        rows = [jnp.broadcast_to(v_ref[head, ri], (GRID_W, LANES)) for ri in range(n_ri)]
        for ri in range(n_ri - 1):
            left = pltpu.roll(rows[ri], GRID_W + 1, 1, stride=1, stride_axis=0)
            right = pltpu.roll(rows[ri + 1], 1, 1, stride=1, stride_axis=0)
            o_ref[head, ri] = jnp.where(in_win, jnp.where(lane < GRID_W, left, right) * LOG2_E, -jnp.inf)


def _na_bias_table(rpb):
    n_ri = 2 * NA_ROWS - 1
    lo = GRID_W - NA_KW
    v = jnp.pad(rpb, ((0, 0), (0, 0), (0, 0), (lo, LANES - lo - (2 * NA_KW - 1))))
    v = v.reshape(DEPTH * C_HEADS, n_ri, 1, LANES)
    return pl.pallas_call(
        _na_bias_kernel,
        out_shape=jax.ShapeDtypeStruct((DEPTH * C_HEADS, n_ri - 1, GRID_W, LANES), _F32),
        grid=(DEPTH,),
        in_specs=[pl.BlockSpec((C_HEADS, n_ri, 1, LANES), lambda i: (i, 0, 0, 0))],
        out_specs=pl.BlockSpec((C_HEADS, n_ri - 1, GRID_W, LANES), lambda i: (i, 0, 0, 0)),
        compiler_params=_params(("parallel",)),
        name="na_bias",
    )(v)


def _na_attention(q, k, v, kct, vct, bias, *, layer):
    B, L, _ = q.shape
    Lc = kct.shape[3]
    n_rows = L // GRID_W
    assert n_rows >= NA_ROWS and 2 * GRID_W == LANES
    lat = pl.BlockSpec((1, L, HEAD_PAIR), lambda p, b: (b, 0, p))
    ctx = pl.BlockSpec((None, None, HEAD_PAIR, Lc), lambda p, b: (b, layer, p, 0))
    return pl.pallas_call(
        functools.partial(_na_kernel, n_rows=n_rows),
        out_shape=jax.ShapeDtypeStruct((B, L, D_C), _BF16),
        grid=(N_HEAD_PAIRS, B),
        in_specs=[lat, lat, lat, ctx, ctx,
                  pl.BlockSpec((2, 2 * NA_ROWS - 2, GRID_W, LANES),
                               lambda p, b: (layer * N_HEAD_PAIRS + p, 0, 0, 0))],
        out_specs=lat,
        compiler_params=_params(("parallel", "parallel")),
        name="na_attention",
    )(q, k, v, kct, vct, bias)


def _out_kernel(x_ref, u_ref, vn_ref, ft_ref, fb_ref, a_ref, mod_ref, ws_ref, bs_ref, fw_ref, wo_ref,
                g_ref, b_ref, x1_ref, *, tiles_per_half):
    nb, tr, _ = x_ref.shape
    lane = lax.broadcasted_iota(jnp.int32, (CHUNK, D_A), 1)
    m = mod_ref[...]
    if tiles_per_half == 0:
        four = jnp.concatenate([half[:, s * D_B:(s + 1) * D_B]
                                for s in range(nb) for half in (ft_ref, fb_ref)], axis=0)
    else:
        four = jnp.where(pl.program_id(1) < tiles_per_half, ft_ref[...], fb_ref[...])

    def rows_of(ref, start, n):
        return ref[start // tr, start % tr:start % tr + n, :]

    for s in range(nb * tr // OUT_SUB_ROWS):
        sub = slice(s * OUT_SUB_ROWS, (s + 1) * OUT_SUB_ROWS)
        ya = []
        for n in range(OUT_SUB_ROWS // CHUNK):
            start = s * OUT_SUB_ROWS + n * CHUNK
            r = _dot(ws_ref[...], rows_of(vn_ref, start, CHUNK))
            mixed = r[:CHUNK]
            for hd in range(1, A_HEADS):
                mixed = jnp.where(lane >= hd * A_HEAD_DIM, r[hd * CHUNK:(hd + 1) * CHUNK], mixed)
            ya.append(rows_of(u_ref, start, CHUNK) * (mixed + bs_ref[...]))
        ya = jnp.concatenate(ya, axis=0).astype(_BF16)
        yb = _dot(four[sub], fw_ref[...]).astype(_BF16)
        y = _dot(jnp.concatenate([ya, yb, rows_of(a_ref, s * OUT_SUB_ROWS, OUT_SUB_ROWS)], axis=1), wo_ref[...])
        x1 = (_norm(ALPHA * rows_of(x_ref, s * OUT_SUB_ROWS, OUT_SUB_ROWS) + m[2:3] * y) * g_ref[...]
              + b_ref[...])
        off = (s * OUT_SUB_ROWS) % tr
        x1_ref[(s * OUT_SUB_ROWS) // tr, off:off + OUT_SUB_ROWS, :] = x1


def _out_proj(x, u, vn, four_top, four_bot, attn, mod, ws, bs, fw, w_out, ln_g, ln_b, *,
              layer, mod_row, per_seq_mod):
    B, L, _ = x.shape
    nb, tr = _tile_split(B, L, OUT_ROW_TILE)
    assert (nb == 1 or not per_seq_mod) and tr % OUT_SUB_ROWS == 0
    row_spec = lambda w: pl.BlockSpec((nb, tr, w), lambda g, i: (g, i, 0))
    mod_idx = (lambda l, r: (lambda g, i: (l, r + g, 0, 0))) if per_seq_mod else None
    nh = (L // 2) // tr
    assert tr == L or (L // 2) % tr == 0
    if nh == 0:
        top_spec = bot_spec = pl.BlockSpec((L // 2, nb * D_B), lambda g, i: (0, g))
    else:
        top_spec = pl.BlockSpec((tr, D_B), lambda g, i: (jnp.minimum(i, nh - 1), g))
        bot_spec = pl.BlockSpec((tr, D_B), lambda g, i: (jnp.maximum(i - nh, 0), g))
    return pl.pallas_call(
        functools.partial(_out_kernel, tiles_per_half=nh),
        out_shape=jax.ShapeDtypeStruct((B, L, D_MODEL), _F32),
        grid=(B // nb, L // tr),
        in_specs=[
            row_spec(D_MODEL), row_spec(D_A), row_spec(D_A), top_spec, bot_spec, row_spec(D_C),
            _mod_spec(layer, mod_row, mod_idx),
            _layer_spec((A_HEADS * CHUNK, CHUNK), layer),
            _layer_spec((CHUNK, D_A), layer),
            _layer_spec((D_B, D_B), layer),
            _layer_spec((D_MODEL, D_MODEL), layer),
            _layer_spec((1, D_MODEL), layer),
            _layer_spec((1, D_MODEL), layer),
        ],
        out_specs=row_spec(D_MODEL),
        compiler_params=_params(("parallel", "parallel")),
        name="out_proj",
    )(x, u, vn, four_top, four_bot, attn, mod, ws, bs, fw, w_out, ln_g, ln_b)


def _ffn_kernel(x_ref, xp_ref, xn_ref, mod_ref, wgu_ref, cw_ref, cb_ref, wd_ref,
                g_ref, b_ref, o_ref, hcat_ref, act_ref, *, seq_len):
    tm = x_ref.shape[0]
    ext = tm + 2 * HALO
    inside_one_seq = seq_len % tm == 0
    m = mod_ref[...]

    def pre_norm(x):
        return (_norm(x) * (1.0 + m[4:5]) + m[3:4]).astype(_BF16)

    h_prev = pre_norm(xp_ref[...])
    h_next = pre_norm(xn_ref[...])
    if inside_one_seq:
        t = pl.program_id(0) % (seq_len // tm)
        hcat_ref[:HALO, :] = jnp.where(t > 0, h_prev, jnp.zeros_like(h_prev))
        hcat_ref[HALO + tm:, :] = jnp.where(t < seq_len // tm - 1, h_next, jnp.zeros_like(h_next))
    else:
        assert seq_len & (seq_len - 1) == 0
        hcat_ref[:HALO, :] = h_prev
        hcat_ref[HALO + tm:, :] = h_next
        pos = lax.broadcasted_iota(jnp.int32, (tm, 1), 0) & (seq_len - 1)
        has_prev = pos != 0
        has_next = pos != seq_len - 1
    for s in range(tm // FFN_SUB_ROWS):
        sub = slice(s * FFN_SUB_ROWS, (s + 1) * FFN_SUB_ROWS)
        hcat_ref[HALO + s * FFN_SUB_ROWS:HALO + (s + 1) * FFN_SUB_ROWS, :] = pre_norm(x_ref[sub, :])
    for j in range(D_FF // FF_CHUNK):
        cols = slice(j * FF_CHUNK, (j + 1) * FF_CHUNK)
        up_cols = slice(D_FF + j * FF_CHUNK, D_FF + (j + 1) * FF_CHUNK)
        g = _dot(hcat_ref[...], wgu_ref[:, cols])
        g_prev = pltpu.roll(g, 1, 0)[HALO:HALO + tm]
        g_next = pltpu.roll(g, ext - 1, 0)[HALO:HALO + tm]
        if not inside_one_seq:
            g_prev = jnp.where(has_prev, g_prev, 0.0)
            g_next = jnp.where(has_next, g_next, 0.0)
        gc = (cb_ref[:, cols] + g_prev * cw_ref[0:1, cols] + g[HALO:HALO + tm] * cw_ref[1:2, cols]
              + g_next * cw_ref[2:3, cols])
        up = _dot(hcat_ref[HALO:HALO + tm, :], wgu_ref[:, up_cols])
        act_ref[:, cols] = ((gc * jax.nn.sigmoid(gc)) * up).astype(_BF16)
    for s in range(tm // FFN_SUB_ROWS):
        sub = slice(s * FFN_SUB_ROWS, (s + 1) * FFN_SUB_ROWS)
        y = _dot(act_ref[sub, :], wd_ref[...])
        o_ref[sub, :] = _norm(ALPHA * x_ref[sub, :] + m[5:6] * y) * g_ref[...] + b_ref[...]


def _conv_ffn(x1, mod, w_gu, conv_w, conv_b, w_down, ln_g, ln_b, *, layer, mod_row, per_seq_mod):
    B, L, _ = x1.shape
    n_tok = B * L
    tm = FFN_ROW_TILE
    assert n_tok % tm == 0 and (L % tm == 0 or (tm % L == 0 and not per_seq_mod))
    x1 = x1.reshape(n_tok, D_MODEL)
    row_spec = pl.BlockSpec((tm, D_MODEL), lambda t: (t, 0))
    per = tm // HALO
    last = n_tok // HALO - 1
    prev_spec = pl.BlockSpec((HALO, D_MODEL), lambda t: (jnp.maximum(t * per - 1, 0), 0))
    next_spec = pl.BlockSpec((HALO, D_MODEL), lambda t: (jnp.minimum((t + 1) * per, last), 0))
    mod_idx = (lambda l, r: (lambda t: (l, r + (t * tm) // L, 0, 0))) if per_seq_mod else None
    out = pl.pallas_call(
        functools.partial(_ffn_kernel, seq_len=L),
        out_shape=jax.ShapeDtypeStruct((n_tok, D_MODEL), _F32),
        grid=(n_tok // tm,),
        in_specs=[
            row_spec, prev_spec, next_spec,
            _mod_spec(layer, mod_row, mod_idx),
            _layer_spec((D_MODEL, 2 * D_FF), layer),
            _layer_spec((3, D_FF), layer),
            _layer_spec((1, D_FF), layer),
            _layer_spec((D_FF, D_MODEL), layer),
            _layer_spec((1, D_MODEL), layer),
            _layer_spec((1, D_MODEL), layer),
        ],
        out_specs=row_spec,
        scratch_shapes=[pltpu.VMEM((tm + 2 * HALO, D_MODEL), _BF16), pltpu.VMEM((tm, D_FF), _BF16)],
        compiler_params=_params(("parallel",)),
        name="conv_ffn",
    )(x1, x1, x1, mod, w_gu, conv_w, conv_b, w_down, ln_g, ln_b)
    return out.reshape(B, L, D_MODEL)


def _dft_tables(n):
    idx = np.arange(n)
    ang = 2.0 * np.pi * ((idx[:, None] * idx[None, :]) % n) / n
    return np.cos(ang) / np.sqrt(n), np.sin(ang) / np.sqrt(n)


def _pos_dft_tables(n):
    cos, sin = _dft_tables(n)
    ti = _dft_tile_rows(n)
    tiles = lambda tab: np.stack([tab[t * ti:t * ti + ti + HALO] for t in range((n // 2) // ti)])
    rev = np.zeros((ti, ti + HALO))
    rev[np.arange(ti), ti - np.arange(ti)] = 1.0
    return tiles(cos), tiles(sin), rev


def _block_diag(blocks):
    n = len(blocks)
    d = blocks[0].shape[0]
    out = np.zeros((n * d, n * d), blocks[0].dtype)
    for g, blk in enumerate(blocks):
        out[g * d:(g + 1) * d, g * d:(g + 1) * d] = blk
    return out


def kernel(x_prompt, x_sample, cache_k, cache_v, c, c_ctx, w_ada, b_ada, w_in, a_ws, a_bs, f_w, rpb,
           w_out, ln1_g, ln1_b, w_gu, conv_w, conv_b, w_down, ln2_g, ln2_b):
    n_ctx_b, ctx_len, _ = x_prompt.shape
    n_lat_b, lat_len, _ = x_sample.shape
    past_len = cache_k.shape[2]

    cs = jnp.zeros((MOD_ROWS, D_MODEL), _F32).at[0].set(c_ctx).at[1:1 + n_lat_b].set(c)
    mod = _modulation(cs, w_ada, b_ada).reshape(DEPTH, MOD_ROWS, 6, D_MODEL)

    cc, sc = _dft_tables(B_GROUP_DIM)
    ch_cos = jnp.asarray(_block_diag([cc] * B_GROUPS), _F32).astype(_BF16)
    ch_sin = jnp.asarray(_block_diag([sc] * B_GROUPS), _F32).astype(_BF16)
    pos_tabs = {n: tuple(jnp.asarray(t, _F32).astype(_BF16) for t in _pos_dft_tables(n))
                for n in (ctx_len, lat_len)}

    w_in_b = w_in.astype(_BF16)
    w_out_b = w_out.astype(_BF16)
    w_gu_b = w_gu.astype(_BF16)
    w_down_b = w_down.astype(_BF16)
    ws = a_ws.reshape(DEPTH, A_HEADS * CHUNK, CHUNK).astype(_BF16)
    bs = jnp.repeat(a_bs.transpose(0, 2, 1), A_HEAD_DIM, axis=2)
    group_eye = jnp.eye(B_GROUPS, dtype=_F32)[None, :, None, :, None]
    fw = (f_w[:, :, :, None, :] * group_eye).reshape(DEPTH, D_B, D_B).astype(_BF16)
    vec = lambda p: p.reshape(DEPTH, 1, p.shape[-1])
    g1, b1, g2, b2, cb = vec(ln1_g), vec(ln1_b), vec(ln2_g), vec(ln2_b), vec(conv_b)
    bias = _na_bias_table(rpb)
    heads_first = (0, 1, 3, 4, 2)
    kct = cache_k.transpose(heads_first).reshape(n_lat_b, DEPTH, D_C, past_len)
    vct = cache_v.transpose(heads_first).reshape(n_lat_b, DEPTH, D_C, past_len)

    xp, xs = x_prompt, x_sample
    new_k, new_v = [], []
    for l in range(DEPTH):
        def layer(x, attend, kv_transposed, **where):
            u, vn, zc, zs, q, k, v = _in_proj(x, mod, w_in_b, ch_cos, ch_sin, kv_transposed=kv_transposed,
                                              **where)
            four_top, four_bot = _pos_dft(*pos_tabs[x.shape[1]], zc, zs)
            attn = attend(q, k, v)
            x1 = _out_proj(x, u, vn, four_top, four_bot, attn, mod, ws, bs, fw, w_out_b, g1, b1, **where)
            x2 = _conv_ffn(x1, mod, w_gu_b, conv_w, cb, w_down_b, g2, b2, **where)
            return x2, k, v

        xp, k_l, v_l = layer(xp, _ctx_attention, True, layer=l, mod_row=0, per_seq_mod=False)
        new_k.append(k_l)
        new_v.append(v_l)
        xs, _, _ = layer(xs, functools.partial(_na_attention, kct=kct, vct=vct, bias=bias, layer=l), False,
                         layer=l, mod_row=1, per_seq_mod=True)

    def kv_out(per_layer):
        t = jnp.stack(per_layer, axis=1).reshape(n_ctx_b, DEPTH, C_HEADS, C_HEAD_DIM, ctx_len)
        return t.transpose(0, 1, 4, 2, 3)

    return (xp, xs, kv_out(new_k), kv_out(new_v))
```

```python
import functools

import numpy as np
import jax
import jax.numpy as jnp
from jax import lax
from jax.experimental import pallas as pl
from jax.experimental.pallas import tpu as pltpu

D_MODEL = 1024
DEPTH = 2
GRID_W = 64
D_A = 256
A_HEADS = 4
A_HEAD_DIM = 64
CHUNK = 128
D_B = 256
B_GROUPS = 4
B_GROUP_DIM = 64
D_C = 512
C_HEAD_DIM = 64
C_HEADS = 8
P_IN = 2 * D_A + D_B + 3 * D_C
NA_ROWS = 8
NA_KW = 16
D_FF = 2816
ALPHA = (2 * DEPTH) ** 0.25
LN_EPS = 1e-6
LOG2_E = 1.4426950408889634
Q_SCALE = C_HEAD_DIM ** -0.5 * LOG2_E

LANES = 128
V7X_VMEM_BYTES = 64 * 1024 * 1024
VMEM_LIMIT = V7X_VMEM_BYTES * 7 // 8
HEAD_PAIR = 2 * C_HEAD_DIM
N_HEAD_PAIRS = C_HEADS // 2
MOD_ROWS = 16
MOD_COL_TILE = 3072
CTX_ATTN_SEQS = 4
IN_ROW_TILE = 1024
IN_SUB_ROWS = 256
OUT_ROW_TILE = 1024
OUT_SUB_ROWS = 256
FFN_ROW_TILE = 1024
FFN_SUB_ROWS = 256
FF_CHUNK = 256
HALO = 16
DFT_ROW_TILE = 512
DFT_COL_TILE = 1024

_F32 = jnp.float32
_BF16 = jnp.bfloat16


def _dot(a, b):
    return jnp.dot(a, b, preferred_element_type=_F32)


def _dot_nt(a, b):
    return lax.dot_general(a, b, (((1,), (1,)), ((), ())), preferred_element_type=_F32)


def _norm(x):
    mu = jnp.mean(x, axis=-1, keepdims=True)
    xc = x - mu
    var = jnp.mean(xc * xc, axis=-1, keepdims=True)
    return xc * lax.rsqrt(var + LN_EPS)


def _params(sem):
    return pltpu.CompilerParams(dimension_semantics=sem, vmem_limit_bytes=VMEM_LIMIT)


def _const_spec(shape):
    nd = len(shape)
    return pl.BlockSpec(shape, lambda *_: (0,) * nd, pipeline_mode=pl.Buffered(1))


def _layer_spec(shape, layer):
    nd = len(shape)
    return pl.BlockSpec((None,) + shape, lambda *_: (layer,) + (0,) * nd, pipeline_mode=pl.Buffered(1))


def _mod_spec(layer, first_row, rows_per_seq):
    if rows_per_seq is None:
        return pl.BlockSpec((None, None, 6, D_MODEL), lambda *_: (layer, first_row, 0, 0))
    return pl.BlockSpec((None, None, 6, D_MODEL), rows_per_seq(layer, first_row))


def _mod_kernel(c_ref, w_ref, b_ref, o_ref):
    c = c_ref[...]
    s = c * jax.nn.sigmoid(c)
    o_ref[0] = _dot(s.astype(_BF16), w_ref[0].astype(_BF16)) + b_ref[0]


def _modulation(cs, w_ada, b_ada):
    n = cs.shape[0]
    tn = MOD_COL_TILE
    return pl.pallas_call(
        _mod_kernel,
        out_shape=jax.ShapeDtypeStruct((DEPTH, n, 6 * D_MODEL), _F32),
        grid=(DEPTH, 6 * D_MODEL // tn),
        in_specs=[
            pl.BlockSpec((n, D_MODEL), lambda l, j: (0, 0)),
            pl.BlockSpec((1, D_MODEL, tn), lambda l, j: (l, 0, j)),
            pl.BlockSpec((1, 1, tn), lambda l, j: (l, 0, j)),
        ],
        out_specs=pl.BlockSpec((1, n, tn), lambda l, j: (l, 0, j)),
        compiler_params=_params(("parallel", "parallel")),
        name="adaln_mod",
    )(cs, w_ada, b_ada.reshape(DEPTH, 1, 6 * D_MODEL))


def _tile_split(B, L, tile_rows):
    if L >= tile_rows:
        assert L % tile_rows == 0
        return 1, tile_rows
    nb = min(tile_rows // L, B)
    assert B % nb == 0
    return nb, L


def _in_kernel(x_ref, mod_ref, w_ref, cc_ref, cs_ref, *refs, kv_transposed, n_cast):
    cast_in, (u_ref, vn_ref, zc_ref, zs_ref, q_ref, k_ref, v_ref), cast_out = (
        refs[:n_cast], refs[n_cast:n_cast + 7], refs[n_cast + 7:])
    for src, dst in zip(cast_in, cast_out):
        dst[...] = src[...].astype(dst.dtype)
    nb, tr, _ = x_ref.shape
    m = mod_ref[...]
    o1 = 2 * D_A
    o2 = o1 + D_B
    for t in range(nb * tr // IN_SUB_ROWS):
        seq = (t * IN_SUB_ROWS) // tr
        rows = slice((t * IN_SUB_ROWS) % tr, (t * IN_SUB_ROWS) % tr + IN_SUB_ROWS)
        h = _norm(x_ref[seq, rows, :]) * (1.0 + m[1:2]) + m[0:1]
        z = _dot(h.astype(_BF16), w_ref[...])
        za = jax.nn.gelu(z[:, :o1])
        u_ref[seq, rows, :] = za[:, :D_A]
        vn_ref[seq, rows, :] = _norm(za[:, D_A:]).astype(_BF16)
        zb = z[:, o1:o2].astype(_BF16)
        zc_ref[rows, seq * D_B:(seq + 1) * D_B] = _dot(zb, cc_ref[...]).astype(_BF16)
        zs_ref[rows, seq * D_B:(seq + 1) * D_B] = _dot(zb, cs_ref[...]).astype(_BF16)
        q_ref[seq, rows, :] = (z[:, o2:o2 + D_C] * Q_SCALE).astype(q_ref.dtype)
        k = z[:, o2 + D_C:o2 + 2 * D_C]
        v = z[:, o2 + 2 * D_C:o2 + 3 * D_C]
        if kv_transposed:
            k_ref[seq, :, rows] = k.T.astype(k_ref.dtype)
            v_ref[seq, :, rows] = v.T.astype(v_ref.dtype)
        else:
            k_ref[seq, rows, :] = k.astype(k_ref.dtype)
            v_ref[seq, rows, :] = v.astype(v_ref.dtype)


def _in_proj(x, mod, w_in, ch_cos, ch_sin, cast=(), *, layer, mod_row, per_seq_mod, kv_transposed):
    B, L, _ = x.shape
    nb, tr = _tile_split(B, L, IN_ROW_TILE)
    assert nb == 1 or not per_seq_mod
    n_i = L // tr
    n_steps = (B // nb) * n_i
    cast_in_specs, cast_out_specs, cast_shapes = [], [], []
    for p in cast:
        _, R, C = p.shape
        assert R % (n_steps * HALO) == 0
        cast_in_specs.append(pl.BlockSpec((None, R // n_steps, C), lambda g, i: (layer, g * n_i + i, 0)))
        cast_out_specs.append(pl.BlockSpec((R // n_steps, C), lambda g, i: (g * n_i + i, 0)))
        cast_shapes.append(jax.ShapeDtypeStruct((R, C), _BF16))
    row = lambda w, dt: jax.ShapeDtypeStruct((B, L, w), dt)
    row_spec = lambda w: pl.BlockSpec((nb, tr, w), lambda g, i: (g, i, 0))
    four = jax.ShapeDtypeStruct((L, B * D_B), _BF16)
    four_spec = pl.BlockSpec((tr, nb * D_B), lambda g, i: (i, g))
    mod_idx = (lambda l, r: (lambda g, i: (l, r + g, 0, 0))) if per_seq_mod else None
    if kv_transposed:
        kv = jax.ShapeDtypeStruct((B, D_C, L), _F32)
        kv_spec = pl.BlockSpec((nb, D_C, tr), lambda g, i: (g, 0, i))
    else:
        kv = row(D_C, _BF16)
        kv_spec = row_spec(D_C)
    return pl.pallas_call(
        functools.partial(_in_kernel, kv_transposed=kv_transposed, n_cast=len(cast)),
        out_shape=(row(D_A, _F32), row(D_A, _BF16), four, four, row(D_C, _BF16), kv, kv, *cast_shapes),
        grid=(B // nb, n_i),
        in_specs=[
            row_spec(D_MODEL),
            _mod_spec(layer, mod_row, mod_idx),
            _layer_spec((D_MODEL, P_IN), layer),
            _const_spec((D_B, D_B)),
            _const_spec((D_B, D_B)),
            *cast_in_specs,
        ],
        out_specs=(row_spec(D_A), row_spec(D_A), four_spec, four_spec, row_spec(D_C), kv_spec, kv_spec,
                   *cast_out_specs),
        compiler_params=_params(("parallel", "parallel")),
        name="in_proj",
    )(x, mod, w_in, ch_cos, ch_sin, *cast)


def _dft_kernel(c_ref, s_ref, zc_ref, zs_ref, rev_ref, top_ref, bot_ref):
    p = _dot(c_ref[0], zc_ref[...])
    q = _dot(s_ref[0], zs_ref[...])
    top_ref[...] = (p - q)[:top_ref.shape[0]].astype(top_ref.dtype)
    bot_ref[...] = _dot(rev_ref[...], (p + q).astype(_BF16)).astype(bot_ref.dtype)


def _dft_tile_rows(L):
    return min(L // 2, DFT_ROW_TILE)


def _pos_dft(pos_cos, pos_sin, rev, zc, zs):
    L, N = zc.shape
    ti = _dft_tile_rows(L)
    n_t = (L // 2) // ti
    tj = DFT_COL_TILE
    half = jax.ShapeDtypeStruct((L // 2, N), _BF16)
    tab_spec = pl.BlockSpec((1, ti + HALO, L), lambda t, j: (t, 0, 0))
    return pl.pallas_call(
        _dft_kernel,
        out_shape=(half, half),
        grid=(n_t, N // tj),
        in_specs=[
            tab_spec, tab_spec,
            pl.BlockSpec((L, tj), lambda t, j: (0, j)),
            pl.BlockSpec((L, tj), lambda t, j: (0, j)),
            pl.BlockSpec((ti, ti + HALO), lambda t, j: (0, 0)),
        ],
        out_specs=(pl.BlockSpec((ti, tj), lambda t, j: (t, j)),
                   pl.BlockSpec((ti, tj), lambda t, j: (n_t - 1 - t, j))),
        compiler_params=_params(("parallel", "parallel")),
        name="pos_dft",
    )(pos_cos, pos_sin, zc, zs, rev)


def _head_lane_mask(rows, head):
    lane = lax.broadcasted_iota(jnp.int32, (rows, HEAD_PAIR), 1)
    return (lane >= head * C_HEAD_DIM) & (lane < (head + 1) * C_HEAD_DIM)


def _ctx_attn_kernel(q_ref, kt_ref, vt_ref, o_ref):
    n_seq, rows, _ = q_ref.shape
    sel0 = _head_lane_mask(rows, 0)
    for seq in range(n_seq):
        for pair in range(N_HEAD_PAIRS):
            lanes = slice(pair * HEAD_PAIR, (pair + 1) * HEAD_PAIR)
            q = q_ref[seq, :, lanes]
            kt = kt_ref[seq, lanes, :].astype(_BF16)
            vt = vt_ref[seq, lanes, :].astype(_BF16)
            lhs = jnp.concatenate([jnp.where(sel0, q, jnp.zeros_like(q)),
                                   jnp.where(sel0, jnp.zeros_like(q), q)], axis=0)
            s = _dot(lhs, kt)
            p = jnp.exp2(s - jnp.max(s, axis=-1, keepdims=True))
            o = _dot_nt(p.astype(_BF16), vt) / jnp.sum(p, axis=-1, keepdims=True)
            o_ref[seq, :, lanes] = jnp.where(sel0, o[:rows], o[rows:]).astype(o_ref.dtype)


def _ctx_attention(q, kt, vt):
    B, L, _ = q.shape
    nb = min(CTX_ATTN_SEQS, B)
    assert B % nb == 0
    spec = pl.BlockSpec((nb, L, D_C), lambda b: (b, 0, 0))
    spec_t = pl.BlockSpec((nb, D_C, L), lambda b: (b, 0, 0))
    return pl.pallas_call(
        _ctx_attn_kernel,
        out_shape=jax.ShapeDtypeStruct((B, L, D_C), _BF16),
        grid=(B // nb,),
        in_specs=[spec, spec_t, spec_t],
        out_specs=spec,
        compiler_params=_params(("parallel",)),
        name="ctx_attention",
    )(q, kt, vt)


def _na_kernel(q_ref, k_ref, v_ref, kct_ref, vct_ref, bias_ref, o_ref, *, n_rows):
    kct = kct_ref[...].astype(_BF16)
    vct = vct_ref[...].astype(_BF16)
    win_keys = NA_ROWS * GRID_W
    pair_rows = 2 * GRID_W
    n_q = n_rows * GRID_W

    windows = []
    for r in range(n_rows):
        rs = min(max(r - NA_ROWS // 2, 0), n_rows - NA_ROWS)
        windows.append((slice(rs * GRID_W, rs * GRID_W + win_keys), rs - r + NA_ROWS - 1))

    q = q_ref[0]
    sel0 = _head_lane_mask(n_q, 0)
    q_heads = (jnp.where(sel0, q, jnp.zeros_like(q)), jnp.where(sel0, jnp.zeros_like(q), q))
    lhs = jnp.concatenate([qh[r * GRID_W:(r + 1) * GRID_W]
                           for r in range(n_rows) for qh in q_heads], axis=0)
    s_ctx = _dot(lhs, kct)
    s_loc = []
    for r, (win, base) in enumerate(windows):
        bias = jnp.concatenate(
            [jnp.concatenate([bias_ref[head, base + 2 * t] for t in range(NA_ROWS // 2)], axis=1)
             for head in range(2)], axis=0)
        s_loc.append(_dot_nt(lhs[r * pair_rows:(r + 1) * pair_rows], k_ref[0, win, :]) + bias)
    s_loc = jnp.concatenate(s_loc, axis=0)
    m = jnp.maximum(jnp.max(s_loc, axis=-1, keepdims=True), jnp.max(s_ctx, axis=-1, keepdims=True))
    p_loc = jnp.exp2(s_loc - m)
    p_ctx = jnp.exp2(s_ctx - m)
    denom = jnp.sum(p_loc, axis=-1, keepdims=True) + jnp.sum(p_ctx, axis=-1, keepdims=True)
    p_loc = p_loc.astype(_BF16)
    o_loc = jnp.concatenate(
        [_dot(p_loc[r * pair_rows:(r + 1) * pair_rows], v_ref[0, win, :])
         for r, (win, _) in enumerate(windows)], axis=0)
    o = (o_loc + _dot_nt(p_ctx.astype(_BF16), vct)) / denom
    row_sel0 = _head_lane_mask(GRID_W, 0)
    o_ref[0] = jnp.concatenate(
        [jnp.where(row_sel0, o[r * pair_rows:r * pair_rows + GRID_W],
                   o[r * pair_rows + GRID_W:(r + 1) * pair_rows])
         for r in range(n_rows)], axis=0).astype(o_ref.dtype)


def _na_bias_kernel(v_ref, o_ref):
    n_ri = 2 * NA_ROWS - 1
    qcol = lax.broadcasted_iota(jnp.int32, (GRID_W, LANES), 0)
    lane = lax.broadcasted_iota(jnp.int32, (GRID_W, LANES), 1)
    kcol = lane & (GRID_W - 1)
    start = jnp.clip(qcol - NA_KW // 2, 0, GRID_W - NA_KW)
    in_win = (kcol >= start) & (kcol < start + NA_KW)
    for head in range(v_ref.shape[0]):
        rows = [jnp.broadcast_to(v_ref[head, ri], (GRID_W, LANES)) for ri in range(n_ri)]
        for ri in range(n_ri - 1):
            left = pltpu.roll(rows[ri], GRID_W + 1, 1, stride=1, stride_axis=0)
            right = pltpu.roll(rows[ri + 1], 1, 1, stride=1, stride_axis=0)
            o_ref[head, ri] = jnp.where(in_win, jnp.where(lane < GRID_W, left, right) * LOG2_E, -jnp.inf)


def _na_bias_table(rpb):
    n_ri = 2 * NA_ROWS - 1
    lo = GRID_W - NA_KW
    v = jnp.pad(rpb, ((0, 0), (0, 0), (0, 0), (lo, LANES - lo - (2 * NA_KW - 1))))
    v = v.reshape(DEPTH * C_HEADS, n_ri, 1, LANES)
    return pl.pallas_call(
        _na_bias_kernel,
        out_shape=jax.ShapeDtypeStruct((DEPTH * C_HEADS, n_ri - 1, GRID_W, LANES), _F32),
        grid=(DEPTH,),
        in_specs=[pl.BlockSpec((C_HEADS, n_ri, 1, LANES), lambda i: (i, 0, 0, 0))],
        out_specs=pl.BlockSpec((C_HEADS, n_ri - 1, GRID_W, LANES), lambda i: (i, 0, 0, 0)),
        compiler_params=_params(("parallel",)),
        name="na_bias",
    )(v)


def _na_attention(q, k, v, kct, vct, bias, *, layer):
    B, L, _ = q.shape
    Lc = kct.shape[3]
    n_rows = L // GRID_W
    assert n_rows >= NA_ROWS and 2 * GRID_W == LANES
    lat = pl.BlockSpec((1, L, HEAD_PAIR), lambda p, b: (b, 0, p))
    ctx = pl.BlockSpec((None, None, HEAD_PAIR, Lc), lambda p, b: (b, layer, p, 0))
    return pl.pallas_call(
        functools.partial(_na_kernel, n_rows=n_rows),
        out_shape=jax.ShapeDtypeStruct((B, L, D_C), _BF16),
        grid=(N_HEAD_PAIRS, B),
        in_specs=[lat, lat, lat, ctx, ctx,
                  pl.BlockSpec((2, 2 * NA_ROWS - 2, GRID_W, LANES),
                               lambda p, b: (layer * N_HEAD_PAIRS + p, 0, 0, 0))],
        out_specs=lat,
        compiler_params=_params(("parallel", "parallel")),
        name="na_attention",
    )(q, k, v, kct, vct, bias)


def _out_kernel(x_ref, u_ref, vn_ref, ft_ref, fb_ref, a_ref, mod_ref, ws_ref, bs_ref, fw_ref, wo_ref,
                g_ref, b_ref, x1_ref, *, tiles_per_half):
    nb, tr, _ = x_ref.shape
    lane = lax.broadcasted_iota(jnp.int32, (CHUNK, D_A), 1)
    m = mod_ref[...]
    if tiles_per_half == 0:
        four = jnp.concatenate([half[:, s * D_B:(s + 1) * D_B]
                                for s in range(nb) for half in (ft_ref, fb_ref)], axis=0)
    else:
        four = jnp.where(pl.program_id(1) < tiles_per_half, ft_ref[...], fb_ref[...])

    def rows_of(ref, start, n):
        return ref[start // tr, start % tr:start % tr + n, :]

    for s in range(nb * tr // OUT_SUB_ROWS):
        sub = slice(s * OUT_SUB_ROWS, (s + 1) * OUT_SUB_ROWS)
        ya = []
        for n in range(OUT_SUB_ROWS // CHUNK):
            start = s * OUT_SUB_ROWS + n * CHUNK
            r = _dot(ws_ref[...], rows_of(vn_ref, start, CHUNK))
            mixed = r[:CHUNK]
            for hd in range(1, A_HEADS):
                mixed = jnp.where(lane >= hd * A_HEAD_DIM, r[hd * CHUNK:(hd + 1) * CHUNK], mixed)
            ya.append(rows_of(u_ref, start, CHUNK) * (mixed + bs_ref[...]))
        ya = jnp.concatenate(ya, axis=0).astype(_BF16)
        yb = _dot(four[sub], fw_ref[...]).astype(_BF16)
        y = _dot(jnp.concatenate([ya, yb, rows_of(a_ref, s * OUT_SUB_ROWS, OUT_SUB_ROWS)], axis=1), wo_ref[...])
        x1 = (_norm(ALPHA * rows_of(x_ref, s * OUT_SUB_ROWS, OUT_SUB_ROWS) + m[2:3] * y) * g_ref[...]
              + b_ref[...])
        off = (s * OUT_SUB_ROWS) % tr
        x1_ref[(s * OUT_SUB_ROWS) // tr, off:off + OUT_SUB_ROWS, :] = x1


def _out_proj(x, u, vn, four_top, four_bot, attn, mod, ws, bs, fw, w_out, ln_g, ln_b, *,
              layer, mod_row, per_seq_mod):
    B, L, _ = x.shape
    nb, tr = _tile_split(B, L, OUT_ROW_TILE)
    assert (nb == 1 or not per_seq_mod) and tr % OUT_SUB_ROWS == 0
    row_spec = lambda w: pl.BlockSpec((nb, tr, w), lambda g, i: (g, i, 0))
    mod_idx = (lambda l, r: (lambda g, i: (l, r + g, 0, 0))) if per_seq_mod else None
    nh = (L // 2) // tr
    assert tr == L or (L // 2) % tr == 0
    if nh == 0:
        top_spec = bot_spec = pl.BlockSpec((L // 2, nb * D_B), lambda g, i: (0, g))
    else:
        top_spec = pl.BlockSpec((tr, D_B), lambda g, i: (jnp.minimum(i, nh - 1), g))
        bot_spec = pl.BlockSpec((tr, D_B), lambda g, i: (jnp.maximum(i - nh, 0), g))
    return pl.pallas_call(
        functools.partial(_out_kernel, tiles_per_half=nh),
        out_shape=jax.ShapeDtypeStruct((B, L, D_MODEL), _F32),
        grid=(B // nb, L // tr),
        in_specs=[
            row_spec(D_MODEL), row_spec(D_A), row_spec(D_A), top_spec, bot_spec, row_spec(D_C),
            _mod_spec(layer, mod_row, mod_idx),
            _layer_spec((A_HEADS * CHUNK, CHUNK), layer),
            _layer_spec((CHUNK, D_A), layer),
            _layer_spec((D_B, D_B), layer),
            _layer_spec((D_MODEL, D_MODEL), layer),
            _layer_spec((1, D_MODEL), layer),
            _layer_spec((1, D_MODEL), layer),
        ],
        out_specs=row_spec(D_MODEL),
        compiler_params=_params(("parallel", "parallel")),
        name="out_proj",
    )(x, u, vn, four_top, four_bot, attn, mod, ws, bs, fw, w_out, ln_g, ln_b)


def _ffn_kernel(x_ref, xp_ref, xn_ref, mod_ref, wgu_ref, cw_ref, cb_ref, wd_ref,
                g_ref, b_ref, o_ref, hcat_ref, act_ref, *, seq_len):
    tm = x_ref.shape[0]
    ext = tm + 2 * HALO
    inside_one_seq = seq_len % tm == 0
    m = mod_ref[...]

    def pre_norm(x):
        return (_norm(x) * (1.0 + m[4:5]) + m[3:4]).astype(_BF16)

    h_prev = pre_norm(xp_ref[...])
    h_next = pre_norm(xn_ref[...])
    if inside_one_seq:
        t = pl.program_id(0) % (seq_len // tm)
        hcat_ref[:HALO, :] = jnp.where(t > 0, h_prev, jnp.zeros_like(h_prev))
        hcat_ref[HALO + tm:, :] = jnp.where(t < seq_len // tm - 1, h_next, jnp.zeros_like(h_next))
    else:
        assert seq_len & (seq_len - 1) == 0
        hcat_ref[:HALO, :] = h_prev
        hcat_ref[HALO + tm:, :] = h_next
        pos = lax.broadcasted_iota(jnp.int32, (tm, 1), 0) & (seq_len - 1)
        has_prev = pos != 0
        has_next = pos != seq_len - 1
    for s in range(tm // FFN_SUB_ROWS):
        sub = slice(s * FFN_SUB_ROWS, (s + 1) * FFN_SUB_ROWS)
        hcat_ref[HALO + s * FFN_SUB_ROWS:HALO + (s + 1) * FFN_SUB_ROWS, :] = pre_norm(x_ref[sub, :])
    for j in range(D_FF // FF_CHUNK):
        cols = slice(j * FF_CHUNK, (j + 1) * FF_CHUNK)
        up_cols = slice(D_FF + j * FF_CHUNK, D_FF + (j + 1) * FF_CHUNK)
        g = _dot(hcat_ref[...], wgu_ref[:, cols])
        g_prev = pltpu.roll(g, 1, 0)[HALO:HALO + tm]
        g_next = pltpu.roll(g, ext - 1, 0)[HALO:HALO + tm]
        if not inside_one_seq:
            g_prev = jnp.where(has_prev, g_prev, 0.0)
            g_next = jnp.where(has_next, g_next, 0.0)
        gc = (cb_ref[:, cols] + g_prev * cw_ref[0:1, cols] + g[HALO:HALO + tm] * cw_ref[1:2, cols]
              + g_next * cw_ref[2:3, cols])
        up = _dot(hcat_ref[HALO:HALO + tm, :], wgu_ref[:, up_cols])
        act_ref[:, cols] = ((gc * jax.nn.sigmoid(gc)) * up).astype(_BF16)
    for s in range(tm // FFN_SUB_ROWS):
        sub = slice(s * FFN_SUB_ROWS, (s + 1) * FFN_SUB_ROWS)
        y = _dot(act_ref[sub, :], wd_ref[...])
        o_ref[sub, :] = _norm(ALPHA * x_ref[sub, :] + m[5:6] * y) * g_ref[...] + b_ref[...]


def _conv_ffn(x1, mod, w_gu, conv_w, conv_b, w_down, ln_g, ln_b, *, layer, mod_row, per_seq_mod):
    B, L, _ = x1.shape
    n_tok = B * L
    tm = FFN_ROW_TILE
    assert n_tok % tm == 0 and (L % tm == 0 or (tm % L == 0 and not per_seq_mod))
    x1 = x1.reshape(n_tok, D_MODEL)
    row_spec = pl.BlockSpec((tm, D_MODEL), lambda t: (t, 0))
    per = tm // HALO
    last = n_tok // HALO - 1
    prev_spec = pl.BlockSpec((HALO, D_MODEL), lambda t: (jnp.maximum(t * per - 1, 0), 0))
    next_spec = pl.BlockSpec((HALO, D_MODEL), lambda t: (jnp.minimum((t + 1) * per, last), 0))
    mod_idx = (lambda l, r: (lambda t: (l, r + (t * tm) // L, 0, 0))) if per_seq_mod else None
    out = pl.pallas_call(
        functools.partial(_ffn_kernel, seq_len=L),
        out_shape=jax.ShapeDtypeStruct((n_tok, D_MODEL), _F32),
        grid=(n_tok // tm,),
        in_specs=[
            row_spec, prev_spec, next_spec,
            _mod_spec(layer, mod_row, mod_idx),
            _const_spec((D_MODEL, 2 * D_FF)),
            _layer_spec((3, D_FF), layer),
            _layer_spec((1, D_FF), layer),
            _const_spec((D_FF, D_MODEL)),
            _layer_spec((1, D_MODEL), layer),
            _layer_spec((1, D_MODEL), layer),
        ],
        out_specs=row_spec,
        scratch_shapes=[pltpu.VMEM((tm + 2 * HALO, D_MODEL), _BF16), pltpu.VMEM((tm, D_FF), _BF16)],
        compiler_params=_params(("parallel",)),
        name="conv_ffn",
    )(x1, x1, x1, mod, w_gu, conv_w, conv_b, w_down, ln_g, ln_b)
    return out.reshape(B, L, D_MODEL)


def _dft_tables(n):
    idx = np.arange(n)
    ang = 2.0 * np.pi * ((idx[:, None] * idx[None, :]) % n) / n
    return np.cos(ang) / np.sqrt(n), np.sin(ang) / np.sqrt(n)


def _pos_dft_tables(n):
    cos, sin = _dft_tables(n)
    ti = _dft_tile_rows(n)
    tiles = lambda tab: np.stack([tab[t * ti:t * ti + ti + HALO] for t in range((n // 2) // ti)])
    rev = np.zeros((ti, ti + HALO))
    rev[np.arange(ti), ti - np.arange(ti)] = 1.0
    return tiles(cos), tiles(sin), rev


def _block_diag(blocks):
    n = len(blocks)
    d = blocks[0].shape[0]
    out = np.zeros((n * d, n * d), blocks[0].dtype)
    for g, blk in enumerate(blocks):
        out[g * d:(g + 1) * d, g * d:(g + 1) * d] = blk
    return out


def kernel(x_prompt, x_sample, cache_k, cache_v, c, c_ctx, w_ada, b_ada, w_in, a_ws, a_bs, f_w, rpb,
           w_out, ln1_g, ln1_b, w_gu, conv_w, conv_b, w_down, ln2_g, ln2_b):
    n_ctx_b, ctx_len, _ = x_prompt.shape
    n_lat_b, lat_len, _ = x_sample.shape
    past_len = cache_k.shape[2]

    cs = jnp.zeros((MOD_ROWS, D_MODEL), _F32).at[0].set(c_ctx).at[1:1 + n_lat_b].set(c)
    mod = _modulation(cs, w_ada, b_ada).reshape(DEPTH, MOD_ROWS, 6, D_MODEL)

    cc, sc = _dft_tables(B_GROUP_DIM)
    ch_cos = jnp.asarray(_block_diag([cc] * B_GROUPS), _F32).astype(_BF16)
    ch_sin = jnp.asarray(_block_diag([sc] * B_GROUPS), _F32).astype(_BF16)
    pos_tabs = {n: tuple(jnp.asarray(t, _F32).astype(_BF16) for t in _pos_dft_tables(n))
                for n in (ctx_len, lat_len)}

    w_in_b = w_in.astype(_BF16)
    w_out_b = w_out.astype(_BF16)
    ws = a_ws.reshape(DEPTH, A_HEADS * CHUNK, CHUNK).astype(_BF16)
    bs = jnp.repeat(a_bs.transpose(0, 2, 1), A_HEAD_DIM, axis=2)
    group_eye = jnp.eye(B_GROUPS, dtype=_F32)[None, :, None, :, None]
    fw = (f_w[:, :, :, None, :] * group_eye).reshape(DEPTH, D_B, D_B).astype(_BF16)
    vec = lambda p: p.reshape(DEPTH, 1, p.shape[-1])
    g1, b1, g2, b2, cb = vec(ln1_g), vec(ln1_b), vec(ln2_g), vec(ln2_b), vec(conv_b)
    bias = _na_bias_table(rpb)
    heads_first = (0, 1, 3, 4, 2)
    kct = cache_k.transpose(heads_first).reshape(n_lat_b, DEPTH, D_C, past_len)
    vct = cache_v.transpose(heads_first).reshape(n_lat_b, DEPTH, D_C, past_len)

    xp, xs = x_prompt, x_sample
    new_k, new_v = [], []
    for l in range(DEPTH):
        ctx = dict(layer=l, mod_row=0, per_seq_mod=False)
        lat = dict(layer=l, mod_row=1, per_seq_mod=True)
        *lat_proj, w_gu_l, w_down_l = _in_proj(xs, mod, w_in_b, ch_cos, ch_sin, (w_gu, w_down),
                                               kv_transposed=False, **lat)
        ctx_proj = _in_proj(xp, mod, w_in_b, ch_cos, ch_sin, kv_transposed=True, **ctx)

        def mix_and_ffn(x, proj, attend, where):
            u, vn, zc, zs, q, k, v = proj
            four_top, four_bot = _pos_dft(*pos_tabs[x.shape[1]], zc, zs)
            attn = attend(q, k, v)
            x1 = _out_proj(x, u, vn, four_top, four_bot, attn, mod, ws, bs, fw, w_out_b, g1, b1, **where)
            return _conv_ffn(x1, mod, w_gu_l, conv_w, cb, w_down_l, g2, b2, **where)

        new_k.append(ctx_proj[5])
        new_v.append(ctx_proj[6])
        xp = mix_and_ffn(xp, ctx_proj, _ctx_attention, ctx)
        xs = mix_and_ffn(xs, lat_proj, functools.partial(_na_attention, kct=kct, vct=vct, bias=bias, layer=l),
                         lat)

    def kv_out(per_layer):
        t = jnp.stack(per_layer, axis=1).reshape(n_ctx_b, DEPTH, C_HEADS, C_HEAD_DIM, ctx_len)
        return t.transpose(0, 1, 4, 2, 3)

    return (xp, xs, kv_out(new_k), kv_out(new_v))
```

```python
import functools

import numpy as np
import jax
import jax.numpy as jnp
from jax import lax
from jax.experimental import pallas as pl
from jax.experimental.pallas import tpu as pltpu

D_MODEL = 1024
DEPTH = 2
GRID_W = 64
D_A = 256
A_HEADS = 4
A_HEAD_DIM = 64
CHUNK = 128
D_B = 256
B_GROUPS = 4
B_GROUP_DIM = 64
D_C = 512
C_HEAD_DIM = 64
C_HEADS = 8
P_IN = 2 * D_A + D_B + 3 * D_C
NA_ROWS = 8
NA_KW = 16
D_FF = 2816
ALPHA = (2 * DEPTH) ** 0.25
LN_EPS = 1e-6
LOG2_E = 1.4426950408889634
Q_SCALE = C_HEAD_DIM ** -0.5 * LOG2_E

LANES = 128
V7X_VMEM_BYTES = 64 * 1024 * 1024
VMEM_LIMIT = V7X_VMEM_BYTES * 7 // 8
HEAD_PAIR = 2 * C_HEAD_DIM
N_HEAD_PAIRS = C_HEADS // 2
MOD_ROWS = 16
MOD_COL_TILE = 3072
CTX_ATTN_SEQS = 4
IN_ROW_TILE = 1024
IN_SUB_ROWS = 256
OUT_ROW_TILE = 1024
OUT_SUB_ROWS = 256
FFN_ROW_TILE = 1024
FFN_SUB_ROWS = 256
FF_CHUNK = 256
HALO = 16
DFT_ROW_TILE = 512
DFT_COL_TILE = 1024

_F32 = jnp.float32
_BF16 = jnp.bfloat16


def _dot(a, b):
    return jnp.dot(a, b, preferred_element_type=_F32)


def _dot_nt(a, b):
    return lax.dot_general(a, b, (((1,), (1,)), ((), ())), preferred_element_type=_F32)


def _norm(x):
    mu = jnp.mean(x, axis=-1, keepdims=True)
    xc = x - mu
    var = jnp.mean(xc * xc, axis=-1, keepdims=True)
    return xc * lax.rsqrt(var + LN_EPS)


def _params(sem):
    return pltpu.CompilerParams(dimension_semantics=sem, vmem_limit_bytes=VMEM_LIMIT)


def _const_spec(shape):
    nd = len(shape)
    return pl.BlockSpec(shape, lambda *_: (0,) * nd, pipeline_mode=pl.Buffered(1))


def _layer_spec(shape, layer):
    nd = len(shape)
    return pl.BlockSpec((None,) + shape, lambda *_: (layer,) + (0,) * nd, pipeline_mode=pl.Buffered(1))


def _mod_spec(layer, first_row, rows_per_seq):
    if rows_per_seq is None:
        return pl.BlockSpec((None, None, 6, D_MODEL), lambda *_: (layer, first_row, 0, 0))
    return pl.BlockSpec((None, None, 6, D_MODEL), rows_per_seq(layer, first_row))


def _mod_kernel(c_ref, w_ref, b_ref, o_ref):
    c = c_ref[...]
    s = c * jax.nn.sigmoid(c)
    o_ref[0] = _dot(s.astype(_BF16), w_ref[0].astype(_BF16)) + b_ref[0]


def _modulation(cs, w_ada, b_ada):
    n = cs.shape[0]
    tn = MOD_COL_TILE
    return pl.pallas_call(
        _mod_kernel,
        out_shape=jax.ShapeDtypeStruct((DEPTH, n, 6 * D_MODEL), _F32),
        grid=(DEPTH, 6 * D_MODEL // tn),
        in_specs=[
            pl.BlockSpec((n, D_MODEL), lambda l, j: (0, 0)),
            pl.BlockSpec((1, D_MODEL, tn), lambda l, j: (l, 0, j)),
            pl.BlockSpec((1, 1, tn), lambda l, j: (l, 0, j)),
        ],
        out_specs=pl.BlockSpec((1, n, tn), lambda l, j: (l, 0, j)),
        compiler_params=_params(("parallel", "parallel")),
        name="adaln_mod",
    )(cs, w_ada, b_ada.reshape(DEPTH, 1, 6 * D_MODEL))


def _tile_split(B, L, tile_rows):
    if L >= tile_rows:
        assert L % tile_rows == 0
        return 1, tile_rows
    nb = min(tile_rows // L, B)
    assert B % nb == 0
    return nb, L


def _in_kernel(x_ref, mod_ref, w_ref, cc_ref, cs_ref, *refs, kv_transposed, n_cast, kv_slot):
    n_prev = 0 if kv_slot is None else 2 * kv_slot
    cast_in, prev_kv = refs[:n_cast], refs[n_cast:n_cast + n_prev]
    u_ref, vn_ref, zc_ref, zs_ref, q_ref, k_ref, v_ref = refs[n_cast + n_prev:n_cast + n_prev + 7]
    cast_out = refs[n_cast + n_prev + 7:]
    for src, dst in zip(cast_in, cast_out):
        dst[...] = src[...].astype(dst.dtype)
    for j in range(n_prev // 2):
        k_ref[:, j] = prev_kv[2 * j][...]
        v_ref[:, j] = prev_kv[2 * j + 1][...]
    nb, tr, _ = x_ref.shape
    m = mod_ref[...]
    o1 = 2 * D_A
    o2 = o1 + D_B
    for t in range(nb * tr // IN_SUB_ROWS):
        seq = (t * IN_SUB_ROWS) // tr
        rows = slice((t * IN_SUB_ROWS) % tr, (t * IN_SUB_ROWS) % tr + IN_SUB_ROWS)
        h = _norm(x_ref[seq, rows, :]) * (1.0 + m[1:2]) + m[0:1]
        z = _dot(h.astype(_BF16), w_ref[...])
        za = jax.nn.gelu(z[:, :o1])
        u_ref[seq, rows, :] = za[:, :D_A]
        vn_ref[seq, rows, :] = _norm(za[:, D_A:]).astype(_BF16)
        zb = z[:, o1:o2].astype(_BF16)
        zc_ref[rows, seq * D_B:(seq + 1) * D_B] = _dot(zb, cc_ref[...]).astype(_BF16)
        zs_ref[rows, seq * D_B:(seq + 1) * D_B] = _dot(zb, cs_ref[...]).astype(_BF16)
        q_ref[seq, rows, :] = (z[:, o2:o2 + D_C] * Q_SCALE).astype(q_ref.dtype)
        k = z[:, o2 + D_C:o2 + 2 * D_C]
        v = z[:, o2 + 2 * D_C:o2 + 3 * D_C]
        if kv_transposed and kv_slot is not None:
            k_ref[seq, kv_slot, :, rows] = k.T.astype(k_ref.dtype)
            v_ref[seq, kv_slot, :, rows] = v.T.astype(v_ref.dtype)
        elif kv_transposed:
            k_ref[seq, :, rows] = k.T.astype(k_ref.dtype)
            v_ref[seq, :, rows] = v.T.astype(v_ref.dtype)
        else:
            k_ref[seq, rows, :] = k.astype(k_ref.dtype)
            v_ref[seq, rows, :] = v.astype(v_ref.dtype)


def _in_proj(x, mod, w_in, ch_cos, ch_sin, cast=(), prev_kv=None, *, layer, mod_row, per_seq_mod,
             kv_transposed):
    B, L, _ = x.shape
    nb, tr = _tile_split(B, L, IN_ROW_TILE)
    assert nb == 1 or not per_seq_mod
    n_i = L // tr
    n_steps = (B // nb) * n_i
    cast_in_specs, cast_out_specs, cast_shapes = [], [], []
    for p, p_layer in cast:
        _, R, C = p.shape
        assert R % (n_steps * HALO) == 0
        cast_in_specs.append(pl.BlockSpec((None, R // n_steps, C),
                                          functools.partial(lambda pl_, g, i: (pl_, g * n_i + i, 0), p_layer)))
        cast_out_specs.append(pl.BlockSpec((R // n_steps, C), lambda g, i: (g * n_i + i, 0)))
        cast_shapes.append(jax.ShapeDtypeStruct((R, C), _BF16))
    row = lambda w, dt: jax.ShapeDtypeStruct((B, L, w), dt)
    row_spec = lambda w: pl.BlockSpec((nb, tr, w), lambda g, i: (g, i, 0))
    four = jax.ShapeDtypeStruct((L, B * D_B), _BF16)
    four_spec = pl.BlockSpec((tr, nb * D_B), lambda g, i: (i, g))
    mod_idx = (lambda l, r: (lambda g, i: (l, r + g, 0, 0))) if per_seq_mod else None
    prev_specs, prev_args, kv_slot = [], [], None
    if kv_transposed and prev_kv is not None:
        kv_slot = len(prev_kv)
        kv = jax.ShapeDtypeStruct((B, kv_slot + 1, D_C, L), _F32)
        kv_spec = pl.BlockSpec((nb, kv_slot + 1, D_C, tr), lambda g, i: (g, 0, 0, i))
        for pair in prev_kv:
            prev_specs += [pl.BlockSpec((nb, D_C, tr), lambda g, i: (g, 0, i))] * 2
            prev_args += list(pair)
    elif kv_transposed:
        kv = jax.ShapeDtypeStruct((B, D_C, L), _F32)
        kv_spec = pl.BlockSpec((nb, D_C, tr), lambda g, i: (g, 0, i))
    else:
        kv = row(D_C, _BF16)
        kv_spec = row_spec(D_C)
    return pl.pallas_call(
        functools.partial(_in_kernel, kv_transposed=kv_transposed, n_cast=len(cast), kv_slot=kv_slot),
        out_shape=(row(D_A, _F32), row(D_A, _BF16), four, four, row(D_C, _BF16), kv, kv, *cast_shapes),
        grid=(B // nb, n_i),
        in_specs=[
            row_spec(D_MODEL),
            _mod_spec(layer, mod_row, mod_idx),
            _const_spec((D_MODEL, P_IN)),
            _const_spec((D_B, D_B)),
            _const_spec((D_B, D_B)),
            *cast_in_specs, *prev_specs,
        ],
        out_specs=(row_spec(D_A), row_spec(D_A), four_spec, four_spec, row_spec(D_C), kv_spec, kv_spec,
                   *cast_out_specs),
        compiler_params=_params(("parallel", "parallel")),
        name="in_proj",
    )(x, mod, w_in, ch_cos, ch_sin, *[p for p, _ in cast], *prev_args)


def _dft_kernel(c_ref, s_ref, zc_ref, zs_ref, rev_ref, top_ref, bot_ref):
    p = _dot(c_ref[0], zc_ref[...])
    q = _dot(s_ref[0], zs_ref[...])
    top_ref[...] = (p - q)[:top_ref.shape[0]].astype(top_ref.dtype)
    bot_ref[...] = _dot(rev_ref[...], (p + q).astype(_BF16)).astype(bot_ref.dtype)


def _dft_tile_rows(L):
    return min(L // 2, DFT_ROW_TILE)


def _pos_dft(pos_cos, pos_sin, rev, zc, zs):
    L, N = zc.shape
    ti = _dft_tile_rows(L)
    n_t = (L // 2) // ti
    tj = DFT_COL_TILE
    half = jax.ShapeDtypeStruct((L // 2, N), _BF16)
    tab_spec = pl.BlockSpec((1, ti + HALO, L), lambda t, j: (t, 0, 0))
    return pl.pallas_call(
        _dft_kernel,
        out_shape=(half, half),
        grid=(n_t, N // tj),
        in_specs=[
            tab_spec, tab_spec,
            pl.BlockSpec((L, tj), lambda t, j: (0, j)),
            pl.BlockSpec((L, tj), lambda t, j: (0, j)),
            pl.BlockSpec((ti, ti + HALO), lambda t, j: (0, 0)),
        ],
        out_specs=(pl.BlockSpec((ti, tj), lambda t, j: (t, j)),
                   pl.BlockSpec((ti, tj), lambda t, j: (n_t - 1 - t, j))),
        compiler_params=_params(("parallel", "parallel")),
        name="pos_dft",
    )(pos_cos, pos_sin, zc, zs, rev)


def _head_lane_mask(rows, head):
    lane = lax.broadcasted_iota(jnp.int32, (rows, HEAD_PAIR), 1)
    return (lane >= head * C_HEAD_DIM) & (lane < (head + 1) * C_HEAD_DIM)


def _ctx_attn_kernel(q_ref, kt_ref, vt_ref, o_ref):
    n_seq, rows, _ = q_ref.shape
    sel0 = _head_lane_mask(rows, 0)
    for seq in range(n_seq):
        for pair in range(N_HEAD_PAIRS):
            lanes = slice(pair * HEAD_PAIR, (pair + 1) * HEAD_PAIR)
            q = q_ref[seq, :, lanes]
            kt = kt_ref[seq, lanes, :].astype(_BF16)
            vt = vt_ref[seq, lanes, :].astype(_BF16)
            lhs = jnp.concatenate([jnp.where(sel0, q, jnp.zeros_like(q)),
                                   jnp.where(sel0, jnp.zeros_like(q), q)], axis=0)
            s = _dot(lhs, kt)
            p = jnp.exp2(s - jnp.max(s, axis=-1, keepdims=True))
            o = _dot_nt(p.astype(_BF16), vt) / jnp.sum(p, axis=-1, keepdims=True)
            o_ref[seq, :, lanes] = jnp.where(sel0, o[:rows], o[rows:]).astype(o_ref.dtype)


def _ctx_attention(q, kt, vt):
    B, L, _ = q.shape
    nb = min(CTX_ATTN_SEQS, B)
    assert B % nb == 0
    spec = pl.BlockSpec((nb, L, D_C), lambda b: (b, 0, 0))
    if kt.ndim == 4:
        slot = kt.shape[1] - 1
        spec_t = pl.BlockSpec((nb, None, D_C, L), lambda b: (b, slot, 0, 0))
    else:
        spec_t = pl.BlockSpec((nb, D_C, L), lambda b: (b, 0, 0))
    return pl.pallas_call(
        _ctx_attn_kernel,
        out_shape=jax.ShapeDtypeStruct((B, L, D_C), _BF16),
        grid=(B // nb,),
        in_specs=[spec, spec_t, spec_t],
        out_specs=spec,
        compiler_params=_params(("parallel",)),
        name="ctx_attention",
    )(q, kt, vt)


def _na_kernel(q_ref, k_ref, v_ref, kct_ref, vct_ref, bias_ref, o_ref, *, n_rows):
    kct = kct_ref[...].astype(_BF16)
    vct = vct_ref[...].astype(_BF16)
    win_keys = NA_ROWS * GRID_W
    pair_rows = 2 * GRID_W
    n_q = n_rows * GRID_W

    windows = []
    for r in range(n_rows):
        rs = min(max(r - NA_ROWS // 2, 0), n_rows - NA_ROWS)
        windows.append((slice(rs * GRID_W, rs * GRID_W + win_keys), rs - r + NA_ROWS - 1))

    q = q_ref[0]
    sel0 = _head_lane_mask(n_q, 0)
    q_heads = (jnp.where(sel0, q, jnp.zeros_like(q)), jnp.where(sel0, jnp.zeros_like(q), q))
    lhs = jnp.concatenate([qh[r * GRID_W:(r + 1) * GRID_W]
                           for r in range(n_rows) for qh in q_heads], axis=0)
    s_ctx = _dot(lhs, kct)
    s_loc = []
    for r, (win, base) in enumerate(windows):
        bias = jnp.concatenate(
            [jnp.concatenate([bias_ref[head, base + 2 * t] for t in range(NA_ROWS // 2)], axis=1)
             for head in range(2)], axis=0)
        s_loc.append(_dot_nt(lhs[r * pair_rows:(r + 1) * pair_rows], k_ref[0, win, :]) + bias)
    s_loc = jnp.concatenate(s_loc, axis=0)
    m = jnp.maximum(jnp.max(s_loc, axis=-1, keepdims=True), jnp.max(s_ctx, axis=-1, keepdims=True))
    p_loc = jnp.exp2(s_loc - m)
    p_ctx = jnp.exp2(s_ctx - m)
    denom = jnp.sum(p_loc, axis=-1, keepdims=True) + jnp.sum(p_ctx, axis=-1, keepdims=True)
    p_loc = p_loc.astype(_BF16)
    o_loc = jnp.concatenate(
        [_dot(p_loc[r * pair_rows:(r + 1) * pair_rows], v_ref[0, win, :])
         for r, (win, _) in enumerate(windows)], axis=0)
    o = (o_loc + _dot_nt(p_ctx.astype(_BF16), vct)) / denom
    row_sel0 = _head_lane_mask(GRID_W, 0)
    o_ref[0] = jnp.concatenate(
        [jnp.where(row_sel0, o[r * pair_rows:r * pair_rows + GRID_W],
                   o[r * pair_rows + GRID_W:(r + 1) * pair_rows])
         for r in range(n_rows)], axis=0).astype(o_ref.dtype)


def _na_bias_kernel(v_ref, o_ref):
    n_ri = 2 * NA_ROWS - 1
    qcol = lax.broadcasted_iota(jnp.int32, (GRID_W, LANES), 0)
    lane = lax.broadcasted_iota(jnp.int32, (GRID_W, LANES), 1)
    kcol = lane & (GRID_W - 1)
    start = jnp.clip(qcol - NA_KW // 2, 0, GRID_W - NA_KW)
    in_win = (kcol >= start) & (kcol < start + NA_KW)
    for head in range(v_ref.shape[0]):
        rows = [jnp.broadcast_to(v_ref[head, ri], (GRID_W, LANES)) for ri in range(n_ri)]
        for ri in range(n_ri - 1):
            left = pltpu.roll(rows[ri], GRID_W + 1, 1, stride=1, stride_axis=0)
            right = pltpu.roll(rows[ri + 1], 1, 1, stride=1, stride_axis=0)
            o_ref[head, ri] = jnp.where(in_win, jnp.where(lane < GRID_W, left, right) * LOG2_E, -jnp.inf)


def _na_bias_table(rpb):
    n_ri = 2 * NA_ROWS - 1
    lo = GRID_W - NA_KW
    v = jnp.pad(rpb, ((0, 0), (0, 0), (0, 0), (lo, LANES - lo - (2 * NA_KW - 1))))
    v = v.reshape(DEPTH * C_HEADS, n_ri, 1, LANES)
    return pl.pallas_call(
        _na_bias_kernel,
        out_shape=jax.ShapeDtypeStruct((DEPTH * C_HEADS, n_ri - 1, GRID_W, LANES), _F32),
        grid=(DEPTH,),
        in_specs=[pl.BlockSpec((C_HEADS, n_ri, 1, LANES), lambda i: (i, 0, 0, 0))],
        out_specs=pl.BlockSpec((C_HEADS, n_ri - 1, GRID_W, LANES), lambda i: (i, 0, 0, 0)),
        compiler_params=_params(("parallel",)),
        name="na_bias",
    )(v)


def _na_attention(q, k, v, kct, vct, bias, *, layer):
    B, L, _ = q.shape
    Lc = kct.shape[3]
    n_rows = L // GRID_W
    assert n_rows >= NA_ROWS and 2 * GRID_W == LANES
    lat = pl.BlockSpec((1, L, HEAD_PAIR), lambda p, b: (b, 0, p))
    ctx = pl.BlockSpec((None, None, HEAD_PAIR, Lc), lambda p, b: (b, layer, p, 0))
    return pl.pallas_call(
        functools.partial(_na_kernel, n_rows=n_rows),
        out_shape=jax.ShapeDtypeStruct((B, L, D_C), _BF16),
        grid=(N_HEAD_PAIRS, B),
        in_specs=[lat, lat, lat, ctx, ctx,
                  pl.BlockSpec((2, 2 * NA_ROWS - 2, GRID_W, LANES),
                               lambda p, b: (layer * N_HEAD_PAIRS + p, 0, 0, 0))],
        out_specs=lat,
        compiler_params=_params(("parallel", "parallel")),
        name="na_attention",
    )(q, k, v, kct, vct, bias)


def _out_kernel(x_ref, u_ref, vn_ref, ft_ref, fb_ref, a_ref, mod_ref, ws_ref, bs_ref, fw_ref, wo_ref,
                g_ref, b_ref, x1_ref, *, tiles_per_half):
    nb, tr, _ = x_ref.shape
    lane = lax.broadcasted_iota(jnp.int32, (CHUNK, D_A), 1)
    m = mod_ref[...]
    if tiles_per_half == 0:
        four = jnp.concatenate([half[:, s * D_B:(s + 1) * D_B]
                                for s in range(nb) for half in (ft_ref, fb_ref)], axis=0)
    else:
        four = jnp.where(pl.program_id(1) < tiles_per_half, ft_ref[...], fb_ref[...])

    def rows_of(ref, start, n):
        return ref[start // tr, start % tr:start % tr + n, :]

    for s in range(nb * tr // OUT_SUB_ROWS):
        sub = slice(s * OUT_SUB_ROWS, (s + 1) * OUT_SUB_ROWS)
        ya = []
        for n in range(OUT_SUB_ROWS // CHUNK):
            start = s * OUT_SUB_ROWS + n * CHUNK
            r = _dot(ws_ref[...], rows_of(vn_ref, start, CHUNK))
            mixed = r[:CHUNK]
            for hd in range(1, A_HEADS):
                mixed = jnp.where(lane >= hd * A_HEAD_DIM, r[hd * CHUNK:(hd + 1) * CHUNK], mixed)
            ya.append(rows_of(u_ref, start, CHUNK) * (mixed + bs_ref[...]))
        ya = jnp.concatenate(ya, axis=0).astype(_BF16)
        yb = _dot(four[sub], fw_ref[...]).astype(_BF16)
        y = _dot(jnp.concatenate([ya, yb, rows_of(a_ref, s * OUT_SUB_ROWS, OUT_SUB_ROWS)], axis=1), wo_ref[...])
        x1 = (_norm(ALPHA * rows_of(x_ref, s * OUT_SUB_ROWS, OUT_SUB_ROWS) + m[2:3] * y) * g_ref[...]
              + b_ref[...])
        off = (s * OUT_SUB_ROWS) % tr
        x1_ref[(s * OUT_SUB_ROWS) // tr, off:off + OUT_SUB_ROWS, :] = x1


def _out_proj(x, u, vn, four_top, four_bot, attn, mod, ws, bs, fw, w_out, ln_g, ln_b, *,
              layer, mod_row, per_seq_mod):
    B, L, _ = x.shape
    nb, tr = _tile_split(B, L, OUT_ROW_TILE)
    assert (nb == 1 or not per_seq_mod) and tr % OUT_SUB_ROWS == 0
    row_spec = lambda w: pl.BlockSpec((nb, tr, w), lambda g, i: (g, i, 0))
    mod_idx = (lambda l, r: (lambda g, i: (l, r + g, 0, 0))) if per_seq_mod else None
    nh = (L // 2) // tr
    assert tr == L or (L // 2) % tr == 0
    if nh == 0:
        top_spec = bot_spec = pl.BlockSpec((L // 2, nb * D_B), lambda g, i: (0, g))
    else:
        top_spec = pl.BlockSpec((tr, D_B), lambda g, i: (jnp.minimum(i, nh - 1), g))
        bot_spec = pl.BlockSpec((tr, D_B), lambda g, i: (jnp.maximum(i - nh, 0), g))
    return pl.pallas_call(
        functools.partial(_out_kernel, tiles_per_half=nh),
        out_shape=jax.ShapeDtypeStruct((B, L, D_MODEL), _F32),
        grid=(B // nb, L // tr),
        in_specs=[
            row_spec(D_MODEL), row_spec(D_A), row_spec(D_A), top_spec, bot_spec, row_spec(D_C),
            _mod_spec(layer, mod_row, mod_idx),
            _layer_spec((A_HEADS * CHUNK, CHUNK), layer),
            _layer_spec((CHUNK, D_A), layer),
            _layer_spec((D_B, D_B), layer),
            _const_spec((D_MODEL, D_MODEL)),
            _layer_spec((1, D_MODEL), layer),
            _layer_spec((1, D_MODEL), layer),
        ],
        out_specs=row_spec(D_MODEL),
        compiler_params=_params(("parallel", "parallel")),
        name="out_proj",
    )(x, u, vn, four_top, four_bot, attn, mod, ws, bs, fw, w_out, ln_g, ln_b)


def _ffn_kernel(x_ref, xp_ref, xn_ref, mod_ref, wgu_ref, cw_ref, cb_ref, wd_ref,
                g_ref, b_ref, o_ref, hcat_ref, act_ref, *, seq_len):
    tm = x_ref.shape[0]
    ext = tm + 2 * HALO
    inside_one_seq = seq_len % tm == 0
    m = mod_ref[...]

    def pre_norm(x):
        return (_norm(x) * (1.0 + m[4:5]) + m[3:4]).astype(_BF16)

    h_prev = pre_norm(xp_ref[...])
    h_next = pre_norm(xn_ref[...])
    if inside_one_seq:
        t = pl.program_id(0) % (seq_len // tm)
        hcat_ref[:HALO, :] = jnp.where(t > 0, h_prev, jnp.zeros_like(h_prev))
        hcat_ref[HALO + tm:, :] = jnp.where(t < seq_len // tm - 1, h_next, jnp.zeros_like(h_next))
    else:
        assert seq_len & (seq_len - 1) == 0
        hcat_ref[:HALO, :] = h_prev
        hcat_ref[HALO + tm:, :] = h_next
        pos = lax.broadcasted_iota(jnp.int32, (tm, 1), 0) & (seq_len - 1)
        has_prev = pos != 0
        has_next = pos != seq_len - 1
    for s in range(tm // FFN_SUB_ROWS):
        sub = slice(s * FFN_SUB_ROWS, (s + 1) * FFN_SUB_ROWS)
        hcat_ref[HALO + s * FFN_SUB_ROWS:HALO + (s + 1) * FFN_SUB_ROWS, :] = pre_norm(x_ref[sub, :])
    for j in range(D_FF // FF_CHUNK):
        cols = slice(j * FF_CHUNK, (j + 1) * FF_CHUNK)
        up_cols = slice(D_FF + j * FF_CHUNK, D_FF + (j + 1) * FF_CHUNK)
        g = _dot(hcat_ref[...], wgu_ref[:, cols])
        g_prev = pltpu.roll(g, 1, 0)[HALO:HALO + tm]
        g_next = pltpu.roll(g, ext - 1, 0)[HALO:HALO + tm]
        if not inside_one_seq:
            g_prev = jnp.where(has_prev, g_prev, 0.0)
            g_next = jnp.where(has_next, g_next, 0.0)
        gc = (cb_ref[:, cols] + g_prev * cw_ref[0:1, cols] + g[HALO:HALO + tm] * cw_ref[1:2, cols]
              + g_next * cw_ref[2:3, cols])
        up = _dot(hcat_ref[HALO:HALO + tm, :], wgu_ref[:, up_cols])
        act_ref[:, cols] = ((gc * jax.nn.sigmoid(gc)) * up).astype(_BF16)
    for s in range(tm // FFN_SUB_ROWS):
        sub = slice(s * FFN_SUB_ROWS, (s + 1) * FFN_SUB_ROWS)
        y = _dot(act_ref[sub, :], wd_ref[...])
        o_ref[sub, :] = _norm(ALPHA * x_ref[sub, :] + m[5:6] * y) * g_ref[...] + b_ref[...]


def _conv_ffn(x1, mod, w_gu, conv_w, conv_b, w_down, ln_g, ln_b, *, layer, mod_row, per_seq_mod):
    B, L, _ = x1.shape
    n_tok = B * L
    tm = FFN_ROW_TILE
    assert n_tok % tm == 0 and (L % tm == 0 or (tm % L == 0 and not per_seq_mod))
    x1 = x1.reshape(n_tok, D_MODEL)
    row_spec = pl.BlockSpec((tm, D_MODEL), lambda t: (t, 0))
    per = tm // HALO
    last = n_tok // HALO - 1
    prev_spec = pl.BlockSpec((HALO, D_MODEL), lambda t: (jnp.maximum(t * per - 1, 0), 0))
    next_spec = pl.BlockSpec((HALO, D_MODEL), lambda t: (jnp.minimum((t + 1) * per, last), 0))
    mod_idx = (lambda l, r: (lambda t: (l, r + (t * tm) // L, 0, 0))) if per_seq_mod else None
    out = pl.pallas_call(
        functools.partial(_ffn_kernel, seq_len=L),
        out_shape=jax.ShapeDtypeStruct((n_tok, D_MODEL), _F32),
        grid=(n_tok // tm,),
        in_specs=[
            row_spec, prev_spec, next_spec,
            _mod_spec(layer, mod_row, mod_idx),
            _const_spec((D_MODEL, 2 * D_FF)),
            _layer_spec((3, D_FF), layer),
            _layer_spec((1, D_FF), layer),
            _const_spec((D_FF, D_MODEL)),
            _layer_spec((1, D_MODEL), layer),
            _layer_spec((1, D_MODEL), layer),
        ],
        out_specs=row_spec,
        scratch_shapes=[pltpu.VMEM((tm + 2 * HALO, D_MODEL), _BF16), pltpu.VMEM((tm, D_FF), _BF16)],
        compiler_params=_params(("parallel",)),
        name="conv_ffn",
    )(x1, x1, x1, mod, w_gu, conv_w, conv_b, w_down, ln_g, ln_b)
    return out.reshape(B, L, D_MODEL)


def _dft_tables(n):
    idx = np.arange(n)
    ang = 2.0 * np.pi * ((idx[:, None] * idx[None, :]) % n) / n
    return np.cos(ang) / np.sqrt(n), np.sin(ang) / np.sqrt(n)


def _pos_dft_tables(n):
    cos, sin = _dft_tables(n)
    ti = _dft_tile_rows(n)
    tiles = lambda tab: np.stack([tab[t * ti:t * ti + ti + HALO] for t in range((n // 2) // ti)])
    rev = np.zeros((ti, ti + HALO))
    rev[np.arange(ti), ti - np.arange(ti)] = 1.0
    return tiles(cos), tiles(sin), rev


def _block_diag(blocks):
    n = len(blocks)
    d = blocks[0].shape[0]
    out = np.zeros((n * d, n * d), blocks[0].dtype)
    for g, blk in enumerate(blocks):
        out[g * d:(g + 1) * d, g * d:(g + 1) * d] = blk
    return out


def kernel(x_prompt, x_sample, cache_k, cache_v, c, c_ctx, w_ada, b_ada, w_in, a_ws, a_bs, f_w, rpb,
           w_out, ln1_g, ln1_b, w_gu, conv_w, conv_b, w_down, ln2_g, ln2_b):
    n_ctx_b, ctx_len, _ = x_prompt.shape
    n_lat_b, lat_len, _ = x_sample.shape
    past_len = cache_k.shape[2]

    cs = jnp.zeros((MOD_ROWS, D_MODEL), _F32).at[0].set(c_ctx).at[1:1 + n_lat_b].set(c)
    mod = _modulation(cs, w_ada, b_ada).reshape(DEPTH, MOD_ROWS, 6, D_MODEL)

    cc, sc = _dft_tables(B_GROUP_DIM)
    ch_cos = jnp.asarray(_block_diag([cc] * B_GROUPS), _F32).astype(_BF16)
    ch_sin = jnp.asarray(_block_diag([sc] * B_GROUPS), _F32).astype(_BF16)
    pos_tabs = {n: tuple(jnp.asarray(t, _F32).astype(_BF16) for t in _pos_dft_tables(n))
                for n in (ctx_len, lat_len)}

    w_in_l = w_in[0].astype(_BF16)
    ws = a_ws.reshape(DEPTH, A_HEADS * CHUNK, CHUNK).astype(_BF16)
    bs = jnp.repeat(a_bs.transpose(0, 2, 1), A_HEAD_DIM, axis=2)
    group_eye = jnp.eye(B_GROUPS, dtype=_F32)[None, :, None, :, None]
    fw = (f_w[:, :, :, None, :] * group_eye).reshape(DEPTH, D_B, D_B).astype(_BF16)
    vec = lambda p: p.reshape(DEPTH, 1, p.shape[-1])
    g1, b1, g2, b2, cb = vec(ln1_g), vec(ln1_b), vec(ln2_g), vec(ln2_b), vec(conv_b)
    bias = _na_bias_table(rpb)
    heads_first = (0, 1, 3, 4, 2)
    kct = cache_k.transpose(heads_first).reshape(n_lat_b, DEPTH, D_C, past_len)
    vct = cache_v.transpose(heads_first).reshape(n_lat_b, DEPTH, D_C, past_len)

    xp, xs = x_prompt, x_sample
    new_k, new_v = [], []
    for l in range(DEPTH):
        ctx = dict(layer=l, mod_row=0, per_seq_mod=False)
        lat = dict(layer=l, mod_row=1, per_seq_mod=True)
        cast = [(w_gu, l), (w_down, l), (w_out, l)] + ([(w_in, l + 1)] if l + 1 < DEPTH else [])
        lat_out = _in_proj(xs, mod, w_in_l, ch_cos, ch_sin, cast, kv_transposed=False, **lat)
        lat_proj, (w_gu_l, w_down_l, w_out_l) = lat_out[:7], lat_out[7:10]
        ctx_proj = _in_proj(xp, mod, w_in_l, ch_cos, ch_sin,
                            prev_kv=list(zip(new_k, new_v)) if l == DEPTH - 1 else None,
                            kv_transposed=True, **ctx)

        def mix_and_ffn(x, proj, attend, where):
            u, vn, zc, zs, q, k, v = proj
            four_top, four_bot = _pos_dft(*pos_tabs[x.shape[1]], zc, zs)
            attn = attend(q, k, v)
            x1 = _out_proj(x, u, vn, four_top, four_bot, attn, mod, ws, bs, fw, w_out_l, g1, b1, **where)
            return _conv_ffn(x1, mod, w_gu_l, conv_w, cb, w_down_l, g2, b2, **where)

        new_k.append(ctx_proj[5])
        new_v.append(ctx_proj[6])
        xp = mix_and_ffn(xp, ctx_proj, _ctx_attention, ctx)
        xs = mix_and_ffn(xs, lat_proj, functools.partial(_na_attention, kct=kct, vct=vct, bias=bias, layer=l),
                         lat)
        if l + 1 < DEPTH:
            w_in_l = lat_out[10]

    def kv_out(stacked):
        return stacked.reshape(n_ctx_b, DEPTH, C_HEADS, C_HEAD_DIM, ctx_len).transpose(0, 1, 4, 2, 3)

    return (xp, xs, kv_out(new_k[-1]), kv_out(new_v[-1]))
```

```python
import functools

import numpy as np
import jax
import jax.numpy as jnp
from jax import lax
from jax.experimental import pallas as pl
from jax.experimental.pallas import tpu as pltpu

D_MODEL = 1024
DEPTH = 2
GRID_W = 64
D_A = 256
A_HEADS = 4
A_HEAD_DIM = 64
CHUNK = 128
D_B = 256
B_GROUPS = 4
B_GROUP_DIM = 64
D_C = 512
C_HEAD_DIM = 64
C_HEADS = 8
P_IN = 2 * D_A + D_B + 3 * D_C
NA_ROWS = 8
NA_KW = 16
D_FF = 2816
ALPHA = (2 * DEPTH) ** 0.25
LN_EPS = 1e-6
LOG2_E = 1.4426950408889634
Q_SCALE = C_HEAD_DIM ** -0.5 * LOG2_E

LANES = 128
V7X_VMEM_BYTES = 64 * 1024 * 1024
VMEM_LIMIT = V7X_VMEM_BYTES * 7 // 8
HEAD_PAIR = 2 * C_HEAD_DIM
N_HEAD_PAIRS = C_HEADS // 2
MOD_ROWS = 16
MOD_COL_TILE = 3072
CTX_ATTN_SEQS = 4
IN_ROW_TILE = 1024
IN_SUB_ROWS = 256
OUT_ROW_TILE = 1024
OUT_SUB_ROWS = 256
FFN_ROW_TILE = 1024
FFN_SUB_ROWS = 256
FF_CHUNK = 256
HALO = 16
DFT_ROW_TILE = 512
DFT_COL_TILE = 1024

_F32 = jnp.float32
_BF16 = jnp.bfloat16


def _dot(a, b):
    return jnp.dot(a, b, preferred_element_type=_F32)


def _dot_nt(a, b):
    return lax.dot_general(a, b, (((1,), (1,)), ((), ())), preferred_element_type=_F32)


def _norm(x):
    mu = jnp.mean(x, axis=-1, keepdims=True)
    xc = x - mu
    var = jnp.mean(xc * xc, axis=-1, keepdims=True)
    return xc * lax.rsqrt(var + LN_EPS)


def _params(sem):
    return pltpu.CompilerParams(dimension_semantics=sem, vmem_limit_bytes=VMEM_LIMIT)


def _const_spec(shape):
    nd = len(shape)
    return pl.BlockSpec(shape, lambda *_: (0,) * nd, pipeline_mode=pl.Buffered(1))


def _layer_spec(shape, layer):
    nd = len(shape)
    return pl.BlockSpec((None,) + shape, lambda *_: (layer,) + (0,) * nd, pipeline_mode=pl.Buffered(1))


def _mod_spec(layer, first_row, rows_per_seq):
    if rows_per_seq is None:
        return pl.BlockSpec((None, None, 6, D_MODEL), lambda *_: (layer, first_row, 0, 0))
    return pl.BlockSpec((None, None, 6, D_MODEL), rows_per_seq(layer, first_row))


def _mod_kernel(c_ref, w_ref, b_ref, o_ref):
    c = c_ref[...]
    s = c * jax.nn.sigmoid(c)
    o_ref[0] = _dot(s.astype(_BF16), w_ref[0].astype(_BF16)) + b_ref[0]


def _modulation(cs, w_ada, b_ada):
    n = cs.shape[0]
    tn = MOD_COL_TILE
    return pl.pallas_call(
        _mod_kernel,
        out_shape=jax.ShapeDtypeStruct((1, n, 6 * D_MODEL), _F32),
        grid=(1, 6 * D_MODEL // tn),
        in_specs=[
            pl.BlockSpec((n, D_MODEL), lambda l, j: (0, 0)),
            pl.BlockSpec((1, D_MODEL, tn), lambda l, j: (l, 0, j)),
            pl.BlockSpec((1, 1, tn), lambda l, j: (l, 0, j)),
        ],
        out_specs=pl.BlockSpec((1, n, tn), lambda l, j: (l, 0, j)),
        compiler_params=_params(("parallel", "parallel")),
        name="adaln_mod",
    )(cs, w_ada, b_ada)


def _tile_split(B, L, tile_rows):
    if L >= tile_rows:
        assert L % tile_rows == 0
        return 1, tile_rows
    nb = min(tile_rows // L, B)
    assert B % nb == 0
    return nb, L


def _in_kernel(x_ref, mod_ref, w_ref, cc_ref, cs_ref, *refs, kv_transposed, n_cast, kv_slot):
    n_prev = 0 if kv_slot is None else 2 * kv_slot
    cast_in, prev_kv = refs[:n_cast], refs[n_cast:n_cast + n_prev]
    u_ref, vn_ref, zc_ref, zs_ref, q_ref, k_ref, v_ref = refs[n_cast + n_prev:n_cast + n_prev + 7]
    cast_out = refs[n_cast + n_prev + 7:]
    for src, dst in zip(cast_in, cast_out):
        dst[...] = src[...].astype(dst.dtype)
    for j in range(n_prev // 2):
        k_ref[:, j] = prev_kv[2 * j][...]
        v_ref[:, j] = prev_kv[2 * j + 1][...]
    nb, tr, _ = x_ref.shape
    m = mod_ref[...]
    o1 = 2 * D_A
    o2 = o1 + D_B
    for t in range(nb * tr // IN_SUB_ROWS):
        seq = (t * IN_SUB_ROWS) // tr
        rows = slice((t * IN_SUB_ROWS) % tr, (t * IN_SUB_ROWS) % tr + IN_SUB_ROWS)
        h = _norm(x_ref[seq, rows, :]) * (1.0 + m[1:2]) + m[0:1]
        z = _dot(h.astype(_BF16), w_ref[...])
        za = jax.nn.gelu(z[:, :o1])
        u_ref[seq, rows, :] = za[:, :D_A]
        vn_ref[seq, rows, :] = _norm(za[:, D_A:]).astype(_BF16)
        zb = z[:, o1:o2].astype(_BF16)
        zc_ref[rows, seq * D_B:(seq + 1) * D_B] = _dot(zb, cc_ref[...]).astype(_BF16)
        zs_ref[rows, seq * D_B:(seq + 1) * D_B] = _dot(zb, cs_ref[...]).astype(_BF16)
        q_ref[seq, rows, :] = (z[:, o2:o2 + D_C] * Q_SCALE).astype(q_ref.dtype)
        k = z[:, o2 + D_C:o2 + 2 * D_C]
        v = z[:, o2 + 2 * D_C:o2 + 3 * D_C]
        if kv_transposed and kv_slot is not None:
            k_ref[seq, kv_slot, :, rows] = k.T.astype(k_ref.dtype)
            v_ref[seq, kv_slot, :, rows] = v.T.astype(v_ref.dtype)
        elif kv_transposed:
            k_ref[seq, :, rows] = k.T.astype(k_ref.dtype)
            v_ref[seq, :, rows] = v.T.astype(v_ref.dtype)
        else:
            k_ref[seq, rows, :] = k.astype(k_ref.dtype)
            v_ref[seq, rows, :] = v.astype(v_ref.dtype)


def _in_proj(x, mod, w_in, ch_cos, ch_sin, cast=(), prev_kv=None, *, layer, mod_row, per_seq_mod,
             kv_transposed):
    B, L, _ = x.shape
    nb, tr = _tile_split(B, L, IN_ROW_TILE)
    assert nb == 1 or not per_seq_mod
    n_i = L // tr
    n_steps = (B // nb) * n_i
    cast_in_specs, cast_out_specs, cast_shapes = [], [], []
    for p, p_layer in cast:
        _, R, C = p.shape
        assert R % (n_steps * HALO) == 0
        cast_in_specs.append(pl.BlockSpec((None, R // n_steps, C),
                                          functools.partial(lambda pl_, g, i: (pl_, g * n_i + i, 0), p_layer)))
        cast_out_specs.append(pl.BlockSpec((R // n_steps, C), lambda g, i: (g * n_i + i, 0)))
        cast_shapes.append(jax.ShapeDtypeStruct((R, C), _BF16))
    row = lambda w, dt: jax.ShapeDtypeStruct((B, L, w), dt)
    row_spec = lambda w: pl.BlockSpec((nb, tr, w), lambda g, i: (g, i, 0))
    four = jax.ShapeDtypeStruct((L, B * D_B), _BF16)
    four_spec = pl.BlockSpec((tr, nb * D_B), lambda g, i: (i, g))
    mod_idx = (lambda l, r: (lambda g, i: (l, r + g, 0, 0))) if per_seq_mod else None
    prev_specs, prev_args, kv_slot = [], [], None
    if kv_transposed and prev_kv is not None:
        kv_slot = len(prev_kv)
        kv = jax.ShapeDtypeStruct((B, kv_slot + 1, D_C, L), _F32)
        kv_spec = pl.BlockSpec((nb, kv_slot + 1, D_C, tr), lambda g, i: (g, 0, 0, i))
        for pair in prev_kv:
            prev_specs += [pl.BlockSpec((nb, D_C, tr), lambda g, i: (g, 0, i))] * 2
            prev_args += list(pair)
    elif kv_transposed:
        kv = jax.ShapeDtypeStruct((B, D_C, L), _F32)
        kv_spec = pl.BlockSpec((nb, D_C, tr), lambda g, i: (g, 0, i))
    else:
        kv = row(D_C, _BF16)
        kv_spec = row_spec(D_C)
    return pl.pallas_call(
        functools.partial(_in_kernel, kv_transposed=kv_transposed, n_cast=len(cast), kv_slot=kv_slot),
        out_shape=(row(D_A, _F32), row(D_A, _BF16), four, four, row(D_C, _BF16), kv, kv, *cast_shapes),
        grid=(B // nb, n_i),
        in_specs=[
            row_spec(D_MODEL),
            _mod_spec(0, mod_row, mod_idx),
            _const_spec((D_MODEL, P_IN)),
            _const_spec((D_B, D_B)),
            _const_spec((D_B, D_B)),
            *cast_in_specs, *prev_specs,
        ],
        out_specs=(row_spec(D_A), row_spec(D_A), four_spec, four_spec, row_spec(D_C), kv_spec, kv_spec,
                   *cast_out_specs),
        compiler_params=_params(("parallel", "parallel")),
        name="in_proj",
    )(x, mod, w_in, ch_cos, ch_sin, *[p for p, _ in cast], *prev_args)


def _dft_kernel(c_ref, s_ref, zc_ref, zs_ref, rev_ref, top_ref, bot_ref):
    p = _dot(c_ref[0], zc_ref[...])
    q = _dot(s_ref[0], zs_ref[...])
    top_ref[...] = (p - q)[:top_ref.shape[0]].astype(top_ref.dtype)
    bot_ref[...] = _dot(rev_ref[...], (p + q).astype(_BF16)).astype(bot_ref.dtype)


def _dft_tile_rows(L):
    return min(L // 2, DFT_ROW_TILE)


def _pos_dft(pos_cos, pos_sin, rev, zc, zs):
    L, N = zc.shape
    ti = _dft_tile_rows(L)
    n_t = (L // 2) // ti
    tj = DFT_COL_TILE
    half = jax.ShapeDtypeStruct((L // 2, N), _BF16)
    tab_spec = pl.BlockSpec((1, ti + HALO, L), lambda t, j: (t, 0, 0))
    return pl.pallas_call(
        _dft_kernel,
        out_shape=(half, half),
        grid=(n_t, N // tj),
        in_specs=[
            tab_spec, tab_spec,
            pl.BlockSpec((L, tj), lambda t, j: (0, j)),
            pl.BlockSpec((L, tj), lambda t, j: (0, j)),
            pl.BlockSpec((ti, ti + HALO), lambda t, j: (0, 0)),
        ],
        out_specs=(pl.BlockSpec((ti, tj), lambda t, j: (t, j)),
                   pl.BlockSpec((ti, tj), lambda t, j: (n_t - 1 - t, j))),
        compiler_params=_params(("parallel", "parallel")),
        name="pos_dft",
    )(pos_cos, pos_sin, zc, zs, rev)


def _head_lane_mask(rows, head):
    lane = lax.broadcasted_iota(jnp.int32, (rows, HEAD_PAIR), 1)
    return (lane >= head * C_HEAD_DIM) & (lane < (head + 1) * C_HEAD_DIM)


def _ctx_attn_kernel(q_ref, kt_ref, vt_ref, o_ref):
    n_seq, rows, _ = q_ref.shape
    sel0 = _head_lane_mask(rows, 0)
    for seq in range(n_seq):
        for pair in range(N_HEAD_PAIRS):
            lanes = slice(pair * HEAD_PAIR, (pair + 1) * HEAD_PAIR)
            q = q_ref[seq, :, lanes]
            kt = kt_ref[seq, lanes, :].astype(_BF16)
            vt = vt_ref[seq, lanes, :].astype(_BF16)
            lhs = jnp.concatenate([jnp.where(sel0, q, jnp.zeros_like(q)),
                                   jnp.where(sel0, jnp.zeros_like(q), q)], axis=0)
            s = _dot(lhs, kt)
            p = jnp.exp2(s - jnp.max(s, axis=-1, keepdims=True))
            o = _dot_nt(p.astype(_BF16), vt) / jnp.sum(p, axis=-1, keepdims=True)
            o_ref[seq, :, lanes] = jnp.where(sel0, o[:rows], o[rows:]).astype(o_ref.dtype)


def _ctx_attention(q, kt, vt):
    B, L, _ = q.shape
    nb = min(CTX_ATTN_SEQS, B)
    assert B % nb == 0
    spec = pl.BlockSpec((nb, L, D_C), lambda b: (b, 0, 0))
    if kt.ndim == 4:
        slot = kt.shape[1] - 1
        spec_t = pl.BlockSpec((nb, None, D_C, L), lambda b: (b, slot, 0, 0))
    else:
        spec_t = pl.BlockSpec((nb, D_C, L), lambda b: (b, 0, 0))
    return pl.pallas_call(
        _ctx_attn_kernel,
        out_shape=jax.ShapeDtypeStruct((B, L, D_C), _BF16),
        grid=(B // nb,),
        in_specs=[spec, spec_t, spec_t],
        out_specs=spec,
        compiler_params=_params(("parallel",)),
        name="ctx_attention",
    )(q, kt, vt)


def _na_kernel(q_ref, k_ref, v_ref, kct_ref, vct_ref, bias_ref, o_ref, *, n_rows):
    kct = kct_ref[...].astype(_BF16)
    vct = vct_ref[...].astype(_BF16)
    win_keys = NA_ROWS * GRID_W
    pair_rows = 2 * GRID_W
    n_q = n_rows * GRID_W

    windows = []
    for r in range(n_rows):
        rs = min(max(r - NA_ROWS // 2, 0), n_rows - NA_ROWS)
        windows.append((slice(rs * GRID_W, rs * GRID_W + win_keys), rs - r + NA_ROWS - 1))

    q = q_ref[0]
    sel0 = _head_lane_mask(n_q, 0)
    q_heads = (jnp.where(sel0, q, jnp.zeros_like(q)), jnp.where(sel0, jnp.zeros_like(q), q))
    lhs = jnp.concatenate([qh[r * GRID_W:(r + 1) * GRID_W]
                           for r in range(n_rows) for qh in q_heads], axis=0)
    s_ctx = _dot(lhs, kct)
    s_loc = []
    for r, (win, base) in enumerate(windows):
        bias = jnp.concatenate(
            [jnp.concatenate([bias_ref[head, base + 2 * t] for t in range(NA_ROWS // 2)], axis=1)
             for head in range(2)], axis=0)
        s_loc.append(_dot_nt(lhs[r * pair_rows:(r + 1) * pair_rows], k_ref[0, win, :]) + bias)
    s_loc = jnp.concatenate(s_loc, axis=0)
    m = jnp.maximum(jnp.max(s_loc, axis=-1, keepdims=True), jnp.max(s_ctx, axis=-1, keepdims=True))
    p_loc = jnp.exp2(s_loc - m)
    p_ctx = jnp.exp2(s_ctx - m)
    denom = jnp.sum(p_loc, axis=-1, keepdims=True) + jnp.sum(p_ctx, axis=-1, keepdims=True)
    p_loc = p_loc.astype(_BF16)
    o_loc = jnp.concatenate(
        [_dot(p_loc[r * pair_rows:(r + 1) * pair_rows], v_ref[0, win, :])
         for r, (win, _) in enumerate(windows)], axis=0)
    o = (o_loc + _dot_nt(p_ctx.astype(_BF16), vct)) / denom
    row_sel0 = _head_lane_mask(GRID_W, 0)
    o_ref[0] = jnp.concatenate(
        [jnp.where(row_sel0, o[r * pair_rows:r * pair_rows + GRID_W],
                   o[r * pair_rows + GRID_W:(r + 1) * pair_rows])
         for r in range(n_rows)], axis=0).astype(o_ref.dtype)


def _na_bias_kernel(v_ref, o_ref):
    n_ri = 2 * NA_ROWS - 1
    qcol = lax.broadcasted_iota(jnp.int32, (GRID_W, LANES), 0)
    lane = lax.broadcasted_iota(jnp.int32, (GRID_W, LANES), 1)
    kcol = lane & (GRID_W - 1)
    start = jnp.clip(qcol - NA_KW // 2, 0, GRID_W - NA_KW)
    in_win = (kcol >= start) & (kcol < start + NA_KW)
    for head in range(v_ref.shape[0]):
        rows = [jnp.broadcast_to(v_ref[head, ri], (GRID_W, LANES)) for ri in range(n_ri)]
        for ri in range(n_ri - 1):
            left = pltpu.roll(rows[ri], GRID_W + 1, 1, stride=1, stride_axis=0)
            right = pltpu.roll(rows[ri + 1], 1, 1, stride=1, stride_axis=0)
            o_ref[head, ri] = jnp.where(in_win, jnp.where(lane < GRID_W, left, right) * LOG2_E, -jnp.inf)


def _na_bias_table(rpb):
    n_ri = 2 * NA_ROWS - 1
    lo = GRID_W - NA_KW
    v = jnp.pad(rpb, ((0, 0), (0, 0), (0, 0), (lo, LANES - lo - (2 * NA_KW - 1))))
    v = v.reshape(DEPTH * C_HEADS, n_ri, 1, LANES)
    return pl.pallas_call(
        _na_bias_kernel,
        out_shape=jax.ShapeDtypeStruct((DEPTH * C_HEADS, n_ri - 1, GRID_W, LANES), _F32),
        grid=(DEPTH,),
        in_specs=[pl.BlockSpec((C_HEADS, n_ri, 1, LANES), lambda i: (i, 0, 0, 0))],
        out_specs=pl.BlockSpec((C_HEADS, n_ri - 1, GRID_W, LANES), lambda i: (i, 0, 0, 0)),
        compiler_params=_params(("parallel",)),
        name="na_bias",
    )(v)


def _na_attention(q, k, v, kct, vct, bias, *, layer):
    B, L, _ = q.shape
    Lc = kct.shape[3]
    n_rows = L // GRID_W
    assert n_rows >= NA_ROWS and 2 * GRID_W == LANES
    lat = pl.BlockSpec((1, L, HEAD_PAIR), lambda p, b: (b, 0, p))
    ctx = pl.BlockSpec((None, None, HEAD_PAIR, Lc), lambda p, b: (b, layer, p, 0))
    return pl.pallas_call(
        functools.partial(_na_kernel, n_rows=n_rows),
        out_shape=jax.ShapeDtypeStruct((B, L, D_C), _BF16),
        grid=(N_HEAD_PAIRS, B),
        in_specs=[lat, lat, lat, ctx, ctx,
                  pl.BlockSpec((2, 2 * NA_ROWS - 2, GRID_W, LANES),
                               lambda p, b: (layer * N_HEAD_PAIRS + p, 0, 0, 0))],
        out_specs=lat,
        compiler_params=_params(("parallel", "parallel")),
        name="na_attention",
    )(q, k, v, kct, vct, bias)


def _out_kernel(x_ref, u_ref, vn_ref, ft_ref, fb_ref, a_ref, mod_ref, ws_ref, bs_ref, fw_ref, wo_ref,
                g_ref, b_ref, x1_ref, *, tiles_per_half):
    nb, tr, _ = x_ref.shape
    lane = lax.broadcasted_iota(jnp.int32, (CHUNK, D_A), 1)
    m = mod_ref[...]
    if tiles_per_half == 0:
        four = jnp.concatenate([half[:, s * D_B:(s + 1) * D_B]
                                for s in range(nb) for half in (ft_ref, fb_ref)], axis=0)
    else:
        four = jnp.where(pl.program_id(1) < tiles_per_half, ft_ref[...], fb_ref[...])

    def rows_of(ref, start, n):
        return ref[start // tr, start % tr:start % tr + n, :]

    for s in range(nb * tr // OUT_SUB_ROWS):
        sub = slice(s * OUT_SUB_ROWS, (s + 1) * OUT_SUB_ROWS)
        ya = []
        for n in range(OUT_SUB_ROWS // CHUNK):
            start = s * OUT_SUB_ROWS + n * CHUNK
            r = _dot(ws_ref[...], rows_of(vn_ref, start, CHUNK))
            mixed = r[:CHUNK]
            for hd in range(1, A_HEADS):
                mixed = jnp.where(lane >= hd * A_HEAD_DIM, r[hd * CHUNK:(hd + 1) * CHUNK], mixed)
            ya.append(rows_of(u_ref, start, CHUNK) * (mixed + bs_ref[...]))
        ya = jnp.concatenate(ya, axis=0).astype(_BF16)
        yb = _dot(four[sub], fw_ref[...]).astype(_BF16)
        y = _dot(jnp.concatenate([ya, yb, rows_of(a_ref, s * OUT_SUB_ROWS, OUT_SUB_ROWS)], axis=1), wo_ref[...])
        x1 = (_norm(ALPHA * rows_of(x_ref, s * OUT_SUB_ROWS, OUT_SUB_ROWS) + m[2:3] * y) * g_ref[...]
              + b_ref[...])
        off = (s * OUT_SUB_ROWS) % tr
        x1_ref[(s * OUT_SUB_ROWS) // tr, off:off + OUT_SUB_ROWS, :] = x1


def _out_proj(x, u, vn, four_top, four_bot, attn, mod, ws, bs, fw, w_out, ln_g, ln_b, *,
              layer, mod_row, per_seq_mod):
    B, L, _ = x.shape
    nb, tr = _tile_split(B, L, OUT_ROW_TILE)
    assert (nb == 1 or not per_seq_mod) and tr % OUT_SUB_ROWS == 0
    row_spec = lambda w: pl.BlockSpec((nb, tr, w), lambda g, i: (g, i, 0))
    mod_idx = (lambda l, r: (lambda g, i: (l, r + g, 0, 0))) if per_seq_mod else None
    nh = (L // 2) // tr
    assert tr == L or (L // 2) % tr == 0
    if nh == 0:
        top_spec = bot_spec = pl.BlockSpec((L // 2, nb * D_B), lambda g, i: (0, g))
    else:
        top_spec = pl.BlockSpec((tr, D_B), lambda g, i: (jnp.minimum(i, nh - 1), g))
        bot_spec = pl.BlockSpec((tr, D_B), lambda g, i: (jnp.maximum(i - nh, 0), g))
    return pl.pallas_call(
        functools.partial(_out_kernel, tiles_per_half=nh),
        out_shape=jax.ShapeDtypeStruct((B, L, D_MODEL), _F32),
        grid=(B // nb, L // tr),
        in_specs=[
            row_spec(D_MODEL), row_spec(D_A), row_spec(D_A), top_spec, bot_spec, row_spec(D_C),
            _mod_spec(0, mod_row, mod_idx),
            _layer_spec((A_HEADS * CHUNK, CHUNK), layer),
            _layer_spec((CHUNK, D_A), layer),
            _layer_spec((D_B, D_B), layer),
            _const_spec((D_MODEL, D_MODEL)),
            _layer_spec((1, D_MODEL), layer),
            _layer_spec((1, D_MODEL), layer),
        ],
        out_specs=row_spec(D_MODEL),
        compiler_params=_params(("parallel", "parallel")),
        name="out_proj",
    )(x, u, vn, four_top, four_bot, attn, mod, ws, bs, fw, w_out, ln_g, ln_b)


def _ffn_kernel(x_ref, xp_ref, xn_ref, mod_ref, wgu_ref, cw_ref, cb_ref, wd_ref, g_ref, b_ref, *refs,
                seq_len, next_mod):
    if next_mod:
        c_ref, wa_ref, ba_ref, o_ref, nm_ref, hcat_ref, act_ref = refs
        _mod_kernel(c_ref, wa_ref, ba_ref, nm_ref)
    else:
        o_ref, hcat_ref, act_ref = refs
    tm = x_ref.shape[0]
    ext = tm + 2 * HALO
    inside_one_seq = seq_len % tm == 0
    m = mod_ref[...]

    def pre_norm(x):
        return (_norm(x) * (1.0 + m[4:5]) + m[3:4]).astype(_BF16)

    h_prev = pre_norm(xp_ref[...])
    h_next = pre_norm(xn_ref[...])
    if inside_one_seq:
        t = pl.program_id(0) % (seq_len // tm)
        hcat_ref[:HALO, :] = jnp.where(t > 0, h_prev, jnp.zeros_like(h_prev))
        hcat_ref[HALO + tm:, :] = jnp.where(t < seq_len // tm - 1, h_next, jnp.zeros_like(h_next))
    else:
        assert seq_len & (seq_len - 1) == 0
        hcat_ref[:HALO, :] = h_prev
        hcat_ref[HALO + tm:, :] = h_next
        pos = lax.broadcasted_iota(jnp.int32, (tm, 1), 0) & (seq_len - 1)
        has_prev = pos != 0
        has_next = pos != seq_len - 1
    for s in range(tm // FFN_SUB_ROWS):
        sub = slice(s * FFN_SUB_ROWS, (s + 1) * FFN_SUB_ROWS)
        hcat_ref[HALO + s * FFN_SUB_ROWS:HALO + (s + 1) * FFN_SUB_ROWS, :] = pre_norm(x_ref[sub, :])
    for j in range(D_FF // FF_CHUNK):
        cols = slice(j * FF_CHUNK, (j + 1) * FF_CHUNK)
        up_cols = slice(D_FF + j * FF_CHUNK, D_FF + (j + 1) * FF_CHUNK)
        g = _dot(hcat_ref[...], wgu_ref[:, cols])
        g_prev = pltpu.roll(g, 1, 0)[HALO:HALO + tm]
        g_next = pltpu.roll(g, ext - 1, 0)[HALO:HALO + tm]
        if not inside_one_seq:
            g_prev = jnp.where(has_prev, g_prev, 0.0)
            g_next = jnp.where(has_next, g_next, 0.0)
        gc = (cb_ref[:, cols] + g_prev * cw_ref[0:1, cols] + g[HALO:HALO + tm] * cw_ref[1:2, cols]
              + g_next * cw_ref[2:3, cols])
        up = _dot(hcat_ref[HALO:HALO + tm, :], wgu_ref[:, up_cols])
        act_ref[:, cols] = ((gc * jax.nn.sigmoid(gc)) * up).astype(_BF16)
    for s in range(tm // FFN_SUB_ROWS):
        sub = slice(s * FFN_SUB_ROWS, (s + 1) * FFN_SUB_ROWS)
        y = _dot(act_ref[sub, :], wd_ref[...])
        o_ref[sub, :] = _norm(ALPHA * x_ref[sub, :] + m[5:6] * y) * g_ref[...] + b_ref[...]


def _conv_ffn(x1, mod, w_gu, conv_w, conv_b, w_down, ln_g, ln_b, next_mod=None, *, layer, mod_row,
              per_seq_mod):
    B, L, _ = x1.shape
    n_tok = B * L
    tm = FFN_ROW_TILE
    assert n_tok % tm == 0 and (L % tm == 0 or (tm % L == 0 and not per_seq_mod))
    x1 = x1.reshape(n_tok, D_MODEL)
    row_spec = pl.BlockSpec((tm, D_MODEL), lambda t: (t, 0))
    per = tm // HALO
    last = n_tok // HALO - 1
    prev_spec = pl.BlockSpec((HALO, D_MODEL), lambda t: (jnp.maximum(t * per - 1, 0), 0))
    next_spec = pl.BlockSpec((HALO, D_MODEL), lambda t: (jnp.minimum((t + 1) * per, last), 0))
    mod_idx = (lambda l, r: (lambda t: (l, r + (t * tm) // L, 0, 0))) if per_seq_mod else None
    out_shape = jax.ShapeDtypeStruct((n_tok, D_MODEL), _F32)
    out_specs = row_spec
    side_specs, side_args = [], []
    if next_mod is not None:
        cs = next_mod[0]
        n_steps = n_tok // tm
        tn = 6 * D_MODEL // n_steps
        assert tn % LANES == 0
        side_specs = [pl.BlockSpec(cs.shape, lambda t: (0, 0)),
                      pl.BlockSpec((1, D_MODEL, tn), lambda t: (layer + 1, 0, t)),
                      pl.BlockSpec((1, 1, tn), lambda t: (layer + 1, 0, t))]
        side_args = list(next_mod)
        out_shape = (out_shape, jax.ShapeDtypeStruct((1, cs.shape[0], 6 * D_MODEL), _F32))
        out_specs = (row_spec, pl.BlockSpec((1, cs.shape[0], tn), lambda t: (0, 0, t)))
    out = pl.pallas_call(
        functools.partial(_ffn_kernel, seq_len=L, next_mod=next_mod is not None),
        out_shape=out_shape,
        grid=(n_tok // tm,),
        in_specs=[
            row_spec, prev_spec, next_spec,
            _mod_spec(0, mod_row, mod_idx),
            _const_spec((D_MODEL, 2 * D_FF)),
            _layer_spec((3, D_FF), layer),
            _layer_spec((1, D_FF), layer),
            _const_spec((D_FF, D_MODEL)),
            _layer_spec((1, D_MODEL), layer),
            _layer_spec((1, D_MODEL), layer),
            *side_specs,
        ],
        out_specs=out_specs,
        scratch_shapes=[pltpu.VMEM((tm + 2 * HALO, D_MODEL), _BF16), pltpu.VMEM((tm, D_FF), _BF16)],
        compiler_params=_params(("parallel",)),
        name="conv_ffn",
    )(x1, x1, x1, mod, w_gu, conv_w, conv_b, w_down, ln_g, ln_b, *side_args)
    if next_mod is not None:
        return out[0].reshape(B, L, D_MODEL), out[1]
    return out.reshape(B, L, D_MODEL)


def _dft_tables(n):
    idx = np.arange(n)
    ang = 2.0 * np.pi * ((idx[:, None] * idx[None, :]) % n) / n
    return np.cos(ang) / np.sqrt(n), np.sin(ang) / np.sqrt(n)


def _pos_dft_tables(n):
    cos, sin = _dft_tables(n)
    ti = _dft_tile_rows(n)
    tiles = lambda tab: np.stack([tab[t * ti:t * ti + ti + HALO] for t in range((n // 2) // ti)])
    rev = np.zeros((ti, ti + HALO))
    rev[np.arange(ti), ti - np.arange(ti)] = 1.0
    return tiles(cos), tiles(sin), rev


def _block_diag(blocks):
    n = len(blocks)
    d = blocks[0].shape[0]
    out = np.zeros((n * d, n * d), blocks[0].dtype)
    for g, blk in enumerate(blocks):
        out[g * d:(g + 1) * d, g * d:(g + 1) * d] = blk
    return out


def kernel(x_prompt, x_sample, cache_k, cache_v, c, c_ctx, w_ada, b_ada, w_in, a_ws, a_bs, f_w, rpb,
           w_out, ln1_g, ln1_b, w_gu, conv_w, conv_b, w_down, ln2_g, ln2_b):
    n_ctx_b, ctx_len, _ = x_prompt.shape
    n_lat_b, lat_len, _ = x_sample.shape
    past_len = cache_k.shape[2]

    cs = jnp.zeros((MOD_ROWS, D_MODEL), _F32).at[0].set(c_ctx).at[1:1 + n_lat_b].set(c)
    b_ada = b_ada.reshape(DEPTH, 1, 6 * D_MODEL)
    mod = _modulation(cs, w_ada, b_ada).reshape(1, MOD_ROWS, 6, D_MODEL)

    cc, sc = _dft_tables(B_GROUP_DIM)
    ch_cos = jnp.asarray(_block_diag([cc] * B_GROUPS), _F32).astype(_BF16)
    ch_sin = jnp.asarray(_block_diag([sc] * B_GROUPS), _F32).astype(_BF16)
    pos_tabs = {n: tuple(jnp.asarray(t, _F32).astype(_BF16) for t in _pos_dft_tables(n))
                for n in (ctx_len, lat_len)}

    w_in_l = w_in[0].astype(_BF16)
    ws = a_ws.reshape(DEPTH, A_HEADS * CHUNK, CHUNK).astype(_BF16)
    bs = jnp.repeat(a_bs.transpose(0, 2, 1), A_HEAD_DIM, axis=2)
    group_eye = jnp.eye(B_GROUPS, dtype=_F32)[None, :, None, :, None]
    fw = (f_w[:, :, :, None, :] * group_eye).reshape(DEPTH, D_B, D_B).astype(_BF16)
    vec = lambda p: p.reshape(DEPTH, 1, p.shape[-1])
    g1, b1, g2, b2, cb = vec(ln1_g), vec(ln1_b), vec(ln2_g), vec(ln2_b), vec(conv_b)
    bias = _na_bias_table(rpb)
    heads_first = (0, 1, 3, 4, 2)
    kct = cache_k.transpose(heads_first).reshape(n_lat_b, DEPTH, D_C, past_len)
    vct = cache_v.transpose(heads_first).reshape(n_lat_b, DEPTH, D_C, past_len)

    xp, xs = x_prompt, x_sample
    new_k, new_v = [], []
    for l in range(DEPTH):
        ctx = dict(layer=l, mod_row=0, per_seq_mod=False)
        lat = dict(layer=l, mod_row=1, per_seq_mod=True)
        cast = [(w_gu, l), (w_down, l), (w_out, l)] + ([(w_in, l + 1)] if l + 1 < DEPTH else [])
        lat_out = _in_proj(xs, mod, w_in_l, ch_cos, ch_sin, cast, kv_transposed=False, **lat)
        lat_proj, (w_gu_l, w_down_l, w_out_l) = lat_out[:7], lat_out[7:10]
        ctx_proj = _in_proj(xp, mod, w_in_l, ch_cos, ch_sin,
                            prev_kv=list(zip(new_k, new_v)) if l == DEPTH - 1 else None,
                            kv_transposed=True, **ctx)

        def mix_and_ffn(x, proj, attend, where, next_mod=None):
            u, vn, zc, zs, q, k, v = proj
            four_top, four_bot = _pos_dft(*pos_tabs[x.shape[1]], zc, zs)
            attn = attend(q, k, v)
            x1 = _out_proj(x, u, vn, four_top, four_bot, attn, mod, ws, bs, fw, w_out_l, g1, b1, **where)
            return _conv_ffn(x1, mod, w_gu_l, conv_w, cb, w_down_l, g2, b2, next_mod, **where)

        new_k.append(ctx_proj[5])
        new_v.append(ctx_proj[6])
        xp = mix_and_ffn(xp, ctx_proj, _ctx_attention, ctx)
        na_attend = functools.partial(_na_attention, kct=kct, vct=vct, bias=bias, layer=l)
        if l + 1 < DEPTH:
            xs, next_mod = mix_and_ffn(xs, lat_proj, na_attend, lat, (cs, w_ada, b_ada))
            mod = next_mod.reshape(1, MOD_ROWS, 6, D_MODEL)
            w_in_l = lat_out[10]
        else:
            xs = mix_and_ffn(xs, lat_proj, na_attend, lat)

    def kv_out(stacked):
        return stacked.reshape(n_ctx_b, DEPTH, C_HEADS, C_HEAD_DIM, ctx_len).transpose(0, 1, 4, 2, 3)

    return (xp, xs, kv_out(new_k[-1]), kv_out(new_v[-1]))
```

```python
import functools

import numpy as np
import jax
import jax.numpy as jnp
from jax import lax
from jax.experimental import pallas as pl
from jax.experimental.pallas import tpu as pltpu

D_MODEL = 1024
DEPTH = 2
GRID_W = 64
D_A = 256
A_HEADS = 4
A_HEAD_DIM = 64
CHUNK = 128
D_B = 256
B_GROUPS = 4
B_GROUP_DIM = 64
D_C = 512
C_HEAD_DIM = 64
C_HEADS = 8
P_IN = 2 * D_A + D_B + 3 * D_C
NA_ROWS = 8
NA_KW = 16
D_FF = 2816
ALPHA = (2 * DEPTH) ** 0.25
LN_EPS = 1e-6
LOG2_E = 1.4426950408889634
Q_SCALE = C_HEAD_DIM ** -0.5 * LOG2_E

LANES = 128
V7X_VMEM_BYTES = 64 * 1024 * 1024
VMEM_LIMIT = V7X_VMEM_BYTES * 7 // 8
HEAD_PAIR = 2 * C_HEAD_DIM
N_HEAD_PAIRS = C_HEADS // 2
MOD_ROWS = 16
MOD_COL_TILE = 3072
CTX_ATTN_SEQS = 4
IN_ROW_TILE = 1024
IN_SUB_ROWS = 256
OUT_ROW_TILE = 1024
OUT_SUB_ROWS = 256
FFN_ROW_TILE = 1024
FFN_SUB_ROWS = 256
FF_CHUNK = 256
HALO = 16
DFT_ROW_TILE = 512
DFT_COL_TILE = 1024

_F32 = jnp.float32
_BF16 = jnp.bfloat16


def _dot(a, b):
    return jnp.dot(a, b, preferred_element_type=_F32)


def _dot_nt(a, b):
    return lax.dot_general(a, b, (((1,), (1,)), ((), ())), preferred_element_type=_F32)


def _norm(x):
    mu = jnp.mean(x, axis=-1, keepdims=True)
    xc = x - mu
    var = jnp.mean(xc * xc, axis=-1, keepdims=True)
    return xc * lax.rsqrt(var + LN_EPS)


def _params(sem):
    return pltpu.CompilerParams(dimension_semantics=sem, vmem_limit_bytes=VMEM_LIMIT)


def _const_spec(shape):
    nd = len(shape)
    return pl.BlockSpec(shape, lambda *_: (0,) * nd, pipeline_mode=pl.Buffered(1))


def _layer_spec(shape, layer):
    nd = len(shape)
    return pl.BlockSpec((None,) + shape, lambda *_: (layer,) + (0,) * nd, pipeline_mode=pl.Buffered(1))


def _mod_spec(layer, first_row, rows_per_seq):
    if rows_per_seq is None:
        return pl.BlockSpec((None, None, 6, D_MODEL), lambda *_: (layer, first_row, 0, 0))
    return pl.BlockSpec((None, None, 6, D_MODEL), rows_per_seq(layer, first_row))


def _mod_kernel(c_ref, w_ref, b_ref, o_ref):
    c = c_ref[...]
    s = c * jax.nn.sigmoid(c)
    o_ref[0] = _dot(s.astype(_BF16), w_ref[0].astype(_BF16)) + b_ref[0]


def _modulation(cs, w_ada, b_ada):
    n = cs.shape[0]
    tn = MOD_COL_TILE
    return pl.pallas_call(
        _mod_kernel,
        out_shape=jax.ShapeDtypeStruct((DEPTH, n, 6 * D_MODEL), _F32),
        grid=(DEPTH, 6 * D_MODEL // tn),
        in_specs=[
            pl.BlockSpec((n, D_MODEL), lambda l, j: (0, 0)),
            pl.BlockSpec((1, D_MODEL, tn), lambda l, j: (l, 0, j)),
            pl.BlockSpec((1, 1, tn), lambda l, j: (l, 0, j)),
        ],
        out_specs=pl.BlockSpec((1, n, tn), lambda l, j: (l, 0, j)),
        compiler_params=_params(("parallel", "parallel")),
        name="adaln_mod",
    )(cs, w_ada, b_ada.reshape(DEPTH, 1, 6 * D_MODEL))


def _tile_split(B, L, tile_rows):
    if L >= tile_rows:
        assert L % tile_rows == 0
        return 1, tile_rows
    nb = min(tile_rows // L, B)
    assert B % nb == 0
    return nb, L


def _in_kernel(x_ref, mod_ref, w_ref, cc_ref, cs_ref, *refs, kv_transposed, n_cast, kv_slot):
    n_prev = 0 if kv_slot is None else 2 * kv_slot
    cast_in, prev_kv = refs[:n_cast], refs[n_cast:n_cast + n_prev]
    u_ref, vn_ref, zc_ref, zs_ref, q_ref, k_ref, v_ref = refs[n_cast + n_prev:n_cast + n_prev + 7]
    cast_out = refs[n_cast + n_prev + 7:]
    for src, dst in zip(cast_in, cast_out):
        dst[...] = src[...].astype(dst.dtype)
    for j in range(n_prev // 2):
        k_ref[:, j] = prev_kv[2 * j][...]
        v_ref[:, j] = prev_kv[2 * j + 1][...]
    nb, tr, _ = x_ref.shape
    m = mod_ref[...]
    o1 = 2 * D_A
    o2 = o1 + D_B
    for t in range(nb * tr // IN_SUB_ROWS):
        seq = (t * IN_SUB_ROWS) // tr
        rows = slice((t * IN_SUB_ROWS) % tr, (t * IN_SUB_ROWS) % tr + IN_SUB_ROWS)
        h = _norm(x_ref[seq, rows, :]) * (1.0 + m[1:2]) + m[0:1]
        z = _dot(h.astype(_BF16), w_ref[...])
        za = jax.nn.gelu(z[:, :o1])
        u_ref[seq, rows, :] = za[:, :D_A]
        vn_ref[seq, rows, :] = _norm(za[:, D_A:]).astype(_BF16)
        zb = z[:, o1:o2].astype(_BF16)
        zc_ref[rows, seq * D_B:(seq + 1) * D_B] = _dot(zb, cc_ref[...]).astype(_BF16)
        zs_ref[rows, seq * D_B:(seq + 1) * D_B] = _dot(zb, cs_ref[...]).astype(_BF16)
        q_ref[seq, rows, :] = (z[:, o2:o2 + D_C] * Q_SCALE).astype(q_ref.dtype)
        k = z[:, o2 + D_C:o2 + 2 * D_C]
        v = z[:, o2 + 2 * D_C:o2 + 3 * D_C]
        if kv_transposed and kv_slot is not None:
            k_ref[seq, kv_slot, :, rows] = k.T.astype(k_ref.dtype)
            v_ref[seq, kv_slot, :, rows] = v.T.astype(v_ref.dtype)
        elif kv_transposed:
            k_ref[seq, :, rows] = k.T.astype(k_ref.dtype)
            v_ref[seq, :, rows] = v.T.astype(v_ref.dtype)
        else:
            k_ref[seq, rows, :] = k.astype(k_ref.dtype)
            v_ref[seq, rows, :] = v.astype(v_ref.dtype)


def _in_proj(x, mod, w_in, ch_cos, ch_sin, cast=(), prev_kv=None, *, layer, mod_row, per_seq_mod,
             kv_transposed):
    B, L, _ = x.shape
    nb, tr = _tile_split(B, L, IN_ROW_TILE)
    assert nb == 1 or not per_seq_mod
    n_i = L // tr
    n_steps = (B // nb) * n_i
    cast_in_specs, cast_out_specs, cast_shapes = [], [], []
    for p, p_layer in cast:
        _, R, C = p.shape
        assert R % (n_steps * HALO) == 0
        cast_in_specs.append(pl.BlockSpec((None, R // n_steps, C),
                                          functools.partial(lambda pl_, g, i: (pl_, g * n_i + i, 0), p_layer)))
        cast_out_specs.append(pl.BlockSpec((R // n_steps, C), lambda g, i: (g * n_i + i, 0)))
        cast_shapes.append(jax.ShapeDtypeStruct((R, C), _BF16))
    row = lambda w, dt: jax.ShapeDtypeStruct((B, L, w), dt)
    row_spec = lambda w: pl.BlockSpec((nb, tr, w), lambda g, i: (g, i, 0))
    four = jax.ShapeDtypeStruct((L, B * D_B), _BF16)
    four_spec = pl.BlockSpec((tr, nb * D_B), lambda g, i: (i, g))
    mod_idx = (lambda l, r: (lambda g, i: (l, r + g, 0, 0))) if per_seq_mod else None
    prev_specs, prev_args, kv_slot = [], [], None
    if kv_transposed and prev_kv is not None:
        kv_slot = len(prev_kv)
        kv = jax.ShapeDtypeStruct((B, kv_slot + 1, D_C, L), _F32)
        kv_spec = pl.BlockSpec((nb, kv_slot + 1, D_C, tr), lambda g, i: (g, 0, 0, i))
        for pair in prev_kv:
            prev_specs += [pl.BlockSpec((nb, D_C, tr), lambda g, i: (g, 0, i))] * 2
            prev_args += list(pair)
    elif kv_transposed:
        kv = jax.ShapeDtypeStruct((B, D_C, L), _F32)
        kv_spec = pl.BlockSpec((nb, D_C, tr), lambda g, i: (g, 0, i))
    else:
        kv = row(D_C, _BF16)
        kv_spec = row_spec(D_C)
    return pl.pallas_call(
        functools.partial(_in_kernel, kv_transposed=kv_transposed, n_cast=len(cast), kv_slot=kv_slot),
        out_shape=(row(D_A, _F32), row(D_A, _BF16), four, four, row(D_C, _BF16), kv, kv, *cast_shapes),
        grid=(B // nb, n_i),
        in_specs=[
            row_spec(D_MODEL),
            _mod_spec(layer, mod_row, mod_idx),
            _const_spec((D_MODEL, P_IN)),
            _const_spec((D_B, D_B)),
            _const_spec((D_B, D_B)),
            *cast_in_specs, *prev_specs,
        ],
        out_specs=(row_spec(D_A), row_spec(D_A), four_spec, four_spec, row_spec(D_C), kv_spec, kv_spec,
                   *cast_out_specs),
        compiler_params=_params(("parallel", "parallel")),
        name="in_proj",
    )(x, mod, w_in, ch_cos, ch_sin, *[p for p, _ in cast], *prev_args)


def _dft_kernel(c_ref, s_ref, zc_ref, zs_ref, rev_ref, top_ref, bot_ref):
    p = _dot(c_ref[0], zc_ref[...])
    q = _dot(s_ref[0], zs_ref[...])
    top_ref[...] = (p - q)[:top_ref.shape[0]].astype(top_ref.dtype)
    bot_ref[...] = _dot(rev_ref[...], (p + q).astype(_BF16)).astype(bot_ref.dtype)


def _dft_tile_rows(L):
    return min(L // 2, DFT_ROW_TILE)


def _pos_dft(pos_cos, pos_sin, rev, zc, zs):
    L, N = zc.shape
    ti = _dft_tile_rows(L)
    n_t = (L // 2) // ti
    tj = DFT_COL_TILE
    half = jax.ShapeDtypeStruct((L // 2, N), _BF16)
    tab_spec = pl.BlockSpec((1, ti + HALO, L), lambda t, j: (t, 0, 0))
    return pl.pallas_call(
        _dft_kernel,
        out_shape=(half, half),
        grid=(n_t, N // tj),
        in_specs=[
            tab_spec, tab_spec,
            pl.BlockSpec((L, tj), lambda t, j: (0, j)),
            pl.BlockSpec((L, tj), lambda t, j: (0, j)),
            pl.BlockSpec((ti, ti + HALO), lambda t, j: (0, 0)),
        ],
        out_specs=(pl.BlockSpec((ti, tj), lambda t, j: (t, j)),
                   pl.BlockSpec((ti, tj), lambda t, j: (n_t - 1 - t, j))),
        compiler_params=_params(("parallel", "parallel")),
        name="pos_dft",
    )(pos_cos, pos_sin, zc, zs, rev)


def _head_lane_mask(rows, head):
    lane = lax.broadcasted_iota(jnp.int32, (rows, HEAD_PAIR), 1)
    return (lane >= head * C_HEAD_DIM) & (lane < (head + 1) * C_HEAD_DIM)


def _ctx_attn_kernel(q_ref, kt_ref, vt_ref, o_ref):
    n_seq, rows, _ = q_ref.shape
    sel0 = _head_lane_mask(rows, 0)
    for seq in range(n_seq):
        for pair in range(N_HEAD_PAIRS):
            lanes = slice(pair * HEAD_PAIR, (pair + 1) * HEAD_PAIR)
            q = q_ref[seq, :, lanes]
            kt = kt_ref[seq, lanes, :].astype(_BF16)
            vt = vt_ref[seq, lanes, :].astype(_BF16)
            lhs = jnp.concatenate([jnp.where(sel0, q, jnp.zeros_like(q)),
                                   jnp.where(sel0, jnp.zeros_like(q), q)], axis=0)
            s = _dot(lhs, kt)
            p = jnp.exp2(s - jnp.max(s, axis=-1, keepdims=True))
            o = _dot_nt(p.astype(_BF16), vt) / jnp.sum(p, axis=-1, keepdims=True)
            o_ref[seq, :, lanes] = jnp.where(sel0, o[:rows], o[rows:]).astype(o_ref.dtype)


def _ctx_attention(q, kt, vt):
    B, L, _ = q.shape
    nb = min(CTX_ATTN_SEQS, B)
    assert B % nb == 0
    spec = pl.BlockSpec((nb, L, D_C), lambda b: (b, 0, 0))
    if kt.ndim == 4:
        slot = kt.shape[1] - 1
        spec_t = pl.BlockSpec((nb, None, D_C, L), lambda b: (b, slot, 0, 0))
    else:
        spec_t = pl.BlockSpec((nb, D_C, L), lambda b: (b, 0, 0))
    return pl.pallas_call(
        _ctx_attn_kernel,
        out_shape=jax.ShapeDtypeStruct((B, L, D_C), _BF16),
        grid=(B // nb,),
        in_specs=[spec, spec_t, spec_t],
        out_specs=spec,
        compiler_params=_params(("parallel",)),
        name="ctx_attention",
    )(q, kt, vt)


def _na_kernel(q_ref, k_ref, v_ref, kct_ref, vct_ref, bias_ref, o_ref, *, n_rows):
    kct = kct_ref[...].astype(_BF16)
    vct = vct_ref[...].astype(_BF16)
    win_keys = NA_ROWS * GRID_W
    pair_rows = 2 * GRID_W
    n_q = n_rows * GRID_W

    windows = []
    for r in range(n_rows):
        rs = min(max(r - NA_ROWS // 2, 0), n_rows - NA_ROWS)
        windows.append((slice(rs * GRID_W, rs * GRID_W + win_keys), rs - r + NA_ROWS - 1))

    q = q_ref[0]
    sel0 = _head_lane_mask(n_q, 0)
    q_heads = (jnp.where(sel0, q, jnp.zeros_like(q)), jnp.where(sel0, jnp.zeros_like(q), q))
    lhs = jnp.concatenate([qh[r * GRID_W:(r + 1) * GRID_W]
                           for r in range(n_rows) for qh in q_heads], axis=0)
    s_ctx = _dot(lhs, kct)
    s_loc = []
    for r, (win, base) in enumerate(windows):
        bias = jnp.concatenate(
            [jnp.concatenate([bias_ref[head, base + 2 * t] for t in range(NA_ROWS // 2)], axis=1)
             for head in range(2)], axis=0)
        s_loc.append(_dot_nt(lhs[r * pair_rows:(r + 1) * pair_rows], k_ref[0, win, :]) + bias)
    s_loc = jnp.concatenate(s_loc, axis=0)
    m = jnp.maximum(jnp.max(s_loc, axis=-1, keepdims=True), jnp.max(s_ctx, axis=-1, keepdims=True))
    p_loc = jnp.exp2(s_loc - m)
    p_ctx = jnp.exp2(s_ctx - m)
    denom = jnp.sum(p_loc, axis=-1, keepdims=True) + jnp.sum(p_ctx, axis=-1, keepdims=True)
    p_loc = p_loc.astype(_BF16)
    o_loc = jnp.concatenate(
        [_dot(p_loc[r * pair_rows:(r + 1) * pair_rows], v_ref[0, win, :])
         for r, (win, _) in enumerate(windows)], axis=0)
    o = (o_loc + _dot_nt(p_ctx.astype(_BF16), vct)) / denom
    row_sel0 = _head_lane_mask(GRID_W, 0)
    o_ref[0] = jnp.concatenate(
        [jnp.where(row_sel0, o[r * pair_rows:r * pair_rows + GRID_W],
                   o[r * pair_rows + GRID_W:(r + 1) * pair_rows])
         for r in range(n_rows)], axis=0).astype(o_ref.dtype)


def _na_bias_kernel(v_ref, o_ref):
    n_ri = 2 * NA_ROWS - 1
    qcol = lax.broadcasted_iota(jnp.int32, (GRID_W, LANES), 0)
    lane = lax.broadcasted_iota(jnp.int32, (GRID_W, LANES), 1)
    kcol = lane & (GRID_W - 1)
    start = jnp.clip(qcol - NA_KW // 2, 0, GRID_W - NA_KW)
    in_win = (kcol >= start) & (kcol < start + NA_KW)
    for head in range(v_ref.shape[0]):
        rows = [jnp.broadcast_to(v_ref[head, ri], (GRID_W, LANES)) for ri in range(n_ri)]
        for ri in range(n_ri - 1):
            left = pltpu.roll(rows[ri], GRID_W + 1, 1, stride=1, stride_axis=0)
            right = pltpu.roll(rows[ri + 1], 1, 1, stride=1, stride_axis=0)
            o_ref[head, ri] = jnp.where(in_win, jnp.where(lane < GRID_W, left, right) * LOG2_E, -jnp.inf)


def _na_bias_table(rpb):
    n_ri = 2 * NA_ROWS - 1
    lo = GRID_W - NA_KW
    v = jnp.pad(rpb, ((0, 0), (0, 0), (0, 0), (lo, LANES - lo - (2 * NA_KW - 1))))
    v = v.reshape(DEPTH * C_HEADS, n_ri, 1, LANES)
    return pl.pallas_call(
        _na_bias_kernel,
        out_shape=jax.ShapeDtypeStruct((DEPTH * C_HEADS, n_ri - 1, GRID_W, LANES), _F32),
        grid=(DEPTH,),
        in_specs=[pl.BlockSpec((C_HEADS, n_ri, 1, LANES), lambda i: (i, 0, 0, 0))],
        out_specs=pl.BlockSpec((C_HEADS, n_ri - 1, GRID_W, LANES), lambda i: (i, 0, 0, 0)),
        compiler_params=_params(("parallel",)),
        name="na_bias",
    )(v)


def _na_attention(q, k, v, kct, vct, bias, *, layer):
    B, L, _ = q.shape
    Lc = kct.shape[3]
    n_rows = L // GRID_W
    assert n_rows >= NA_ROWS and 2 * GRID_W == LANES
    lat = pl.BlockSpec((1, L, HEAD_PAIR), lambda p, b: (b, 0, p))
    ctx = pl.BlockSpec((None, None, HEAD_PAIR, Lc), lambda p, b: (b, layer, p, 0))
    return pl.pallas_call(
        functools.partial(_na_kernel, n_rows=n_rows),
        out_shape=jax.ShapeDtypeStruct((B, L, D_C), _BF16),
        grid=(N_HEAD_PAIRS, B),
        in_specs=[lat, lat, lat, ctx, ctx,
                  pl.BlockSpec((2, 2 * NA_ROWS - 2, GRID_W, LANES),
                               lambda p, b: (layer * N_HEAD_PAIRS + p, 0, 0, 0))],
        out_specs=lat,
        compiler_params=_params(("parallel", "parallel")),
        name="na_attention",
    )(q, k, v, kct, vct, bias)


def _out_kernel(x_ref, u_ref, vn_ref, ft_ref, fb_ref, a_ref, mod_ref, ws_ref, bs_ref, fw_ref, wo_ref,
                g_ref, b_ref, x1_ref, *, tiles_per_half):
    nb, tr, _ = x_ref.shape
    lane = lax.broadcasted_iota(jnp.int32, (CHUNK, D_A), 1)
    m = mod_ref[...]
    if tiles_per_half == 0:
        four = jnp.concatenate([half[:, s * D_B:(s + 1) * D_B]
                                for s in range(nb) for half in (ft_ref, fb_ref)], axis=0)
    else:
        four = jnp.where(pl.program_id(1) < tiles_per_half, ft_ref[...], fb_ref[...])

    def rows_of(ref, start, n):
        return ref[start // tr, start % tr:start % tr + n, :]

    for s in range(nb * tr // OUT_SUB_ROWS):
        sub = slice(s * OUT_SUB_ROWS, (s + 1) * OUT_SUB_ROWS)
        ya = []
        for n in range(OUT_SUB_ROWS // CHUNK):
            start = s * OUT_SUB_ROWS + n * CHUNK
            r = _dot(ws_ref[...], rows_of(vn_ref, start, CHUNK))
            mixed = r[:CHUNK]
            for hd in range(1, A_HEADS):
                mixed = jnp.where(lane >= hd * A_HEAD_DIM, r[hd * CHUNK:(hd + 1) * CHUNK], mixed)
            ya.append(rows_of(u_ref, start, CHUNK) * (mixed + bs_ref[...]))
        ya = jnp.concatenate(ya, axis=0).astype(_BF16)
        yb = _dot(four[sub], fw_ref[...]).astype(_BF16)
        y = _dot(jnp.concatenate([ya, yb, rows_of(a_ref, s * OUT_SUB_ROWS, OUT_SUB_ROWS)], axis=1), wo_ref[...])
        x1 = (_norm(ALPHA * rows_of(x_ref, s * OUT_SUB_ROWS, OUT_SUB_ROWS) + m[2:3] * y) * g_ref[...]
              + b_ref[...])
        off = (s * OUT_SUB_ROWS) % tr
        x1_ref[(s * OUT_SUB_ROWS) // tr, off:off + OUT_SUB_ROWS, :] = x1


def _out_proj(x, u, vn, four_top, four_bot, attn, mod, ws, bs, fw, w_out, ln_g, ln_b, *,
              layer, mod_row, per_seq_mod):
    B, L, _ = x.shape
    nb, tr = _tile_split(B, L, OUT_ROW_TILE)
    assert (nb == 1 or not per_seq_mod) and tr % OUT_SUB_ROWS == 0
    row_spec = lambda w: pl.BlockSpec((nb, tr, w), lambda g, i: (g, i, 0))
    mod_idx = (lambda l, r: (lambda g, i: (l, r + g, 0, 0))) if per_seq_mod else None
    nh = (L // 2) // tr
    assert tr == L or (L // 2) % tr == 0
    if nh == 0:
        top_spec = bot_spec = pl.BlockSpec((L // 2, nb * D_B), lambda g, i: (0, g))
    else:
        top_spec = pl.BlockSpec((tr, D_B), lambda g, i: (jnp.minimum(i, nh - 1), g))
        bot_spec = pl.BlockSpec((tr, D_B), lambda g, i: (jnp.maximum(i - nh, 0), g))
    return pl.pallas_call(
        functools.partial(_out_kernel, tiles_per_half=nh),
        out_shape=jax.ShapeDtypeStruct((B, L, D_MODEL), _F32),
        grid=(B // nb, L // tr),
        in_specs=[
            row_spec(D_MODEL), row_spec(D_A), row_spec(D_A), top_spec, bot_spec, row_spec(D_C),
            _mod_spec(layer, mod_row, mod_idx),
            _layer_spec((A_HEADS * CHUNK, CHUNK), layer),
            _layer_spec((CHUNK, D_A), layer),
            _layer_spec((D_B, D_B), layer),
            _const_spec((D_MODEL, D_MODEL)),
            _layer_spec((1, D_MODEL), layer),
            _layer_spec((1, D_MODEL), layer),
        ],
        out_specs=row_spec(D_MODEL),
        compiler_params=_params(("parallel", "parallel")),
        name="out_proj",
    )(x, u, vn, four_top, four_bot, attn, mod, ws, bs, fw, w_out, ln_g, ln_b)


def _ffn_kernel(x_ref, xp_ref, xn_ref, mod_ref, wgu_ref, cw_ref, cb_ref, wd_ref,
                g_ref, b_ref, o_ref, hcat_ref, act_ref, *, seq_len):
    tm = x_ref.shape[0]
    ext = tm + 2 * HALO
    inside_one_seq = seq_len % tm == 0
    m = mod_ref[...]

    def pre_norm(x):
        return (_norm(x) * (1.0 + m[4:5]) + m[3:4]).astype(_BF16)

    h_prev = pre_norm(xp_ref[...])
    h_next = pre_norm(xn_ref[...])
    if inside_one_seq:
        t = pl.program_id(0) % (seq_len // tm)
        hcat_ref[:HALO, :] = jnp.where(t > 0, h_prev, jnp.zeros_like(h_prev))
        hcat_ref[HALO + tm:, :] = jnp.where(t < seq_len // tm - 1, h_next, jnp.zeros_like(h_next))
    else:
        assert seq_len & (seq_len - 1) == 0
        hcat_ref[:HALO, :] = h_prev
        hcat_ref[HALO + tm:, :] = h_next
        pos = lax.broadcasted_iota(jnp.int32, (tm, 1), 0) & (seq_len - 1)
        has_prev = pos != 0
        has_next = pos != seq_len - 1
    for s in range(tm // FFN_SUB_ROWS):
        sub = slice(s * FFN_SUB_ROWS, (s + 1) * FFN_SUB_ROWS)
        hcat_ref[HALO + s * FFN_SUB_ROWS:HALO + (s + 1) * FFN_SUB_ROWS, :] = pre_norm(x_ref[sub, :])
    for j in range(D_FF // FF_CHUNK):
        cols = slice(j * FF_CHUNK, (j + 1) * FF_CHUNK)
        up_cols = slice(D_FF + j * FF_CHUNK, D_FF + (j + 1) * FF_CHUNK)
        g = _dot(hcat_ref[...], wgu_ref[:, cols])
        g_prev = pltpu.roll(g, 1, 0)[HALO:HALO + tm]
        g_next = pltpu.roll(g, ext - 1, 0)[HALO:HALO + tm]
        if not inside_one_seq:
            g_prev = jnp.where(has_prev, g_prev, 0.0)
            g_next = jnp.where(has_next, g_next, 0.0)
        gc = (cb_ref[:, cols] + g_prev * cw_ref[0:1, cols] + g[HALO:HALO + tm] * cw_ref[1:2, cols]
              + g_next * cw_ref[2:3, cols])
        up = _dot(hcat_ref[HALO:HALO + tm, :], wgu_ref[:, up_cols])
        act_ref[:, cols] = ((gc * jax.nn.sigmoid(gc)) * up).astype(_BF16)
    for s in range(tm // FFN_SUB_ROWS):
        sub = slice(s * FFN_SUB_ROWS, (s + 1) * FFN_SUB_ROWS)
        y = _dot(act_ref[sub, :], wd_ref[...])
        o_ref[sub, :] = _norm(ALPHA * x_ref[sub, :] + m[5:6] * y) * g_ref[...] + b_ref[...]


def _conv_ffn(x1, mod, w_gu, conv_w, conv_b, w_down, ln_g, ln_b, *, layer, mod_row, per_seq_mod):
    B, L, _ = x1.shape
    n_tok = B * L
    tm = FFN_ROW_TILE
    assert n_tok % tm == 0 and (L % tm == 0 or (tm % L == 0 and not per_seq_mod))
    x1 = x1.reshape(n_tok, D_MODEL)
    row_spec = pl.BlockSpec((tm, D_MODEL), lambda t: (t, 0))
    per = tm // HALO
    last = n_tok // HALO - 1
    prev_spec = pl.BlockSpec((HALO, D_MODEL), lambda t: (jnp.maximum(t * per - 1, 0), 0))
    next_spec = pl.BlockSpec((HALO, D_MODEL), lambda t: (jnp.minimum((t + 1) * per, last), 0))
    mod_idx = (lambda l, r: (lambda t: (l, r + (t * tm) // L, 0, 0))) if per_seq_mod else None
    out = pl.pallas_call(
        functools.partial(_ffn_kernel, seq_len=L),
        out_shape=jax.ShapeDtypeStruct((n_tok, D_MODEL), _F32),
        grid=(n_tok // tm,),
        in_specs=[
            row_spec, prev_spec, next_spec,
            _mod_spec(layer, mod_row, mod_idx),
            _const_spec((D_MODEL, 2 * D_FF)),
            _layer_spec((3, D_FF), layer),
            _layer_spec((1, D_FF), layer),
            _const_spec((D_FF, D_MODEL)),
            _layer_spec((1, D_MODEL), layer),
            _layer_spec((1, D_MODEL), layer),
        ],
        out_specs=row_spec,
        scratch_shapes=[pltpu.VMEM((tm + 2 * HALO, D_MODEL), _BF16), pltpu.VMEM((tm, D_FF), _BF16)],
        compiler_params=_params(("parallel",)),
        name="conv_ffn",
    )(x1, x1, x1, mod, w_gu, conv_w, conv_b, w_down, ln_g, ln_b)
    return out.reshape(B, L, D_MODEL)


def _dft_tables(n):
    idx = np.arange(n)
    ang = 2.0 * np.pi * ((idx[:, None] * idx[None, :]) % n) / n
    return np.cos(ang) / np.sqrt(n), np.sin(ang) / np.sqrt(n)


def _pos_dft_tables(n):
    cos, sin = _dft_tables(n)
    ti = _dft_tile_rows(n)
    tiles = lambda tab: np.stack([tab[t * ti:t * ti + ti + HALO] for t in range((n // 2) // ti)])
    rev = np.zeros((ti, ti + HALO))
    rev[np.arange(ti), ti - np.arange(ti)] = 1.0
    return tiles(cos), tiles(sin), rev


def _block_diag(blocks):
    n = len(blocks)
    d = blocks[0].shape[0]
    out = np.zeros((n * d, n * d), blocks[0].dtype)
    for g, blk in enumerate(blocks):
        out[g * d:(g + 1) * d, g * d:(g + 1) * d] = blk
    return out


def kernel(x_prompt, x_sample, cache_k, cache_v, c, c_ctx, w_ada, b_ada, w_in, a_ws, a_bs, f_w, rpb,
           w_out, ln1_g, ln1_b, w_gu, conv_w, conv_b, w_down, ln2_g, ln2_b):
    n_ctx_b, ctx_len, _ = x_prompt.shape
    n_lat_b, lat_len, _ = x_sample.shape
    past_len = cache_k.shape[2]

    cs = jnp.zeros((MOD_ROWS, D_MODEL), _F32).at[0].set(c_ctx).at[1:1 + n_lat_b].set(c)
    mod = _modulation(cs, w_ada, b_ada).reshape(DEPTH, MOD_ROWS, 6, D_MODEL)

    cc, sc = _dft_tables(B_GROUP_DIM)
    ch_cos = jnp.asarray(_block_diag([cc] * B_GROUPS), _F32).astype(_BF16)
    ch_sin = jnp.asarray(_block_diag([sc] * B_GROUPS), _F32).astype(_BF16)
    pos_tabs = {n: tuple(jnp.asarray(t, _F32).astype(_BF16) for t in _pos_dft_tables(n))
                for n in (ctx_len, lat_len)}

    w_in_l = w_in[0].astype(_BF16)
    ws = a_ws.reshape(DEPTH, A_HEADS * CHUNK, CHUNK).astype(_BF16)
    bs = jnp.repeat(a_bs.transpose(0, 2, 1), A_HEAD_DIM, axis=2)
    group_eye = jnp.eye(B_GROUPS, dtype=_F32)[None, :, None, :, None]
    fw = (f_w[:, :, :, None, :] * group_eye).reshape(DEPTH, D_B, D_B).astype(_BF16)
    vec = lambda p: p.reshape(DEPTH, 1, p.shape[-1])
    g1, b1, g2, b2, cb = vec(ln1_g), vec(ln1_b), vec(ln2_g), vec(ln2_b), vec(conv_b)
    bias = _na_bias_table(rpb)
    heads_first = (0, 1, 3, 4, 2)
    kct = cache_k.transpose(heads_first).reshape(n_lat_b, DEPTH, D_C, past_len)
    vct = cache_v.transpose(heads_first).reshape(n_lat_b, DEPTH, D_C, past_len)

    xp, xs = x_prompt, x_sample
    new_k, new_v = [], []
    for l in range(DEPTH):
        ctx = dict(layer=l, mod_row=0, per_seq_mod=False)
        lat = dict(layer=l, mod_row=1, per_seq_mod=True)
        cast = [(w_gu, l), (w_down, l), (w_out, l)] + ([(w_in, l + 1)] if l + 1 < DEPTH else [])
        lat_out = _in_proj(xs, mod, w_in_l, ch_cos, ch_sin, cast, kv_transposed=False, **lat)
        lat_proj, (w_gu_l, w_down_l, w_out_l) = lat_out[:7], lat_out[7:10]
        ctx_proj = _in_proj(xp, mod, w_in_l, ch_cos, ch_sin,
                            prev_kv=list(zip(new_k, new_v)) if l == DEPTH - 1 else None,
                            kv_transposed=True, **ctx)

        def mix_and_ffn(x, proj, attend, where):
            u, vn, zc, zs, q, k, v = proj
            four_top, four_bot = _pos_dft(*pos_tabs[x.shape[1]], zc, zs)
            attn = attend(q, k, v)
            x1 = _out_proj(x, u, vn, four_top, four_bot, attn, mod, ws, bs, fw, w_out_l, g1, b1, **where)
            return _conv_ffn(x1, mod, w_gu_l, conv_w, cb, w_down_l, g2, b2, **where)

        new_k.append(ctx_proj[5])
        new_v.append(ctx_proj[6])
        xp = mix_and_ffn(xp, ctx_proj, _ctx_attention, ctx)
        xs = mix_and_ffn(xs, lat_proj, functools.partial(_na_attention, kct=kct, vct=vct, bias=bias, layer=l),
                         lat)
        if l + 1 < DEPTH:
            w_in_l = lat_out[10]

    def kv_out(stacked):
        return stacked.reshape(n_ctx_b, DEPTH, C_HEADS, C_HEAD_DIM, ctx_len).transpose(0, 1, 4, 2, 3)

    return (xp, xs, kv_out(new_k[-1]), kv_out(new_v[-1]))
```

```python
import functools

import numpy as np
import jax
import jax.numpy as jnp
from jax import lax
from jax.experimental import pallas as pl
from jax.experimental.pallas import tpu as pltpu

D_MODEL = 1024
DEPTH = 2
GRID_W = 64
D_A = 256
A_HEADS = 4
A_HEAD_DIM = 64
CHUNK = 128
D_B = 256
B_GROUPS = 4
B_GROUP_DIM = 64
D_C = 512
C_HEAD_DIM = 64
C_HEADS = 8
P_IN = 2 * D_A + D_B + 3 * D_C
NA_ROWS = 8
NA_KW = 16
D_FF = 2816
ALPHA = (2 * DEPTH) ** 0.25
LN_EPS = 1e-6
LOG2_E = 1.4426950408889634
Q_SCALE = C_HEAD_DIM ** -0.5 * LOG2_E

LANES = 128
V7X_VMEM_BYTES = 64 * 1024 * 1024
VMEM_LIMIT = V7X_VMEM_BYTES * 7 // 8
HEAD_PAIR = 2 * C_HEAD_DIM
N_HEAD_PAIRS = C_HEADS // 2
MOD_ROWS = 16
MOD_COL_TILE = 3072
CTX_ATTN_SEQS = 4
IN_ROW_TILE = 1024
IN_SUB_ROWS = 256
OUT_ROW_TILE = 1024
OUT_SUB_ROWS = 256
FFN_ROW_TILE = 1024
FFN_SUB_ROWS = 256
FF_CHUNK = 256
HALO = 16
DFT_ROW_TILE = 512
DFT_COL_TILE = 1024

_F32 = jnp.float32
_BF16 = jnp.bfloat16


def _dot(a, b):
    return jnp.dot(a, b, preferred_element_type=_F32)


def _dot_nt(a, b):
    return lax.dot_general(a, b, (((1,), (1,)), ((), ())), preferred_element_type=_F32)


def _norm(x):
    mu = jnp.mean(x, axis=-1, keepdims=True)
    xc = x - mu
    var = jnp.mean(xc * xc, axis=-1, keepdims=True)
    return xc * lax.rsqrt(var + LN_EPS)


def _params(sem):
    return pltpu.CompilerParams(dimension_semantics=sem, vmem_limit_bytes=VMEM_LIMIT)


def _const_spec(shape):
    nd = len(shape)
    return pl.BlockSpec(shape, lambda *_: (0,) * nd, pipeline_mode=pl.Buffered(1))


def _layer_spec(shape, layer):
    nd = len(shape)
    return pl.BlockSpec((None,) + shape, lambda *_: (layer,) + (0,) * nd, pipeline_mode=pl.Buffered(1))


def _mod_spec(layer, first_row, rows_per_seq):
    if rows_per_seq is None:
        return pl.BlockSpec((None, None, 6, D_MODEL), lambda *_: (layer, first_row, 0, 0))
    return pl.BlockSpec((None, None, 6, D_MODEL), rows_per_seq(layer, first_row))


def _mod_kernel(c_ref, w_ref, b_ref, o_ref):
    c = c_ref[...]
    s = c * jax.nn.sigmoid(c)
    o_ref[0] = _dot(s.astype(_BF16), w_ref[0].astype(_BF16)) + b_ref[0]


def _modulation(cs, w_ada, b_ada):
    n = cs.shape[0]
    tn = MOD_COL_TILE
    return pl.pallas_call(
        _mod_kernel,
        out_shape=jax.ShapeDtypeStruct((DEPTH, n, 6 * D_MODEL), _F32),
        grid=(DEPTH, 6 * D_MODEL // tn),
        in_specs=[
            pl.BlockSpec((n, D_MODEL), lambda l, j: (0, 0)),
            pl.BlockSpec((1, D_MODEL, tn), lambda l, j: (l, 0, j)),
            pl.BlockSpec((1, 1, tn), lambda l, j: (l, 0, j)),
        ],
        out_specs=pl.BlockSpec((1, n, tn), lambda l, j: (l, 0, j)),
        compiler_params=_params(("parallel", "parallel")),
        name="adaln_mod",
    )(cs, w_ada, b_ada.reshape(DEPTH, 1, 6 * D_MODEL))


def _tile_split(B, L, tile_rows):
    if L >= tile_rows:
        assert L % tile_rows == 0
        return 1, tile_rows
    nb = min(tile_rows // L, B)
    assert B % nb == 0
    return nb, L


def _in_kernel(x_ref, mod_ref, w_ref, cc_ref, cs_ref, *refs, kv_transposed, n_cast, kv_slot):
    n_prev = 0 if kv_slot is None else 2 * kv_slot
    cast_in, prev_kv = refs[:n_cast], refs[n_cast:n_cast + n_prev]
    u_ref, vn_ref, zc_ref, zs_ref, q_ref, k_ref, v_ref = refs[n_cast + n_prev:n_cast + n_prev + 7]
    cast_out = refs[n_cast + n_prev + 7:]
    for src, dst in zip(cast_in, cast_out):
        dst[...] = src[...].astype(dst.dtype)
    for j in range(n_prev // 2):
        k_ref[:, j] = prev_kv[2 * j][...]
        v_ref[:, j] = prev_kv[2 * j + 1][...]
    nb, tr, _ = x_ref.shape
    m = mod_ref[...]
    o1 = 2 * D_A
    o2 = o1 + D_B
    for t in range(nb * tr // IN_SUB_ROWS):
        seq = (t * IN_SUB_ROWS) // tr
        rows = slice((t * IN_SUB_ROWS) % tr, (t * IN_SUB_ROWS) % tr + IN_SUB_ROWS)
        h = _norm(x_ref[seq, rows, :]) * (1.0 + m[1:2]) + m[0:1]
        z = _dot(h.astype(_BF16), w_ref[...])
        za = jax.nn.gelu(z[:, :o1])
        u_ref[seq, rows, :] = za[:, :D_A].astype(u_ref.dtype)
        vn_ref[seq, rows, :] = _norm(za[:, D_A:]).astype(_BF16)
        zb = z[:, o1:o2].astype(_BF16)
        zc_ref[rows, seq * D_B:(seq + 1) * D_B] = _dot(zb, cc_ref[...]).astype(_BF16)
        zs_ref[rows, seq * D_B:(seq + 1) * D_B] = _dot(zb, cs_ref[...]).astype(_BF16)
        q_ref[seq, rows, :] = (z[:, o2:o2 + D_C] * Q_SCALE).astype(q_ref.dtype)
        k = z[:, o2 + D_C:o2 + 2 * D_C]
        v = z[:, o2 + 2 * D_C:o2 + 3 * D_C]
        if kv_transposed and kv_slot is not None:
            k_ref[seq, kv_slot, :, rows] = k.T.astype(k_ref.dtype)
            v_ref[seq, kv_slot, :, rows] = v.T.astype(v_ref.dtype)
        elif kv_transposed:
            k_ref[seq, :, rows] = k.T.astype(k_ref.dtype)
            v_ref[seq, :, rows] = v.T.astype(v_ref.dtype)
        else:
            k_ref[seq, rows, :] = k.astype(k_ref.dtype)
            v_ref[seq, rows, :] = v.astype(v_ref.dtype)


def _in_proj(x, mod, w_in, ch_cos, ch_sin, cast=(), prev_kv=None, *, layer, mod_row, per_seq_mod,
             kv_transposed):
    B, L, _ = x.shape
    nb, tr = _tile_split(B, L, IN_ROW_TILE)
    assert nb == 1 or not per_seq_mod
    n_i = L // tr
    n_steps = (B // nb) * n_i
    cast_in_specs, cast_out_specs, cast_shapes = [], [], []
    for p, p_layer in cast:
        _, R, C = p.shape
        assert R % (n_steps * HALO) == 0
        cast_in_specs.append(pl.BlockSpec((None, R // n_steps, C),
                                          functools.partial(lambda pl_, g, i: (pl_, g * n_i + i, 0), p_layer)))
        cast_out_specs.append(pl.BlockSpec((R // n_steps, C), lambda g, i: (g * n_i + i, 0)))
        cast_shapes.append(jax.ShapeDtypeStruct((R, C), _BF16))
    row = lambda w, dt: jax.ShapeDtypeStruct((B, L, w), dt)
    row_spec = lambda w: pl.BlockSpec((nb, tr, w), lambda g, i: (g, i, 0))
    four = jax.ShapeDtypeStruct((L, B * D_B), _BF16)
    four_spec = pl.BlockSpec((tr, nb * D_B), lambda g, i: (i, g))
    mod_idx = (lambda l, r: (lambda g, i: (l, r + g, 0, 0))) if per_seq_mod else None
    prev_specs, prev_args, kv_slot = [], [], None
    if kv_transposed and prev_kv is not None:
        kv_slot = len(prev_kv)
        kv = jax.ShapeDtypeStruct((B, kv_slot + 1, D_C, L), _F32)
        kv_spec = pl.BlockSpec((nb, kv_slot + 1, D_C, tr), lambda g, i: (g, 0, 0, i))
        for pair in prev_kv:
            prev_specs += [pl.BlockSpec((nb, D_C, tr), lambda g, i: (g, 0, i))] * 2
            prev_args += list(pair)
    elif kv_transposed:
        kv = jax.ShapeDtypeStruct((B, D_C, L), _F32)
        kv_spec = pl.BlockSpec((nb, D_C, tr), lambda g, i: (g, 0, i))
    else:
        kv = row(D_C, _BF16)
        kv_spec = row_spec(D_C)
    return pl.pallas_call(
        functools.partial(_in_kernel, kv_transposed=kv_transposed, n_cast=len(cast), kv_slot=kv_slot),
        out_shape=(row(D_A, _BF16), row(D_A, _BF16), four, four, row(D_C, _BF16), kv, kv, *cast_shapes),
        grid=(B // nb, n_i),
        in_specs=[
            row_spec(D_MODEL),
            _mod_spec(layer, mod_row, mod_idx),
            _const_spec((D_MODEL, P_IN)),
            _const_spec((D_B, D_B)),
            _const_spec((D_B, D_B)),
            *cast_in_specs, *prev_specs,
        ],
        out_specs=(row_spec(D_A), row_spec(D_A), four_spec, four_spec, row_spec(D_C), kv_spec, kv_spec,
                   *cast_out_specs),
        compiler_params=_params(("parallel", "parallel")),
        name="in_proj",
    )(x, mod, w_in, ch_cos, ch_sin, *[p for p, _ in cast], *prev_args)


def _dft_kernel(c_ref, s_ref, zc_ref, zs_ref, rev_ref, top_ref, bot_ref):
    p = _dot(c_ref[0], zc_ref[...])
    q = _dot(s_ref[0], zs_ref[...])
    top_ref[...] = (p - q)[:top_ref.shape[0]].astype(top_ref.dtype)
    bot_ref[...] = _dot(rev_ref[...], (p + q).astype(_BF16)).astype(bot_ref.dtype)


def _dft_tile_rows(L):
    return min(L // 2, DFT_ROW_TILE)


def _pos_dft(pos_cos, pos_sin, rev, zc, zs):
    L, N = zc.shape
    ti = _dft_tile_rows(L)
    n_t = (L // 2) // ti
    tj = DFT_COL_TILE
    half = jax.ShapeDtypeStruct((L // 2, N), _BF16)
    tab_spec = pl.BlockSpec((1, ti + HALO, L), lambda t, j: (t, 0, 0))
    return pl.pallas_call(
        _dft_kernel,
        out_shape=(half, half),
        grid=(n_t, N // tj),
        in_specs=[
            tab_spec, tab_spec,
            pl.BlockSpec((L, tj), lambda t, j: (0, j)),
            pl.BlockSpec((L, tj), lambda t, j: (0, j)),
            pl.BlockSpec((ti, ti + HALO), lambda t, j: (0, 0)),
        ],
        out_specs=(pl.BlockSpec((ti, tj), lambda t, j: (t, j)),
                   pl.BlockSpec((ti, tj), lambda t, j: (n_t - 1 - t, j))),
        compiler_params=_params(("parallel", "parallel")),
        name="pos_dft",
    )(pos_cos, pos_sin, zc, zs, rev)


def _head_lane_mask(rows, head):
    lane = lax.broadcasted_iota(jnp.int32, (rows, HEAD_PAIR), 1)
    return (lane >= head * C_HEAD_DIM) & (lane < (head + 1) * C_HEAD_DIM)


def _ctx_attn_kernel(q_ref, kt_ref, vt_ref, o_ref):
    n_seq, rows, _ = q_ref.shape
    sel0 = _head_lane_mask(rows, 0)
    for seq in range(n_seq):
        for pair in range(N_HEAD_PAIRS):
            lanes = slice(pair * HEAD_PAIR, (pair + 1) * HEAD_PAIR)
            q = q_ref[seq, :, lanes]
            kt = kt_ref[seq, lanes, :].astype(_BF16)
            vt = vt_ref[seq, lanes, :].astype(_BF16)
            lhs = jnp.concatenate([jnp.where(sel0, q, jnp.zeros_like(q)),
                                   jnp.where(sel0, jnp.zeros_like(q), q)], axis=0)
            s = _dot(lhs, kt)
            p = jnp.exp2(s - jnp.max(s, axis=-1, keepdims=True))
            o = _dot_nt(p.astype(_BF16), vt) / jnp.sum(p, axis=-1, keepdims=True)
            o_ref[seq, :, lanes] = jnp.where(sel0, o[:rows], o[rows:]).astype(o_ref.dtype)


def _ctx_attention(q, kt, vt):
    B, L, _ = q.shape
    nb = min(CTX_ATTN_SEQS, B)
    assert B % nb == 0
    spec = pl.BlockSpec((nb, L, D_C), lambda b: (b, 0, 0))
    if kt.ndim == 4:
        slot = kt.shape[1] - 1
        spec_t = pl.BlockSpec((nb, None, D_C, L), lambda b: (b, slot, 0, 0))
    else:
        spec_t = pl.BlockSpec((nb, D_C, L), lambda b: (b, 0, 0))
    return pl.pallas_call(
        _ctx_attn_kernel,
        out_shape=jax.ShapeDtypeStruct((B, L, D_C), _BF16),
        grid=(B // nb,),
        in_specs=[spec, spec_t, spec_t],
        out_specs=spec,
        compiler_params=_params(("parallel",)),
        name="ctx_attention",
    )(q, kt, vt)


def _na_kernel(q_ref, k_ref, v_ref, kct_ref, vct_ref, bias_ref, o_ref, *, n_rows):
    kct = kct_ref[...].astype(_BF16)
    vct = vct_ref[...].astype(_BF16)
    win_keys = NA_ROWS * GRID_W
    pair_rows = 2 * GRID_W
    n_q = n_rows * GRID_W

    windows = []
    for r in range(n_rows):
        rs = min(max(r - NA_ROWS // 2, 0), n_rows - NA_ROWS)
        windows.append((slice(rs * GRID_W, rs * GRID_W + win_keys), rs - r + NA_ROWS - 1))

    q = q_ref[0]
    sel0 = _head_lane_mask(n_q, 0)
    q_heads = (jnp.where(sel0, q, jnp.zeros_like(q)), jnp.where(sel0, jnp.zeros_like(q), q))
    lhs = jnp.concatenate([qh[r * GRID_W:(r + 1) * GRID_W]
                           for r in range(n_rows) for qh in q_heads], axis=0)
    s_ctx = _dot(lhs, kct)
    s_loc = []
    for r, (win, base) in enumerate(windows):
        bias = jnp.concatenate(
            [jnp.concatenate([bias_ref[head, base + 2 * t] for t in range(NA_ROWS // 2)], axis=1)
             for head in range(2)], axis=0)
        s_loc.append(_dot_nt(lhs[r * pair_rows:(r + 1) * pair_rows], k_ref[0, win, :]) + bias)
    s_loc = jnp.concatenate(s_loc, axis=0)
    m = jnp.maximum(jnp.max(s_loc, axis=-1, keepdims=True), jnp.max(s_ctx, axis=-1, keepdims=True))
    p_loc = jnp.exp2(s_loc - m)
    p_ctx = jnp.exp2(s_ctx - m)
    denom = jnp.sum(p_loc, axis=-1, keepdims=True) + jnp.sum(p_ctx, axis=-1, keepdims=True)
    p_loc = p_loc.astype(_BF16)
    o_loc = jnp.concatenate(
        [_dot(p_loc[r * pair_rows:(r + 1) * pair_rows], v_ref[0, win, :])
         for r, (win, _) in enumerate(windows)], axis=0)
    o = (o_loc + _dot_nt(p_ctx.astype(_BF16), vct)) / denom
    row_sel0 = _head_lane_mask(GRID_W, 0)
    o_ref[0] = jnp.concatenate(
        [jnp.where(row_sel0, o[r * pair_rows:r * pair_rows + GRID_W],
                   o[r * pair_rows + GRID_W:(r + 1) * pair_rows])
         for r in range(n_rows)], axis=0).astype(o_ref.dtype)


def _na_bias_kernel(v_ref, o_ref):
    n_ri = 2 * NA_ROWS - 1
    qcol = lax.broadcasted_iota(jnp.int32, (GRID_W, LANES), 0)
    lane = lax.broadcasted_iota(jnp.int32, (GRID_W, LANES), 1)
    kcol = lane & (GRID_W - 1)
    start = jnp.clip(qcol - NA_KW // 2, 0, GRID_W - NA_KW)
    in_win = (kcol >= start) & (kcol < start + NA_KW)
    for head in range(v_ref.shape[0]):
        rows = [jnp.broadcast_to(v_ref[head, ri], (GRID_W, LANES)) for ri in range(n_ri)]
        for ri in range(n_ri - 1):
            left = pltpu.roll(rows[ri], GRID_W + 1, 1, stride=1, stride_axis=0)
            right = pltpu.roll(rows[ri + 1], 1, 1, stride=1, stride_axis=0)
            o_ref[head, ri] = jnp.where(in_win, jnp.where(lane < GRID_W, left, right) * LOG2_E, -jnp.inf)


def _na_bias_table(rpb):
    n_ri = 2 * NA_ROWS - 1
    lo = GRID_W - NA_KW
    v = jnp.pad(rpb, ((0, 0), (0, 0), (0, 0), (lo, LANES - lo - (2 * NA_KW - 1))))
    v = v.reshape(DEPTH * C_HEADS, n_ri, 1, LANES)
    return pl.pallas_call(
        _na_bias_kernel,
        out_shape=jax.ShapeDtypeStruct((DEPTH * C_HEADS, n_ri - 1, GRID_W, LANES), _F32),
        grid=(DEPTH,),
        in_specs=[pl.BlockSpec((C_HEADS, n_ri, 1, LANES), lambda i: (i, 0, 0, 0))],
        out_specs=pl.BlockSpec((C_HEADS, n_ri - 1, GRID_W, LANES), lambda i: (i, 0, 0, 0)),
        compiler_params=_params(("parallel",)),
        name="na_bias",
    )(v)


def _na_attention(q, k, v, kct, vct, bias, *, layer):
    B, L, _ = q.shape
    Lc = kct.shape[3]
    n_rows = L // GRID_W
    assert n_rows >= NA_ROWS and 2 * GRID_W == LANES
    lat = pl.BlockSpec((1, L, HEAD_PAIR), lambda p, b: (b, 0, p))
    ctx = pl.BlockSpec((None, None, HEAD_PAIR, Lc), lambda p, b: (b, layer, p, 0))
    return pl.pallas_call(
        functools.partial(_na_kernel, n_rows=n_rows),
        out_shape=jax.ShapeDtypeStruct((B, L, D_C), _BF16),
        grid=(N_HEAD_PAIRS, B),
        in_specs=[lat, lat, lat, ctx, ctx,
                  pl.BlockSpec((2, 2 * NA_ROWS - 2, GRID_W, LANES),
                               lambda p, b: (layer * N_HEAD_PAIRS + p, 0, 0, 0))],
        out_specs=lat,
        compiler_params=_params(("parallel", "parallel")),
        name="na_attention",
    )(q, k, v, kct, vct, bias)


def _out_kernel(x_ref, u_ref, vn_ref, ft_ref, fb_ref, a_ref, mod_ref, ws_ref, bs_ref, fw_ref, wo_ref,
                g_ref, b_ref, x1_ref, *, tiles_per_half):
    nb, tr, _ = x_ref.shape
    lane = lax.broadcasted_iota(jnp.int32, (CHUNK, D_A), 1)
    m = mod_ref[...]
    if tiles_per_half == 0:
        four = jnp.concatenate([half[:, s * D_B:(s + 1) * D_B]
                                for s in range(nb) for half in (ft_ref, fb_ref)], axis=0)
    else:
        four = jnp.where(pl.program_id(1) < tiles_per_half, ft_ref[...], fb_ref[...])

    def rows_of(ref, start, n):
        return ref[start // tr, start % tr:start % tr + n, :]

    for s in range(nb * tr // OUT_SUB_ROWS):
        sub = slice(s * OUT_SUB_ROWS, (s + 1) * OUT_SUB_ROWS)
        ya = []
        for n in range(OUT_SUB_ROWS // CHUNK):
            start = s * OUT_SUB_ROWS + n * CHUNK
            r = _dot(ws_ref[...], rows_of(vn_ref, start, CHUNK))
            mixed = r[:CHUNK]
            for hd in range(1, A_HEADS):
                mixed = jnp.where(lane >= hd * A_HEAD_DIM, r[hd * CHUNK:(hd + 1) * CHUNK], mixed)
            ya.append(rows_of(u_ref, start, CHUNK) * (mixed + bs_ref[...]))
        ya = jnp.concatenate(ya, axis=0).astype(_BF16)
        yb = _dot(four[sub], fw_ref[...]).astype(_BF16)
        y = _dot(jnp.concatenate([ya, yb, rows_of(a_ref, s * OUT_SUB_ROWS, OUT_SUB_ROWS)], axis=1), wo_ref[...])
        x1 = (_norm(ALPHA * rows_of(x_ref, s * OUT_SUB_ROWS, OUT_SUB_ROWS) + m[2:3] * y) * g_ref[...]
              + b_ref[...])
        off = (s * OUT_SUB_ROWS) % tr
        x1_ref[(s * OUT_SUB_ROWS) // tr, off:off + OUT_SUB_ROWS, :] = x1


def _out_proj(x, u, vn, four_top, four_bot, attn, mod, ws, bs, fw, w_out, ln_g, ln_b, *,
              layer, mod_row, per_seq_mod):
    B, L, _ = x.shape
    nb, tr = _tile_split(B, L, OUT_ROW_TILE)
    assert (nb == 1 or not per_seq_mod) and tr % OUT_SUB_ROWS == 0
    row_spec = lambda w: pl.BlockSpec((nb, tr, w), lambda g, i: (g, i, 0))
    mod_idx = (lambda l, r: (lambda g, i: (l, r + g, 0, 0))) if per_seq_mod else None
    nh = (L // 2) // tr
    assert tr == L or (L // 2) % tr == 0
    if nh == 0:
        top_spec = bot_spec = pl.BlockSpec((L // 2, nb * D_B), lambda g, i: (0, g))
    else:
        top_spec = pl.BlockSpec((tr, D_B), lambda g, i: (jnp.minimum(i, nh - 1), g))
        bot_spec = pl.BlockSpec((tr, D_B), lambda g, i: (jnp.maximum(i - nh, 0), g))
    return pl.pallas_call(
        functools.partial(_out_kernel, tiles_per_half=nh),
        out_shape=jax.ShapeDtypeStruct((B, L, D_MODEL), _F32),
        grid=(B // nb, L // tr),
        in_specs=[
            row_spec(D_MODEL), row_spec(D_A), row_spec(D_A), top_spec, bot_spec, row_spec(D_C),
            _mod_spec(layer, mod_row, mod_idx),
            _layer_spec((A_HEADS * CHUNK, CHUNK), layer),
            _layer_spec((CHUNK, D_A), layer),
            _layer_spec((D_B, D_B), layer),
            _const_spec((D_MODEL, D_MODEL)),
            _layer_spec((1, D_MODEL), layer),
            _layer_spec((1, D_MODEL), layer),
        ],
        out_specs=row_spec(D_MODEL),
        compiler_params=_params(("parallel", "parallel")),
        name="out_proj",
    )(x, u, vn, four_top, four_bot, attn, mod, ws, bs, fw, w_out, ln_g, ln_b)


def _ffn_kernel(x_ref, xp_ref, xn_ref, mod_ref, wgu_ref, cw_ref, cb_ref, wd_ref,
                g_ref, b_ref, o_ref, hcat_ref, act_ref, *, seq_len):
    tm = x_ref.shape[0]
    ext = tm + 2 * HALO
    inside_one_seq = seq_len % tm == 0
    m = mod_ref[...]

    def pre_norm(x):
        return (_norm(x) * (1.0 + m[4:5]) + m[3:4]).astype(_BF16)

    h_prev = pre_norm(xp_ref[...])
    h_next = pre_norm(xn_ref[...])
    if inside_one_seq:
        t = pl.program_id(0) % (seq_len // tm)
        hcat_ref[:HALO, :] = jnp.where(t > 0, h_prev, jnp.zeros_like(h_prev))
        hcat_ref[HALO + tm:, :] = jnp.where(t < seq_len // tm - 1, h_next, jnp.zeros_like(h_next))
    else:
        assert seq_len & (seq_len - 1) == 0
        hcat_ref[:HALO, :] = h_prev
        hcat_ref[HALO + tm:, :] = h_next
        pos = lax.broadcasted_iota(jnp.int32, (tm, 1), 0) & (seq_len - 1)
        has_prev = pos != 0
        has_next = pos != seq_len - 1
    for s in range(tm // FFN_SUB_ROWS):
        sub = slice(s * FFN_SUB_ROWS, (s + 1) * FFN_SUB_ROWS)
        hcat_ref[HALO + s * FFN_SUB_ROWS:HALO + (s + 1) * FFN_SUB_ROWS, :] = pre_norm(x_ref[sub, :])
    for j in range(D_FF // FF_CHUNK):
        cols = slice(j * FF_CHUNK, (j + 1) * FF_CHUNK)
        up_cols = slice(D_FF + j * FF_CHUNK, D_FF + (j + 1) * FF_CHUNK)
        g = _dot(hcat_ref[...], wgu_ref[:, cols])
        g_prev = pltpu.roll(g, 1, 0)[HALO:HALO + tm]
        g_next = pltpu.roll(g, ext - 1, 0)[HALO:HALO + tm]
        if not inside_one_seq:
            g_prev = jnp.where(has_prev, g_prev, 0.0)
            g_next = jnp.where(has_next, g_next, 0.0)
        gc = (cb_ref[:, cols] + g_prev * cw_ref[0:1, cols] + g[HALO:HALO + tm] * cw_ref[1:2, cols]
              + g_next * cw_ref[2:3, cols])
        up = _dot(hcat_ref[HALO:HALO + tm, :], wgu_ref[:, up_cols])
        act_ref[:, cols] = ((gc * jax.nn.sigmoid(gc)) * up).astype(_BF16)
    for s in range(tm // FFN_SUB_ROWS):
        sub = slice(s * FFN_SUB_ROWS, (s + 1) * FFN_SUB_ROWS)
        y = _dot(act_ref[sub, :], wd_ref[...])
        o_ref[sub, :] = _norm(ALPHA * x_ref[sub, :] + m[5:6] * y) * g_ref[...] + b_ref[...]


def _conv_ffn(x1, mod, w_gu, conv_w, conv_b, w_down, ln_g, ln_b, *, layer, mod_row, per_seq_mod):
    B, L, _ = x1.shape
    n_tok = B * L
    tm = FFN_ROW_TILE
    assert n_tok % tm == 0 and (L % tm == 0 or (tm % L == 0 and not per_seq_mod))
    x1 = x1.reshape(n_tok, D_MODEL)
    row_spec = pl.BlockSpec((tm, D_MODEL), lambda t: (t, 0))
    per = tm // HALO
    last = n_tok // HALO - 1
    prev_spec = pl.BlockSpec((HALO, D_MODEL), lambda t: (jnp.maximum(t * per - 1, 0), 0))
    next_spec = pl.BlockSpec((HALO, D_MODEL), lambda t: (jnp.minimum((t + 1) * per, last), 0))
    mod_idx = (lambda l, r: (lambda t: (l, r + (t * tm) // L, 0, 0))) if per_seq_mod else None
    out = pl.pallas_call(
        functools.partial(_ffn_kernel, seq_len=L),
        out_shape=jax.ShapeDtypeStruct((n_tok, D_MODEL), _F32),
        grid=(n_tok // tm,),
        in_specs=[
            row_spec, prev_spec, next_spec,
            _mod_spec(layer, mod_row, mod_idx),
            _const_spec((D_MODEL, 2 * D_FF)),
            _layer_spec((3, D_FF), layer),
            _layer_spec((1, D_FF), layer),
            _const_spec((D_FF, D_MODEL)),
            _layer_spec((1, D_MODEL), layer),
            _layer_spec((1, D_MODEL), layer),
        ],
        out_specs=row_spec,
        scratch_shapes=[pltpu.VMEM((tm + 2 * HALO, D_MODEL), _BF16), pltpu.VMEM((tm, D_FF), _BF16)],
        compiler_params=_params(("parallel",)),
        name="conv_ffn",
    )(x1, x1, x1, mod, w_gu, conv_w, conv_b, w_down, ln_g, ln_b)
    return out.reshape(B, L, D_MODEL)


def _dft_tables(n):
    idx = np.arange(n)
    ang = 2.0 * np.pi * ((idx[:, None] * idx[None, :]) % n) / n
    return np.cos(ang) / np.sqrt(n), np.sin(ang) / np.sqrt(n)


def _pos_dft_tables(n):
    cos, sin = _dft_tables(n)
    ti = _dft_tile_rows(n)
    tiles = lambda tab: np.stack([tab[t * ti:t * ti + ti + HALO] for t in range((n // 2) // ti)])
    rev = np.zeros((ti, ti + HALO))
    rev[np.arange(ti), ti - np.arange(ti)] = 1.0
    return tiles(cos), tiles(sin), rev


def _block_diag(blocks):
    n = len(blocks)
    d = blocks[0].shape[0]
    out = np.zeros((n * d, n * d), blocks[0].dtype)
    for g, blk in enumerate(blocks):
        out[g * d:(g + 1) * d, g * d:(g + 1) * d] = blk
    return out


def kernel(x_prompt, x_sample, cache_k, cache_v, c, c_ctx, w_ada, b_ada, w_in, a_ws, a_bs, f_w, rpb,
           w_out, ln1_g, ln1_b, w_gu, conv_w, conv_b, w_down, ln2_g, ln2_b):
    n_ctx_b, ctx_len, _ = x_prompt.shape
    n_lat_b, lat_len, _ = x_sample.shape
    past_len = cache_k.shape[2]

    cs = jnp.zeros((MOD_ROWS, D_MODEL), _F32).at[0].set(c_ctx).at[1:1 + n_lat_b].set(c)
    mod = _modulation(cs, w_ada, b_ada).reshape(DEPTH, MOD_ROWS, 6, D_MODEL)

    cc, sc = _dft_tables(B_GROUP_DIM)
    ch_cos = jnp.asarray(_block_diag([cc] * B_GROUPS), _F32).astype(_BF16)
    ch_sin = jnp.asarray(_block_diag([sc] * B_GROUPS), _F32).astype(_BF16)
    pos_tabs = {n: tuple(jnp.asarray(t, _F32).astype(_BF16) for t in _pos_dft_tables(n))
                for n in (ctx_len, lat_len)}

    w_in_l = w_in[0].astype(_BF16)
    ws = a_ws.reshape(DEPTH, A_HEADS * CHUNK, CHUNK).astype(_BF16)
    bs = jnp.repeat(a_bs.transpose(0, 2, 1), A_HEAD_DIM, axis=2)
    group_eye = jnp.eye(B_GROUPS, dtype=_F32)[None, :, None, :, None]
    fw = (f_w[:, :, :, None, :] * group_eye).reshape(DEPTH, D_B, D_B).astype(_BF16)
    vec = lambda p: p.reshape(DEPTH, 1, p.shape[-1])
    g1, b1, g2, b2, cb = vec(ln1_g), vec(ln1_b), vec(ln2_g), vec(ln2_b), vec(conv_b)
    bias = _na_bias_table(rpb)
    heads_first = (0, 1, 3, 4, 2)
    kct = cache_k.transpose(heads_first).reshape(n_lat_b, DEPTH, D_C, past_len)
    vct = cache_v.transpose(heads_first).reshape(n_lat_b, DEPTH, D_C, past_len)

    xp, xs = x_prompt, x_sample
    new_k, new_v = [], []
    for l in range(DEPTH):
        ctx = dict(layer=l, mod_row=0, per_seq_mod=False)
        lat = dict(layer=l, mod_row=1, per_seq_mod=True)
        cast = [(w_gu, l), (w_down, l), (w_out, l)] + ([(w_in, l + 1)] if l + 1 < DEPTH else [])
        lat_out = _in_proj(xs, mod, w_in_l, ch_cos, ch_sin, cast, kv_transposed=False, **lat)
        lat_proj, (w_gu_l, w_down_l, w_out_l) = lat_out[:7], lat_out[7:10]
        ctx_proj = _in_proj(xp, mod, w_in_l, ch_cos, ch_sin,
                            prev_kv=list(zip(new_k, new_v)) if l == DEPTH - 1 else None,
                            kv_transposed=True, **ctx)

        def mix_and_ffn(x, proj, attend, where):
            u, vn, zc, zs, q, k, v = proj
            four_top, four_bot = _pos_dft(*pos_tabs[x.shape[1]], zc, zs)
            attn = attend(q, k, v)
            x1 = _out_proj(x, u, vn, four_top, four_bot, attn, mod, ws, bs, fw, w_out_l, g1, b1, **where)
            return _conv_ffn(x1, mod, w_gu_l, conv_w, cb, w_down_l, g2, b2, **where)

        new_k.append(ctx_proj[5])
        new_v.append(ctx_proj[6])
        xp = mix_and_ffn(xp, ctx_proj, _ctx_attention, ctx)
        xs = mix_and_ffn(xs, lat_proj, functools.partial(_na_attention, kct=kct, vct=vct, bias=bias, layer=l),
                         lat)
        if l + 1 < DEPTH:
            w_in_l = lat_out[10]

    def kv_out(stacked):
        return stacked.reshape(n_ctx_b, DEPTH, C_HEADS, C_HEAD_DIM, ctx_len).transpose(0, 1, 4, 2, 3)

    return (xp, xs, kv_out(new_k[-1]), kv_out(new_v[-1]))
```
